```python
import math
import jax, jax.numpy as jnp
from jax import lax
import numpy as np

D_MODEL = 2048
BATCH = 2
SEQ = 4096
DEPTH = 4
DEC_BATCH = 32
DEC_SEQ = 4
PAST_LEN = 16384
PAGE_SIZE = 128

HEAD_DIM = 64
N_HEADS_A = D_MODEL // 128
N_KV_A = 4
GROUP_A = N_HEADS_A // N_KV_A
WIDTH_A = N_HEADS_A * HEAD_DIM
KV_WIDTH_A = N_KV_A * HEAD_DIM
WINDOW = 128
N_BUCKETS = 32
MAX_DISTANCE = 128
N_HEADS_B = 4
DK_B = 128
DV_B = 256
WIDTH_BK = N_HEADS_B * DK_B
WIDTH_BV = N_HEADS_B * DV_B
GATE_RANK = 16
GATE_TAU = 16.0
GLA_CHUNK = 16
D_FF = 5632
CONV_W = 3
EPS = 1e-6
SPLITS = (WIDTH_A, KV_WIDTH_A, KV_WIDTH_A, WIDTH_BK, WIDTH_BK, WIDTH_BV, WIDTH_BV, GATE_RANK, D_MODEL, D_MODEL)
IN_COLS = sum(SPLITS)

kernel_name = "hybrid_swa_sink_gla_convffn_step"


def rms_norm(x, g):
    xf = x.astype(jnp.float32)
    y = xf * lax.rsqrt(jnp.mean(xf * xf, axis=-1, keepdims=True) + EPS)
    return (y * g.astype(jnp.float32)).astype(x.dtype)


def t5_bucket(d):
    d = jnp.maximum(d, 0)
    max_exact = N_BUCKETS // 2
    df = jnp.maximum(d, 1).astype(jnp.float32)
    large = max_exact + (jnp.log(df / max_exact) / math.log(MAX_DISTANCE / max_exact)
                         * (N_BUCKETS - max_exact)).astype(jnp.int32)
    large = jnp.minimum(large, N_BUCKETS - 1)
    return jnp.where(d < max_exact, d, large)


def window_bias(rel_bias, d):
    bias = jnp.transpose(rel_bias[t5_bucket(d)], (2, 0, 1))
    mask = (d >= 0) & (d < WINDOW)
    return bias, mask


def swa_core(q, k, v, bias, mask, sinks):
    B, N, C = q.shape[:3]
    S = k.shape[2]
    qg = q.reshape(B, N, C, N_KV_A, GROUP_A, HEAD_DIM)
    s = jnp.einsum('bnqkgd,bnskd->bnkgqs', qg, k).astype(jnp.float32) * (HEAD_DIM ** -0.5)
    s = s + bias.reshape(N_KV_A, GROUP_A, C, S).astype(jnp.float32)
    s = jnp.where(mask[None, :, None, None], s, -jnp.inf)
    sink = jnp.broadcast_to(sinks.reshape(N_KV_A, GROUP_A, 1, 1).astype(jnp.float32), s.shape[:-1] + (1,))
    p = jax.nn.softmax(jnp.concatenate([s, sink], axis=-1), axis=-1)[..., :-1]
    o = jnp.einsum('bnkgqs,bnskd->bnqkgd', p.astype(v.dtype), v)
    return o.reshape(B, N, C, WIDTH_A)


def swa_prompt(q, k, v, rel_bias, sinks):
    B, L = q.shape[:2]
    N = L // WINDOW
    qb = q.reshape(B, N, WINDOW, N_HEADS_A, HEAD_DIM)
    kb = k.reshape(B, N, WINDOW, N_KV_A, HEAD_DIM)
    vb = v.reshape(B, N, WINDOW, N_KV_A, HEAD_DIM)
    kband = jnp.concatenate([jnp.concatenate([jnp.zeros_like(kb[:, :1]), kb[:, :-1]], 1), kb], 2)
    vband = jnp.concatenate([jnp.concatenate([jnp.zeros_like(vb[:, :1]), vb[:, :-1]], 1), vb], 2)
    i = jnp.arange(WINDOW)[:, None]
    j = jnp.arange(2 * WINDOW)[None, :]
    bias, m = window_bias(rel_bias, i + WINDOW - j)
    first = (jnp.arange(N) == 0)[:, None, None] & (j < WINDOW)[None]
    mask = m[None] & ~first
    o = swa_core(qb, kband, vband, bias, mask, sinks)
    return o.reshape(B, L, WIDTH_A)


def swa_sample(q, k, v, win_k, win_v, rel_bias, sinks):
    B, T = q.shape[:2]
    R = win_k.shape[1]
    k_all = jnp.concatenate([win_k.astype(k.dtype), k], 1)
    v_all = jnp.concatenate([win_v.astype(v.dtype), v], 1)
    d = jnp.arange(T)[:, None] + R - jnp.arange(R + T)[None, :]
    bias, m = window_bias(rel_bias, d)
    o = swa_core(q[:, None], k_all[:, None], v_all[:, None], bias, m[None], sinks)[:, 0]
    return o, k_all[:, -R:], v_all[:, -R:]


def gla_chunked(q, k, v, g, s0):
    B, L, H = q.shape[:3]
    C = GLA_CHUNK if L % GLA_CHUNK == 0 else L
    N = L // C

    def to_chunks(t):
        return t.reshape(B, N, C, H, t.shape[-1]).transpose(1, 0, 3, 2, 4).astype(jnp.float32)

    qc, kc, vc, gc = to_chunks(q), to_chunks(k), to_chunks(v), to_chunks(g)
    b = jnp.cumsum(gc, axis=3)
    tri = jnp.arange(C)[:, None] >= jnp.arange(C)[None, :]
    decay = jnp.exp(jnp.where(tri[:, :, None], b[..., :, None, :] - b[..., None, :, :], -jnp.inf))
    a_intra = jnp.einsum('nbhid,nbhjd,nbhijd->nbhij', qc, kc, decay)
    o_intra = jnp.einsum('nbhij,nbhje->nbhie', a_intra, vc)
    b_last = b[..., -1:, :]
    q_inter = qc * jnp.exp(b)
    k_state = kc * jnp.exp(b_last - b)
    decay_last = jnp.exp(b_last[..., 0, :])

    def step(S, xs):
        qi, ks, vv, dl = xs
        o = jnp.einsum('bhid,bhde->bhie', qi, S)
        S = dl[..., None] * S + jnp.einsum('bhjd,bhje->bhde', ks, vv)
        return S, o

    s_fin, o_inter = lax.scan(step, s0.astype(jnp.float32), (q_inter, k_state, vc, decay_last))
    o = (o_intra + o_inter).transpose(1, 0, 3, 2, 4).reshape(B, L, H, v.shape[-1])
    return o.astype(v.dtype), s_fin


def block(x, win_k, win_v, s0, conv0, rel_bias,
          ln1, w_in, ln_q, ln_k, sinks, w_gk2, b_gk, ln_o, w_oa, w_ob, w_out,
          ln2, w_up, w_conv, b_conv, w_down):
    B, L, _ = x.shape
    h = rms_norm(x, ln1)
    proj = h @ w_in
    offs = np.cumsum(SPLITS)[:-1].tolist()
    qa, ka, va, qb, kb, vb, rb, gk_lr, ga, gb = jnp.split(proj, offs, axis=-1)
    qa = rms_norm(qa.reshape(B, L, N_HEADS_A, HEAD_DIM), ln_q)
    ka = rms_norm(ka.reshape(B, L, N_KV_A, HEAD_DIM), ln_k)
    va = va.reshape(B, L, N_KV_A, HEAD_DIM)
    if win_k is None:
        oa = swa_prompt(qa, ka, va, rel_bias, sinks)
        new_k, new_v = ka[:, -WINDOW:], va[:, -WINDOW:]
    else:
        oa, new_k, new_v = swa_sample(qa, ka, va, win_k, win_v, rel_bias, sinks)
    g = jax.nn.log_sigmoid((gk_lr @ w_gk2 + b_gk).astype(jnp.float32)) / GATE_TAU
    ob, s_new = gla_chunked(qb.reshape(B, L, N_HEADS_B, DK_B) * (DK_B ** -0.5),
                            kb.reshape(B, L, N_HEADS_B, DK_B),
                            vb.reshape(B, L, N_HEADS_B, DV_B),
                            g.reshape(B, L, N_HEADS_B, DK_B), s0)
    ob = rms_norm(ob, ln_o).reshape(B, L, WIDTH_BV) * jax.nn.silu(rb)
    mix = jax.nn.sigmoid(ga) * (oa @ w_oa) + jax.nn.sigmoid(gb) * (ob @ w_ob)
    x = x + mix @ w_out
    u = rms_norm(x, ln2) @ w_up
    up = jnp.concatenate([conv0.astype(u.dtype), u], 1)
    uc = sum(w_conv[i] * up[:, i:i + L] for i in range(CONV_W)) + b_conv
    val, gate = jnp.split(uc, 2, axis=-1)
    x = x + (jax.nn.gelu(gate) * val) @ w_down
    new_conv = up[:, -(CONV_W - 1):]
    return x, new_k, new_v, s_new.astype(x.dtype), new_conv


def setup_inputs(seed: int = 0) -> dict:
    key = jax.random.key(seed)
    ks = jax.random.split(key, 32)
    f = jnp.float32
    nrm = lambda k, shape, scale: jax.random.normal(k, shape, f) * scale
    gain = lambda k, shape: 1.0 + 0.01 * jax.random.normal(k, shape, f)
    return {
        "x_prompt": nrm(ks[0], (BATCH, SEQ, D_MODEL), 1.0),
        "x_sample": nrm(ks[1], (DEC_BATCH, DEC_SEQ, D_MODEL), 1.0),
        "cache_win_k": nrm(ks[2], (DEPTH, DEC_BATCH, min(WINDOW, PAST_LEN), N_KV_A, HEAD_DIM), 1.0),
        "cache_win_v": nrm(ks[3], (DEPTH, DEC_BATCH, min(WINDOW, PAST_LEN), N_KV_A, HEAD_DIM), 1.0),
        "state_gla": nrm(ks[4], (DEPTH, DEC_BATCH, N_HEADS_B, DK_B, DV_B), 2.0),
        "state_conv": nrm(ks[5], (DEPTH, DEC_BATCH, CONV_W - 1, 2 * D_FF), 1.0),
        "rel_bias": nrm(ks[6], (N_BUCKETS, N_HEADS_A), 0.5),
        "ln1": gain(ks[7], (DEPTH, D_MODEL)),
        "w_in": nrm(ks[8], (DEPTH, D_MODEL, IN_COLS), D_MODEL ** -0.5),
        "ln_q": gain(ks[9], (DEPTH, HEAD_DIM)),
        "ln_k": gain(ks[10], (DEPTH, HEAD_DIM)),
        "sinks": nrm(ks[11], (DEPTH, N_HEADS_A), 0.5),
        "w_gk2": nrm(ks[12], (DEPTH, GATE_RANK, WIDTH_BK), GATE_RANK ** -0.5),
        "b_gk": nrm(ks[13], (DEPTH, WIDTH_BK), 0.01),
        "ln_o": gain(ks[14], (DEPTH, DV_B)),
        "w_oa": nrm(ks[15], (DEPTH, WIDTH_A, D_MODEL), WIDTH_A ** -0.5),
        "w_ob": nrm(ks[16], (DEPTH, WIDTH_BV, D_MODEL), WIDTH_BV ** -0.5),
        "w_out": nrm(ks[17], (DEPTH, D_MODEL, D_MODEL), D_MODEL ** -0.5),
        "ln2": gain(ks[18], (DEPTH, D_MODEL)),
        "w_up": nrm(ks[19], (DEPTH, D_MODEL, 2 * D_FF), D_MODEL ** -0.5),
        "w_conv": nrm(ks[20], (DEPTH, CONV_W, 2 * D_FF), CONV_W ** -0.5),
        "b_conv": nrm(ks[21], (DEPTH, 2 * D_FF), 0.01),
        "w_down": nrm(ks[22], (DEPTH, D_FF, D_MODEL), D_FF ** -0.5),
    }


def reference(x_prompt, x_sample, cache_win_k, cache_win_v, state_gla, state_conv, rel_bias,
              ln1, w_in, ln_q, ln_k, sinks, w_gk2, b_gk, ln_o, w_oa, w_ob, w_out,
              ln2, w_up, w_conv, b_conv, w_down):
    xp, xs = x_prompt, x_sample
    Bp = x_prompt.shape[0]
    kp, vp, gp, cp = [], [], [], []
    ksm, vsm, gsm, csm = [], [], [], []
    for l in range(DEPTH):
        lw = (ln1[l], w_in[l], ln_q[l], ln_k[l], sinks[l], w_gk2[l], b_gk[l], ln_o[l],
              w_oa[l], w_ob[l], w_out[l], ln2[l], w_up[l], w_conv[l], b_conv[l], w_down[l])
        s0_p = jnp.zeros((Bp, N_HEADS_B, DK_B, DV_B), jnp.float32)
        c0_p = jnp.zeros((Bp, CONV_W - 1, 2 * D_FF), xp.dtype)
        xp, nk, nv, ng, nc = block(xp, None, None, s0_p, c0_p, rel_bias, *lw)
        kp.append(nk); vp.append(nv); gp.append(ng); cp.append(nc)
        xs, nk, nv, ng, nc = block(xs, cache_win_k[l], cache_win_v[l], state_gla[l], state_conv[l], rel_bias, *lw)
        ksm.append(nk); vsm.append(nv); gsm.append(ng); csm.append(nc)
    return (xp, xs,
            jnp.stack(kp), jnp.stack(vp), jnp.stack(gp), jnp.stack(cp),
            jnp.stack(ksm), jnp.stack(vsm), jnp.stack(gsm), jnp.stack(csm))
```

```python
import functools
import math

import numpy as np
import jax
import jax.numpy as jnp
from jax import lax
from jax.experimental import pallas as pl
from jax.experimental.pallas import tpu as pltpu

F32 = jnp.float32
BF16 = jnp.bfloat16

D_MODEL = 2048
DEPTH = 4
HEAD_DIM = 64
N_HEADS_A = 16
N_KV_A = 4
WIDTH_A = N_HEADS_A * HEAD_DIM
KV_WIDTH_A = N_KV_A * HEAD_DIM
WINDOW = 128
N_BUCKETS = 32
MAX_DISTANCE = 128
N_HEADS_B = 4
DK_B = 128
DV_B = 256
WIDTH_BK = N_HEADS_B * DK_B
WIDTH_BV = N_HEADS_B * DV_B
GATE_RANK = 16
GATE_TAU = 16.0
D_FF = 5632
CONV_W = 3
EPS = 1e-6

OFF_QA = 0
OFF_KA = OFF_QA + WIDTH_A
OFF_VA = OFF_KA + KV_WIDTH_A
OFF_QB = OFF_VA + KV_WIDTH_A
OFF_KB = OFF_QB + WIDTH_BK
OFF_VB = OFF_KB + WIDTH_BK
OFF_RB = OFF_VB + WIDTH_BV
OFF_GK = OFF_RB + WIDTH_BV
OFF_GA = OFF_GK + GATE_RANK
MAIN_COLS = OFF_GK

LANES = 128
NEG_BIG = -1e30
VMEM_LIMIT = 56 * 1024 * 1024

GLA_CHUNK_P = 64
GLA_TBLOCK = 512
SAMPLE_GROUP_A = 8
SAMPLE_GROUP_B = 4


def _cparams(sem):
    return pltpu.CompilerParams(dimension_semantics=sem, vmem_limit_bytes=VMEM_LIMIT)


def _dot(a, b):
    return jnp.dot(a, b, preferred_element_type=F32)


def _dot_nt(a, b):
    return lax.dot_general(a, b, (((1,), (1,)), ((), ())), preferred_element_type=F32)


def _dot_tn(a, b):
    return lax.dot_general(a, b, (((0,), (0,)), ((), ())), preferred_element_type=F32)


def _split3(x):
    hi = x.astype(BF16)
    r = x - hi.astype(F32)
    mid = r.astype(BF16)
    lo = (r - mid.astype(F32)).astype(BF16)
    return hi, mid, lo


def _dot_exact_lhs01(m01, x):
    hi, mid, lo = _split3(x)
    return _dot(m01, hi) + _dot(m01, mid) + _dot(m01, lo)


def _rms(x, g):
    ms = jnp.mean(x * x, axis=-1, keepdims=True)
    return x * lax.rsqrt(ms + EPS) * g


def _inproj_kernel(x_ref, ln_ref, w_ref, wgk_ref, o_ref, gk_ref, h_scr):
    @pl.when(pl.program_id(1) == 0)
    def _():
        h = _rms(x_ref[...], ln_ref[...]).astype(BF16)
        h_scr[...] = h
        gk_ref[...] = _dot(h, wgk_ref[...].astype(BF16))

    o_ref[...] = _dot(h_scr[...], w_ref[...].astype(BF16))


def _normproj_kernel(x_ref, ln_ref, w_ref, o_ref, h_scr):
    @pl.when(pl.program_id(1) == 0)
    def _():
        h_scr[...] = _rms(x_ref[...], ln_ref[...]).astype(BF16)

    o_ref[...] = _dot(h_scr[...], w_ref[...].astype(BF16))


def _inproj(x, ln, w_in, w_gk_pad, l, tm, tn=512):
    r = x.shape[0]
    grid = (r // tm, MAIN_COLS // tn)
    return pl.pallas_call(
        _inproj_kernel,
        grid=grid,
        in_specs=[
            pl.BlockSpec((tm, D_MODEL), lambda i, j: (i, 0)),
            pl.BlockSpec((None, 1, D_MODEL), lambda i, j: (l, 0, 0)),
            pl.BlockSpec((None, D_MODEL, tn), lambda i, j: (l, 0, j)),
            pl.BlockSpec((None, D_MODEL, LANES), lambda i, j: (l, 0, 0)),
        ],
        out_specs=[
            pl.BlockSpec((tm, tn), lambda i, j: (i, j)),
            pl.BlockSpec((tm, LANES), lambda i, j: (i, 0)),
        ],
        out_shape=[
            jax.ShapeDtypeStruct((r, MAIN_COLS), F32),
            jax.ShapeDtypeStruct((r, LANES), F32),
        ],
        scratch_shapes=[pltpu.VMEM((tm, D_MODEL), BF16)],
        compiler_params=_cparams(("parallel", "arbitrary")),
        name="inproj",
    )(x, ln, w_in, w_gk_pad)


def _normproj(x, ln, w, l, tm, tn, name):
    r = x.shape[0]
    n = w.shape[-1]
    return pl.pallas_call(
        _normproj_kernel,
        grid=(r // tm, n // tn),
        in_specs=[
            pl.BlockSpec((tm, D_MODEL), lambda i, j: (i, 0)),
            pl.BlockSpec((None, 1, D_MODEL), lambda i, j: (l, 0, 0)),
            pl.BlockSpec((None, D_MODEL, tn), lambda i, j: (l, 0, j)),
        ],
        out_specs=pl.BlockSpec((tm, tn), lambda i, j: (i, j)),
        out_shape=jax.ShapeDtypeStruct((r, n), F32),
        scratch_shapes=[pltpu.VMEM((tm, D_MODEL), BF16)],
        compiler_params=_cparams(("parallel", "arbitrary")),
        name=name,
    )(x, ln, w)


def _bucket_ranges():
    d = np.arange(WINDOW)
    max_exact = N_BUCKETS // 2
    df = np.maximum(d, 1).astype(np.float32)
    large = max_exact + (np.log(df / np.float32(max_exact)) / np.float32(math.log(MAX_DISTANCE / max_exact))
                         * np.float32(N_BUCKETS - max_exact)).astype(np.int32)
    large = np.minimum(large, N_BUCKETS - 1)
    bucket = np.where(d < max_exact, d, large)
    ranges = []
    for b in range(N_BUCKETS):
        idx = np.nonzero(bucket == b)[0]
        if idx.size:
            assert idx[-1] - idx[0] + 1 == idx.size
            ranges.append((b, int(idx[0]), int(idx[-1])))
    return ranges


def _bias_kernel(rel_ref, d_ref, o_ref):
    h = pl.program_id(0)
    d = d_ref[...]
    val = jnp.full(d.shape, NEG_BIG, F32)
    for b, lo, hi in _bucket_ranges():
        val = jnp.where((d >= lo) & (d <= hi), rel_ref[b, h], val)
    o_ref[...] = val


def _bias_table(rel_bias, dmap):
    mq, nk = dmap.shape
    return pl.pallas_call(
        _bias_kernel,
        grid=(N_HEADS_A,),
        in_specs=[
            pl.BlockSpec(memory_space=pltpu.SMEM),
            pl.BlockSpec((mq, nk), lambda h: (0, 0)),
        ],
        out_specs=pl.BlockSpec((None, mq, nk), lambda h: (h, 0, 0)),
        out_shape=jax.ShapeDtypeStruct((N_HEADS_A, mq, nk), F32),
        compiler_params=_cparams(("parallel",)),
        name="bias_table",
    )(rel_bias, dmap)


def _dmap_prompt():
    i = np.arange(WINDOW)[:, None]
    j = np.arange(2 * WINDOW)[None, :]
    d = i + WINDOW - j
    return np.where((d >= 0) & (d < WINDOW), d, -1).astype(np.int32)


def _dmap_sample(t_new, group):
    nk = group * WINDOW + LANES
    rows = np.arange(group * t_new)
    rb, rt = rows // t_new, rows % t_new
    d = np.full((group * t_new, nk), -1, np.int64)
    cols = np.arange(group * WINDOW)
    cb, cs = cols // WINDOW, cols % WINDOW
    dw = rt[:, None] + WINDOW - cs[None, :]
    ok = (rb[:, None] == cb[None, :]) & (dw >= 0) & (dw < WINDOW)
    d[:, :group * WINDOW] = np.where(ok, dw, -1)
    ncols = np.arange(group * t_new)
    nb, nu = ncols // t_new, ncols % t_new
    dn = rt[:, None] - nu[None, :]
    okn = (rb[:, None] == nb[None, :]) & (dn >= 0)
    d[:, group * WINDOW:group * WINDOW + group * t_new] = np.where(okn, dn, -1)
    return d.astype(np.int32)


def _group_ones():
    r = lax.broadcasted_iota(jnp.int32, (LANES, LANES), 0)
    c = lax.broadcasted_iota(jnp.int32, (LANES, LANES), 1)
    low_c = jnp.where(c < HEAD_DIM, 1.0, 0.0)
    return jnp.where(r < HEAD_DIM, low_c, 1.0 - low_c).astype(BF16)


def _head_norm(x, gmat, ln2):
    sq = x * x
    hi = sq.astype(BF16)
    lo = (sq - hi.astype(F32)).astype(BF16)
    ms = (_dot(hi, gmat) + _dot(lo, gmat)) * (1.0 / HEAD_DIM)
    return x * lax.rsqrt(ms + EPS) * ln2


def _split_heads(x):
    lane = lax.broadcasted_iota(jnp.int32, (x.shape[0], LANES), 1)
    low = lane < HEAD_DIM
    lo_parts, hi_parts = [], []
    for c in range(KV_WIDTH_A // LANES):
        xc = x[:, c * LANES:(c + 1) * LANES]
        xr = pltpu.roll(xc, HEAD_DIM, 1)
        lo_parts += [jnp.where(low, xc, 0.0).astype(BF16), jnp.where(low, xr, 0.0).astype(BF16)]
        hi_parts += [jnp.where(low, 0.0, xr).astype(BF16), jnp.where(low, 0.0, xc).astype(BF16)]
    return lo_parts, hi_parts


def _attend(q_cols, k, v, bias_ref, sink_ref, pen):
    k_lo, k_hi = _split_heads(k)
    v_lo, v_hi = _split_heads(v)
    group = N_HEADS_A // N_KV_A
    outs = []
    for pc in range(N_HEADS_A // 2):
        acc = None
        for half in range(2):
            h = 2 * pc + half
            kh = h // group
            kk = (k_lo, k_hi)[half][kh]
            vv = (v_lo, v_hi)[half][kh]
            s = _dot_nt(q_cols[pc], kk) + bias_ref[h]
            if pen is not None:
                s = s + pen
            sink = sink_ref[h]
            m = jnp.maximum(jnp.max(s, axis=-1, keepdims=True), sink)
            e = jnp.exp(s - m)
            den = jnp.sum(e, axis=-1, keepdims=True) + jnp.exp(sink - m)
            o = _dot(e.astype(BF16), vv) * (1.0 / den)
            acc = o if acc is None else acc + o
        outs.append(acc)
    return outs


def _attn_prompt_kernel(sink_ref, q_ref, kp_ref, ko_ref, vp_ref, vo_ref, lnq_ref, lnk_ref, bias_ref,
                        o_ref, kn_ref):
    n = pl.program_id(1)
    gmat = _group_ones()
    lnq = lnq_ref[...]
    lnk = lnk_ref[...]
    scale = HEAD_DIM ** -0.5

    def norm_k(ref):
        return jnp.concatenate(
            [_head_norm(ref[:, c * LANES:(c + 1) * LANES], gmat, lnk) for c in range(KV_WIDTH_A // LANES)], axis=1)

    kn_own = norm_k(ko_ref)
    kn_ref[...] = kn_own
    k = jnp.concatenate([norm_k(kp_ref), kn_own], axis=0)
    v = jnp.concatenate([vp_ref[...], vo_ref[...]], axis=0)
    q_cols = [(_head_norm(q_ref[:, c * LANES:(c + 1) * LANES], gmat, lnq) * scale).astype(BF16)
              for c in range(WIDTH_A // LANES)]
    col = lax.broadcasted_iota(jnp.int32, (1, 2 * WINDOW), 1)
    pen = jnp.where(col < WINDOW, jnp.where(n == 0, NEG_BIG, 0.0), 0.0)
    outs = _attend(q_cols, k, v, bias_ref, sink_ref, pen)
    for c, o in enumerate(outs):
        o_ref[:, c * LANES:(c + 1) * LANES] = o.astype(o_ref.dtype)


def _attn_prompt(proj, sinks_l, lnq2, lnk2, bias_p, batch, seq):
    nb = seq // WINDOW
    r = batch * seq
    kcol = OFF_KA // KV_WIDTH_A
    vcol = OFF_VA // KV_WIDTH_A
    own = lambda b, n: (b * nb + n, 0)
    return pl.pallas_call(
        _attn_prompt_kernel,
        grid=(batch, nb),
        in_specs=[
            pl.BlockSpec(memory_space=pltpu.SMEM),
            pl.BlockSpec((WINDOW, WIDTH_A), own),
            pl.BlockSpec((WINDOW, KV_WIDTH_A), lambda b, n: (b * nb + jnp.maximum(n - 1, 0), kcol)),
            pl.BlockSpec((WINDOW, KV_WIDTH_A), lambda b, n: (b * nb + n, kcol)),
            pl.BlockSpec((WINDOW, KV_WIDTH_A), lambda b, n: (b * nb + jnp.maximum(n - 1, 0), vcol)),
            pl.BlockSpec((WINDOW, KV_WIDTH_A), lambda b, n: (b * nb + n, vcol)),
            pl.BlockSpec((1, LANES), lambda b, n: (0, 0)),
            pl.BlockSpec((1, LANES), lambda b, n: (0, 0)),
            pl.BlockSpec((N_HEADS_A, WINDOW, 2 * WINDOW), lambda b, n: (0, 0, 0)),
        ],
        out_specs=[
            pl.BlockSpec((WINDOW, WIDTH_A), own),
            pl.BlockSpec((WINDOW, KV_WIDTH_A), own),
        ],
        out_shape=[
            jax.ShapeDtypeStruct((r, WIDTH_A), BF16),
            jax.ShapeDtypeStruct((r, KV_WIDTH_A), F32),
        ],
        compiler_params=_cparams(("parallel", "arbitrary")),
        name="attn_prompt",
    )(sinks_l, proj, proj, proj, proj, proj, lnq2, lnk2, bias_p)


def _attn_sample_kernel(sink_ref, q_ref, kn_in_ref, vn_ref, wk_ref, wv_ref, lnq_ref, lnk_ref, bias_ref,
                        o_ref, kn_ref):
    gmat = _group_ones()
    lnq = lnq_ref[...]
    lnk = lnk_ref[...]
    scale = HEAD_DIM ** -0.5
    rows = q_ref.shape[0]
    g = wk_ref.shape[0]
    kn_new = jnp.concatenate(
        [_head_norm(kn_in_ref[:, c * LANES:(c + 1) * LANES], gmat, lnk) for c in range(KV_WIDTH_A // LANES)], axis=1)
    kn_ref[...] = kn_new
    pad = jnp.zeros((LANES - rows, KV_WIDTH_A), F32)
    k = jnp.concatenate([wk_ref[...].reshape(g * WINDOW, KV_WIDTH_A), kn_new, pad], axis=0)
    v = jnp.concatenate([wv_ref[...].reshape(g * WINDOW, KV_WIDTH_A), vn_ref[...], pad], axis=0)
    q_cols = [(_head_norm(q_ref[:, c * LANES:(c + 1) * LANES], gmat, lnq) * scale).astype(BF16)
              for c in range(WIDTH_A // LANES)]
    outs = _attend(q_cols, k, v, bias_ref, sink_ref, None)
    for c, o in enumerate(outs):
        o_ref[:, c * LANES:(c + 1) * LANES] = o.astype(o_ref.dtype)


def _attn_sample(proj, win_k, win_v, sinks_l, lnq2, lnk2, bias_s, batch, t_new):
    g = SAMPLE_GROUP_A
    rows = g * t_new
    nk = g * WINDOW + LANES
    kcol = OFF_KA // KV_WIDTH_A
    vcol = OFF_VA // KV_WIDTH_A
    return pl.pallas_call(
        _attn_sample_kernel,
        grid=(batch // g,),
        in_specs=[
            pl.BlockSpec(memory_space=pltpu.SMEM),
            pl.BlockSpec((rows, WIDTH_A), lambda i: (i, 0)),
            pl.BlockSpec((rows, KV_WIDTH_A), lambda i: (i, kcol)),
            pl.BlockSpec((rows, KV_WIDTH_A), lambda i: (i, vcol)),
            pl.BlockSpec((g, WINDOW, KV_WIDTH_A), lambda i: (i, 0, 0)),
            pl.BlockSpec((g, WINDOW, KV_WIDTH_A), lambda i: (i, 0, 0)),
            pl.BlockSpec((1, LANES), lambda i: (0, 0)),
            pl.BlockSpec((1, LANES), lambda i: (0, 0)),
            pl.BlockSpec((N_HEADS_A, rows, nk), lambda i: (0, 0, 0)),
        ],
        out_specs=[
            pl.BlockSpec((rows, WIDTH_A), lambda i: (i, 0)),
            pl.BlockSpec((rows, KV_WIDTH_A), lambda i: (i, 0)),
        ],
        out_shape=[
            jax.ShapeDtypeStruct((batch * t_new, WIDTH_A), BF16),
            jax.ShapeDtypeStruct((batch * t_new, KV_WIDTH_A), F32),
        ],
        compiler_params=_cparams(("parallel",)),
        name="attn_sample",
    )(sinks_l, proj, proj, proj, win_k, win_v, lnq2, lnk2, bias_s)


def _log_decay(gk, w2_ref, bgk_ref):
    x = _dot(gk.astype(BF16), w2_ref[...]) + bgk_ref[...]
    log_sig = jnp.minimum(x, 0.0) - jnp.log(1.0 + jnp.exp(-jnp.abs(x)))
    return log_sig * (1.0 / GATE_TAU)


def _column(row):
    r = lax.broadcasted_iota(jnp.int32, (LANES, LANES), 0)
    c = lax.broadcasted_iota(jnp.int32, (LANES, LANES), 1)
    return jnp.sum(jnp.where(r == c, row, 0.0), axis=1, keepdims=True)


def _gla_out(o, ln_o, rgate):
    return (_rms(o, ln_o) * (rgate * jax.nn.sigmoid(rgate))).astype(BF16)


def _gla_prompt_kernel(q_ref, k_ref, va_ref, vb_ref, ra_ref, rb_ref, gk_ref, w2_ref, bgk_ref, lno_ref,
                       o_ref, s_out_ref, s_scr):
    t = pl.program_id(1)
    c_len = GLA_CHUNK_P
    n_chunks = q_ref.shape[0] // c_len
    half = c_len // 2
    scale = DK_B ** -0.5

    @pl.when(t == 0)
    def _():
        s_scr[...] = jnp.zeros_like(s_scr)

    ri = lax.broadcasted_iota(jnp.int32, (c_len, c_len), 0)
    ci = lax.broadcasted_iota(jnp.int32, (c_len, c_len), 1)
    causal = ri >= ci
    tri = jnp.where(causal, 1.0, 0.0).astype(BF16)
    ln_o = lno_ref[...]
    v_refs = (va_ref, vb_ref)
    r_refs = (ra_ref, rb_ref)

    def chunk(c, carry):
        rows = pl.ds(pl.multiple_of(c * c_len, c_len), c_len)
        g = _log_decay(gk_ref[rows, :], w2_ref, bgk_ref)
        b = _dot_exact_lhs01(tri, g)
        for h in range(N_HEADS_B):
            ks = slice(h * DK_B, (h + 1) * DK_B)
            vs = slice((h % 2) * DV_B, (h % 2 + 1) * DV_B)
            bh = b[:, ks]
            b_last = bh[c_len - 1:c_len, :]
            b_mid = bh[half - 1:half, :]
            q = q_ref[rows, ks] * scale
            k = k_ref[rows, ks]
            v = v_refs[h // 2][rows, vs].astype(BF16)
            q_inter = (q * jnp.exp(bh)).astype(BF16)
            q_t = (q * jnp.exp(bh - b_mid)).astype(BF16)
            k_t = (k * jnp.exp(b_mid - bh)).astype(BF16)
            a = jnp.where(causal, _dot_nt(q_t, k_t), 0.0).astype(BF16)
            k_state = (k * jnp.exp(b_last - bh)).astype(BF16)
            s_old = s_scr[h]
            o = _dot(a, v) + _dot(q_inter, s_old.astype(BF16))
            s_scr[h] = _column(jnp.exp(b_last)) * s_old + _dot_tn(k_state, v)
            o_ref[rows, h * DV_B:(h + 1) * DV_B] = _gla_out(o, ln_o, r_refs[h // 2][rows, vs])
        return carry

    lax.fori_loop(0, n_chunks, chunk, 0)

    @pl.when(t == pl.num_programs(1) - 1)
    def _():
        s_out_ref[...] = s_scr[...]


def _gla_prompt(proj, gk, w2_l, bgk_l, lno_l, batch, seq):
    tb = GLA_TBLOCK
    nt = seq // tb
    w = 2 * DV_B
    row = lambda b, t: b * nt + t
    spec = lambda col: pl.BlockSpec((tb, w), lambda b, t: (row(b, t), col))
    return pl.pallas_call(
        _gla_prompt_kernel,
        grid=(batch, nt),
        in_specs=[
            spec(OFF_QB // w), spec(OFF_KB // w),
            spec(OFF_VB // w), spec(OFF_VB // w + 1),
            spec(OFF_RB // w), spec(OFF_RB // w + 1),
            pl.BlockSpec((tb, LANES), lambda b, t: (row(b, t), 0)),
            pl.BlockSpec((LANES, WIDTH_BK), lambda b, t: (0, 0)),
            pl.BlockSpec((1, WIDTH_BK), lambda b, t: (0, 0)),
            pl.BlockSpec((1, DV_B), lambda b, t: (0, 0)),
        ],
        out_specs=[
            pl.BlockSpec((tb, WIDTH_BV), lambda b, t: (row(b, t), 0)),
            pl.BlockSpec((None, N_HEADS_B, DK_B, DV_B), lambda b, t: (b, 0, 0, 0)),
        ],
        out_shape=[
            jax.ShapeDtypeStruct((batch * seq, WIDTH_BV), BF16),
            jax.ShapeDtypeStruct((batch, N_HEADS_B, DK_B, DV_B), F32),
        ],
        scratch_shapes=[pltpu.VMEM((N_HEADS_B, DK_B, DV_B), F32)],
        compiler_params=_cparams(("parallel", "arbitrary")),
        name="gla_prompt",
    )(proj, proj, proj, proj, proj, proj, gk, w2_l, bgk_l, lno_l)


def _gla_sample_kernel(t_new, q_ref, k_ref, va_ref, vb_ref, ra_ref, rb_ref, gk_ref, w2_ref, bgk_ref, lno_ref,
                       s_ref, o_ref, s_out_ref):
    rows = q_ref.shape[0]
    n_seq = rows // t_new
    scale = DK_B ** -0.5
    ri = lax.broadcasted_iota(jnp.int32, (rows, rows), 0)
    ci = lax.broadcasted_iota(jnp.int32, (rows, rows), 1)
    same = None
    for sq in range(n_seq):
        lo, hi = sq * t_new, (sq + 1) * t_new
        blk = (ri >= lo) & (ri < hi) & (ci >= lo) & (ci < hi)
        same = blk if same is None else (same | blk)
    causal = same & (ri >= ci)
    tri = jnp.where(causal, 1.0, 0.0).astype(BF16)
    rcol = lax.broadcasted_iota(jnp.int32, (rows, 1), 0)
    ln_o = lno_ref[...]
    v_refs = (va_ref, vb_ref)
    r_refs = (ra_ref, rb_ref)

    g = _log_decay(gk_ref[...], w2_ref, bgk_ref)
    b = _dot_exact_lhs01(tri, g)
    for h in range(N_HEADS_B):
        ks = slice(h * DK_B, (h + 1) * DK_B)
        vs = slice((h % 2) * DV_B, (h % 2 + 1) * DV_B)
        bh = b[:, ks]
        gh = g[:, ks]
        q = q_ref[:, ks] * scale
        k = k_ref[:, ks]
        v = v_refs[h // 2][:, vs].astype(BF16)
        q_inter = q * jnp.exp(bh)
        k_t = (k * jnp.exp(-bh)).astype(BF16)
        a = jnp.where(causal, _dot_nt(q_inter.astype(BF16), k_t), 0.0).astype(BF16)
        o = _dot(a, v)
        for sq in range(n_seq):
            mine = (rcol >= sq * t_new) & (rcol < (sq + 1) * t_new)
            b_last = jnp.sum(jnp.where(mine, gh, 0.0), axis=0, keepdims=True)
            k_state = jnp.where(mine, k * jnp.exp(b_last - bh), 0.0).astype(BF16)
            q_mine = jnp.where(mine, q_inter, 0.0).astype(BF16)
            s_old = s_ref[sq, h]
            o = o + _dot(q_mine, s_old.astype(BF16))
            s_out_ref[sq, h] = _column(jnp.exp(b_last)) * s_old + _dot_tn(k_state, v)
        o_ref[:, h * DV_B:(h + 1) * DV_B] = _gla_out(o, ln_o, r_refs[h // 2][:, vs])


def _gla_sample(proj, gk, w2_l, bgk_l, lno_l, state, batch, t_new):
    g = SAMPLE_GROUP_B
    rows = g * t_new
    w = 2 * DV_B
    spec = lambda col: pl.BlockSpec((rows, w), lambda i: (i, col))
    st = pl.BlockSpec((g, N_HEADS_B, DK_B, DV_B), lambda i: (i, 0, 0, 0))
    return pl.pallas_call(
        functools.partial(_gla_sample_kernel, t_new),
        grid=(batch // g,),
        in_specs=[
            spec(OFF_QB // w), spec(OFF_KB // w),
            spec(OFF_VB // w), spec(OFF_VB // w + 1),
            spec(OFF_RB // w), spec(OFF_RB // w + 1),
            pl.BlockSpec((rows, LANES), lambda i: (i, 0)),
            pl.BlockSpec((LANES, WIDTH_BK), lambda i: (0, 0)),
            pl.BlockSpec((1, WIDTH_BK), lambda i: (0, 0)),
            pl.BlockSpec((1, DV_B), lambda i: (0, 0)),
            st,
        ],
        out_specs=[pl.BlockSpec((rows, WIDTH_BV), lambda i: (i, 0)), st],
        out_shape=[
            jax.ShapeDtypeStruct((batch * t_new, WIDTH_BV), BF16),
            jax.ShapeDtypeStruct((batch, N_HEADS_B, DK_B, DV_B), F32),
        ],
        compiler_params=_cparams(("parallel",)),
        name="gla_sample",
    )(proj, proj, proj, proj, proj, proj, gk, w2_l, bgk_l, lno_l, state)


def _mix_kernel(oa_ref, ob_ref, ga_ref, gb_ref, woa_ref, wob_ref, o_ref):
    a = _dot(oa_ref[...], woa_ref[...].astype(BF16))
    b = _dot(ob_ref[...], wob_ref[...].astype(BF16))
    o_ref[...] = (jax.nn.sigmoid(ga_ref[...]) * a + jax.nn.sigmoid(gb_ref[...]) * b).astype(o_ref.dtype)


def _mix(oa, ob, gates, w_oa, w_ob, l, tm, tn=512):
    r = oa.shape[0]
    nj = D_MODEL // tn
    return pl.pallas_call(
        _mix_kernel,
        grid=(r // tm, nj),
        in_specs=[
            pl.BlockSpec((tm, WIDTH_A), lambda i, j: (i, 0)),
            pl.BlockSpec((tm, WIDTH_BV), lambda i, j: (i, 0)),
            pl.BlockSpec((tm, tn), lambda i, j: (i, j)),
            pl.BlockSpec((tm, tn), lambda i, j: (i, j + nj)),
            pl.BlockSpec((None, WIDTH_A, tn), lambda i, j: (l, 0, j)),
            pl.BlockSpec((None, WIDTH_BV, tn), lambda i, j: (l, 0, j)),
        ],
        out_specs=pl.BlockSpec((tm, tn), lambda i, j: (i, j)),
        out_shape=jax.ShapeDtypeStruct((r, D_MODEL), BF16),
        compiler_params=_cparams(("parallel", "arbitrary")),
        name="mix",
    )(oa, ob, gates, gates, w_oa, w_ob)


def _resid_proj_kernel(a_ref, w_ref, x_ref, o_ref):
    o_ref[...] = x_ref[...] + _dot(a_ref[...], w_ref[...].astype(BF16))


def _resid_proj(a, w, x, l, tm, tn, name):
    r, kdim = a.shape
    return pl.pallas_call(
        _resid_proj_kernel,
        grid=(r // tm, D_MODEL // tn),
        in_specs=[
            pl.BlockSpec((tm, kdim), lambda i, j: (i, 0)),
            pl.BlockSpec((None, kdim, tn), lambda i, j: (l, 0, j)),
            pl.BlockSpec((tm, tn), lambda i, j: (i, j)),
        ],
        out_specs=pl.BlockSpec((tm, tn), lambda i, j: (i, j)),
        out_shape=jax.ShapeDtypeStruct((r, D_MODEL), F32),
        compiler_params=_cparams(("parallel", "arbitrary")),
        name=name,
    )(a, w, x)


def _gelu_tanh(x):
    return x * (0.5 * (1.0 + jnp.tanh(math.sqrt(2.0 / math.pi) * (x + 0.044715 * (x * x * x)))))


def _conv3(u, p1, p2, w_ref, b_ref):
    return w_ref[0:1, :] * p2 + w_ref[1:2, :] * p1 + w_ref[2:3, :] * u + b_ref[...]


def _ffn_up_prompt_kernel(tiles_per_seq, x_ref, ln_ref, wv_ref, wg_ref, cv_ref, cg_ref, bv_ref, bg_ref,
                          act_ref, tv_ref, tg_ref, h_scr, carry_v, carry_g):
    i = pl.program_id(0)
    j = pl.program_id(1)

    @pl.when(j == 0)
    def _():
        h_scr[...] = _rms(x_ref[...], ln_ref[...]).astype(BF16)

    h = h_scr[...]
    tm = h.shape[0]
    first = (i % tiles_per_seq) == 0
    row8 = lax.broadcasted_iota(jnp.int32, (8, 1), 0)

    def branch(w_ref, carry, c_ref, b_ref, tail_ref):
        u = _dot(h, w_ref[...].astype(BF16))
        prev = jnp.where(first, 0.0, carry[j])
        tail = u[tm - 8:tm, :]
        carry[j] = tail
        tail_ref[...] = tail
        p1 = pltpu.roll(u, 1, 0)
        p2 = pltpu.roll(u, 2, 0)
        body = _conv3(u, p1, p2, c_ref, b_ref)
        p1_top = jnp.where(row8 < 1, pltpu.roll(prev, 1, 0), p1[0:8, :])
        p2_top = jnp.where(row8 < 2, pltpu.roll(prev, 2, 0), p2[0:8, :])
        top = _conv3(u[0:8, :], p1_top, p2_top, c_ref, b_ref)
        return jnp.concatenate([top, body[8:, :]], axis=0)

    val = branch(wv_ref, carry_v, cv_ref, bv_ref, tv_ref)
    gate = branch(wg_ref, carry_g, cg_ref, bg_ref, tg_ref)
    act_ref[...] = (_gelu_tanh(gate) * val).astype(act_ref.dtype)


def _ffn_up_prompt(x, ln2, w_up, w_conv, b_conv, l, tm, tiles_per_seq, tf=512):
    r = x.shape[0]
    nt = r // tm
    nj = D_FF // tf
    w_spec = lambda off: pl.BlockSpec((None, D_MODEL, tf), lambda i, j: (l, 0, j + off))
    c_spec = lambda off: pl.BlockSpec((None, CONV_W, tf), lambda i, j: (l, 0, j + off))
    b_spec = lambda off: pl.BlockSpec((None, 1, tf), lambda i, j: (l, 0, j + off))
    tail_spec = pl.BlockSpec((None, 8, tf), lambda i, j: (i, 0, j))
    return pl.pallas_call(
        functools.partial(_ffn_up_prompt_kernel, tiles_per_seq),
        grid=(nt, nj),
        in_specs=[
            pl.BlockSpec((tm, D_MODEL), lambda i, j: (i, 0)),
            pl.BlockSpec((None, 1, D_MODEL), lambda i, j: (l, 0, 0)),
            w_spec(0), w_spec(nj), c_spec(0), c_spec(nj), b_spec(0), b_spec(nj),
        ],
        out_specs=[pl.BlockSpec((tm, tf), lambda i, j: (i, j)), tail_spec, tail_spec],
        out_shape=[
            jax.ShapeDtypeStruct((r, D_FF), BF16),
            jax.ShapeDtypeStruct((nt, 8, D_FF), F32),
            jax.ShapeDtypeStruct((nt, 8, D_FF), F32),
        ],
        scratch_shapes=[
            pltpu.VMEM((tm, D_MODEL), BF16),
            pltpu.VMEM((nj, 8, tf), F32),
            pltpu.VMEM((nj, 8, tf), F32),
        ],
        compiler_params=_cparams(("arbitrary", "arbitrary")),
        name="ffn_up_prompt",
    )(x, ln2, w_up, w_up, w_conv, w_conv, b_conv, b_conv)


def _ffn_up_sample_kernel(pos_ref, x_ref, ln_ref, wv_ref, wg_ref, cv_ref, cg_ref, bv_ref, bg_ref,
                          h1v_ref, h2v_ref, h1g_ref, h2g_ref, act_ref, uv_ref, ug_ref, h_scr):
    @pl.when(pl.program_id(0) == 0)
    def _():
        h_scr[...] = _rms(x_ref[...], ln_ref[...]).astype(BF16)

    h = h_scr[...]
    pos = pos_ref[...]

    def branch(w_ref, c_ref, b_ref, h1_ref, h2_ref, u_ref):
        u = _dot(h, w_ref[...].astype(BF16))
        u_ref[...] = u
        p1 = jnp.where(pos >= 1, pltpu.roll(u, 1, 0), h1_ref[...])
        p2 = jnp.where(pos >= 2, pltpu.roll(u, 2, 0), h2_ref[...])
        return _conv3(u, p1, p2, c_ref, b_ref)

    val = branch(wv_ref, cv_ref, bv_ref, h1v_ref, h2v_ref, uv_ref)
    gate = branch(wg_ref, cg_ref, bg_ref, h1g_ref, h2g_ref, ug_ref)
    act_ref[...] = (_gelu_tanh(gate) * val).astype(act_ref.dtype)


def _ffn_up_sample(x, ln2, w_up, w_conv, b_conv, hist1, hist2, l, t_new, tf=512):
    r = x.shape[0]
    nj = D_FF // tf
    w_spec = lambda off: pl.BlockSpec((None, D_MODEL, tf), lambda j: (l, 0, j + off))
    c_spec = lambda off: pl.BlockSpec((None, CONV_W, tf), lambda j: (l, 0, j + off))
    b_spec = lambda off: pl.BlockSpec((None, 1, tf), lambda j: (l, 0, j + off))
    h_spec = lambda off: pl.BlockSpec((r, tf), lambda j: (0, j + off))
    u_spec = pl.BlockSpec((r, tf), lambda j: (0, j))
    pos = jnp.asarray((np.arange(r) % t_new).astype(np.int32).reshape(r, 1))
    return pl.pallas_call(
        _ffn_up_sample_kernel,
        grid=(nj,),
        in_specs=[
            pl.BlockSpec((r, 1), lambda j: (0, 0)),
            pl.BlockSpec((r, D_MODEL), lambda j: (0, 0)),
            pl.BlockSpec((None, 1, D_MODEL), lambda j: (l, 0, 0)),
            w_spec(0), w_spec(nj), c_spec(0), c_spec(nj), b_spec(0), b_spec(nj),
            h_spec(0), h_spec(0), h_spec(nj), h_spec(nj),
        ],
        out_specs=[u_spec, u_spec, u_spec],
        out_shape=[
            jax.ShapeDtypeStruct((r, D_FF), BF16),
            jax.ShapeDtypeStruct((r, D_FF), F32),
            jax.ShapeDtypeStruct((r, D_FF), F32),
        ],
        scratch_shapes=[pltpu.VMEM((r, D_MODEL), BF16)],
        compiler_params=_cparams(("arbitrary",)),
        name="ffn_up_sample",
    )(pos, x, ln2, w_up, w_up, w_conv, w_conv, b_conv, b_conv, hist1, hist2, hist1, hist2)


def _layer_dense_in(x, l, p, tm):
    proj, gk = _inproj(x, p["ln1"], p["w_in"], p["w_gk_pad"], l, tm)
    gates = _normproj(x, p["ln1"], p["w_gates"], l, tm, 512, "gates")
    return proj, gk, gates


def _layer_dense_out(x, oa, ob, gates, l, p, tm):
    mix = _mix(oa, ob, gates, p["w_oa"], p["w_ob"], l, tm)
    return _resid_proj(mix, p["w_out"], x, l, tm, 512, "out_proj")


def kernel(x_prompt, x_sample, cache_win_k, cache_win_v, state_gla, state_conv, rel_bias,
           ln1, w_in, ln_q, ln_k, sinks, w_gk2, b_gk, ln_o, w_oa, w_ob, w_out,
           ln2, w_up, w_conv, b_conv, w_down):
    bp, seq, _ = x_prompt.shape
    bs, t_new, _ = x_sample.shape
    tm_p = 1024
    tm_s = bs * t_new
    tiles_per_seq = seq // tm_p

    p = {
        "ln1": ln1.reshape(DEPTH, 1, D_MODEL),
        "ln2": ln2.reshape(DEPTH, 1, D_MODEL),
        "w_in": w_in,
        "w_gk_pad": jnp.pad(w_in[:, :, OFF_GK:OFF_GA], ((0, 0), (0, 0), (0, LANES - GATE_RANK))),
        "w_gates": w_in[:, :, OFF_GA:].astype(BF16),
        "w_oa": w_oa, "w_ob": w_ob, "w_out": w_out,
    }
    w2_pad = jnp.pad(w_gk2, ((0, 0), (0, LANES - GATE_RANK), (0, 0))).astype(BF16)
    b_conv3 = b_conv.reshape(DEPTH, 1, 2 * D_FF)
    lnq2 = jnp.tile(ln_q, (1, 2)).reshape(DEPTH, 1, LANES)
    lnk2 = jnp.tile(ln_k, (1, 2)).reshape(DEPTH, 1, LANES)

    bias_p = _bias_table(rel_bias, jnp.asarray(_dmap_prompt()))
    bias_s = _bias_table(rel_bias, jnp.asarray(_dmap_sample(t_new, SAMPLE_GROUP_A)))

    xp = x_prompt.reshape(bp * seq, D_MODEL)
    xs = x_sample.reshape(bs * t_new, D_MODEL)
    win_k = cache_win_k.reshape(DEPTH, bs, WINDOW, KV_WIDTH_A)
    win_v = cache_win_v.reshape(DEPTH, bs, WINDOW, KV_WIDTH_A)
    zeros = jnp.zeros((DEPTH, bs, 1, 2 * D_FF), F32)
    hist1 = jnp.concatenate([state_conv[:, :, 1:2], zeros, zeros, zeros], axis=2).reshape(DEPTH, tm_s, 2 * D_FF)
    hist2 = jnp.concatenate([state_conv[:, :, 0:1], state_conv[:, :, 1:2], zeros, zeros], axis=2)
    hist2 = hist2.reshape(DEPTH, tm_s, 2 * D_FF)

    kp, vp, gp, cp = [], [], [], []
    ksm, vsm, gsm, csm = [], [], [], []
    for l in range(DEPTH):
        lq, lk, sk = lnq2[l], lnk2[l], sinks[l]
        w2_l, bgk_l, lno_l = w2_pad[l], b_gk[l].reshape(1, WIDTH_BK), ln_o[l].reshape(1, DV_B)

        proj, gk, gates = _layer_dense_in(xp, l, p, tm_p)
        oa, kn = _attn_prompt(proj, sk, lq, lk, bias_p, bp, seq)
        ob, s_new = _gla_prompt(proj, gk, w2_l, bgk_l, lno_l, bp, seq)
        xp = _layer_dense_out(xp, oa, ob, gates, l, p, tm_p)
        act, tail_v, tail_g = _ffn_up_prompt(xp, p["ln2"], w_up, w_conv, b_conv3, l, tm_p, tiles_per_seq)
        xp = _resid_proj(act, w_down, xp, l, tm_p, 256, "down_proj")
        kp.append(kn.reshape(bp, seq, N_KV_A, HEAD_DIM)[:, -WINDOW:])
        vp.append(proj[:, OFF_VA:OFF_VA + KV_WIDTH_A].reshape(bp, seq, N_KV_A, HEAD_DIM)[:, -WINDOW:])
        gp.append(s_new)
        tails = jnp.concatenate([tail_v, tail_g], axis=-1).reshape(bp, tiles_per_seq, 8, 2 * D_FF)
        cp.append(tails[:, -1, 8 - (CONV_W - 1):])

        proj, gk, gates = _layer_dense_in(xs, l, p, tm_s)
        oa, kn = _attn_sample(proj, win_k[l], win_v[l], sk, lq, lk, bias_s, bs, t_new)
        ob, s_new = _gla_sample(proj, gk, w2_l, bgk_l, lno_l, state_gla[l], bs, t_new)
        xs = _layer_dense_out(xs, oa, ob, gates, l, p, tm_s)
        act, u_v, u_g = _ffn_up_sample(xs, p["ln2"], w_up, w_conv, b_conv3, hist1[l], hist2[l], l, t_new)
        xs = _resid_proj(act, w_down, xs, l, tm_s, 256, "down_proj_s")
        k_new = kn.reshape(bs, t_new, N_KV_A, HEAD_DIM)
        v_new = proj[:, OFF_VA:OFF_VA + KV_WIDTH_A].reshape(bs, t_new, N_KV_A, HEAD_DIM)
        ksm.append(jnp.concatenate([cache_win_k[l][:, t_new:], k_new], axis=1))
        vsm.append(jnp.concatenate([cache_win_v[l][:, t_new:], v_new], axis=1))
        gsm.append(s_new)
        u = jnp.concatenate([u_v, u_g], axis=-1).reshape(bs, t_new, 2 * D_FF)
        csm.append(u[:, t_new - (CONV_W - 1):])

    return (xp.reshape(bp, seq, D_MODEL), xs.reshape(bs, t_new, D_MODEL),
            jnp.stack(kp), jnp.stack(vp), jnp.stack(gp), jnp.stack(cp),
            jnp.stack(ksm), jnp.stack(vsm), jnp.stack(gsm), jnp.stack(csm))
```

```python
import functools
import math

import numpy as np
import jax
import jax.numpy as jnp
from jax import lax
from jax.experimental import pallas as pl
from jax.experimental.pallas import tpu as pltpu

F32 = jnp.float32
BF16 = jnp.bfloat16

D_MODEL = 2048
DEPTH = 4
HEAD_DIM = 64
N_HEADS_A = 16
N_KV_A = 4
WIDTH_A = N_HEADS_A * HEAD_DIM
KV_WIDTH_A = N_KV_A * HEAD_DIM
WINDOW = 128
N_BUCKETS = 32
MAX_DISTANCE = 128
N_HEADS_B = 4
DK_B = 128
DV_B = 256
WIDTH_BK = N_HEADS_B * DK_B
WIDTH_BV = N_HEADS_B * DV_B
GATE_RANK = 16
GATE_TAU = 16.0
D_FF = 5632
CONV_W = 3
EPS = 1e-6

OFF_QA = 0
OFF_KA = OFF_QA + WIDTH_A
OFF_VA = OFF_KA + KV_WIDTH_A
OFF_QB = OFF_VA + KV_WIDTH_A
OFF_KB = OFF_QB + WIDTH_BK
OFF_VB = OFF_KB + WIDTH_BK
OFF_RB = OFF_VB + WIDTH_BV
OFF_GK = OFF_RB + WIDTH_BV
OFF_GA = OFF_GK + GATE_RANK
MAIN_COLS = OFF_GK
GATE_COLS = 2 * D_MODEL

LANES = 128
NEG_BIG = -1e30
VMEM_LIMIT = 56 * 1024 * 1024

TM_PROMPT = 1024
TM_MERGE = 256
TN_MAIN = 1536
TN_GATES = 1024
TF_UP = 512
GLA_CHUNK_P = 64
GLA_TBLOCK = 512
SAMPLE_GROUP_A = 8
SAMPLE_GROUP_B = 4


def _cparams(sem):
    return pltpu.CompilerParams(dimension_semantics=sem, vmem_limit_bytes=VMEM_LIMIT)


def _dot(a, b):
    return jnp.dot(a, b, preferred_element_type=F32)


def _dot_nt(a, b):
    return lax.dot_general(a, b, (((1,), (1,)), ((), ())), preferred_element_type=F32)


def _dot_tn(a, b):
    return lax.dot_general(a, b, (((0,), (0,)), ((), ())), preferred_element_type=F32)


def _split3(x):
    hi = x.astype(BF16)
    r = x - hi.astype(F32)
    mid = r.astype(BF16)
    lo = (r - mid.astype(F32)).astype(BF16)
    return hi, mid, lo


def _dot_exact_lhs01(m01, x):
    hi, mid, lo = _split3(x)
    return _dot(m01, hi) + _dot(m01, mid) + _dot(m01, lo)


def _rms(x, g):
    ms = jnp.mean(x * x, axis=-1, keepdims=True)
    return x * lax.rsqrt(ms + EPS) * g


def _norm_kernel(x_ref, ln_ref, wgk_ref, h_ref, gk_ref):
    h = _rms(x_ref[...], ln_ref[...]).astype(BF16)
    h_ref[...] = h
    gk_ref[...] = _dot_nt(h, wgk_ref[...].astype(BF16))


def _ln_spec(l, nargs):
    if nargs == 1:
        return pl.BlockSpec((None, 1, D_MODEL), lambda i: (l, 0, 0))
    return pl.BlockSpec((None, 1, D_MODEL), lambda i, j: (l, 0, 0))


def _wgk_spec(l, nargs):
    blk = OFF_GK // GATE_RANK
    if nargs == 1:
        return pl.BlockSpec((None, GATE_RANK, D_MODEL), lambda i: (l, blk, 0))
    return pl.BlockSpec((None, GATE_RANK, D_MODEL), lambda i, j: (l, blk, 0))


def _norm(x, ln1, w_in_t, l, tm):
    r = x.shape[0]
    return pl.pallas_call(
        _norm_kernel,
        grid=(r // tm,),
        in_specs=[pl.BlockSpec((tm, D_MODEL), lambda i: (i, 0)), _ln_spec(l, 1), _wgk_spec(l, 1)],
        out_specs=[pl.BlockSpec((tm, D_MODEL), lambda i: (i, 0)), pl.BlockSpec((tm, GATE_RANK), lambda i: (i, 0))],
        out_shape=[jax.ShapeDtypeStruct((r, D_MODEL), BF16), jax.ShapeDtypeStruct((r, GATE_RANK), F32)],
        compiler_params=_cparams(("parallel",)),
        name="norm",
    )(x, ln1, w_in_t)


def _proj_kernel(sigmoid, h_ref, w_ref, o_ref, wb_scr):
    @pl.when(pl.program_id(1) == 0)
    def _():
        wb_scr[...] = w_ref[...].astype(BF16)

    y = _dot_nt(h_ref[...], wb_scr[...])
    if sigmoid:
        y = jax.nn.sigmoid(y)
    o_ref[...] = y.astype(o_ref.dtype)


def _proj(h, w_in_t, l, tm, tn, row_off, n_cols, sigmoid, out_dtype, name):
    r = h.shape[0]
    if row_off == 0:
        w_spec = pl.BlockSpec((None, tn, D_MODEL), lambda j, i: (l, j, 0))
    else:
        w_spec = pl.BlockSpec((pl.Squeezed(), pl.Element(tn), pl.Element(D_MODEL)),
                              lambda j, i: (l, pl.multiple_of(row_off + j * tn, GATE_RANK), 0))
    return pl.pallas_call(
        functools.partial(_proj_kernel, sigmoid),
        grid=(n_cols // tn, r // tm),
        in_specs=[pl.BlockSpec((tm, D_MODEL), lambda j, i: (i, 0)), w_spec],
        out_specs=pl.BlockSpec((tm, tn), lambda j, i: (i, j)),
        out_shape=jax.ShapeDtypeStruct((r, n_cols), out_dtype),
        scratch_shapes=[pltpu.VMEM((tn, D_MODEL), BF16)],
        compiler_params=_cparams(("arbitrary", "arbitrary")),
        name=name,
    )(h, w_in_t)


def _bucket_ranges():
    d = np.arange(WINDOW)
    max_exact = N_BUCKETS // 2
    df = np.maximum(d, 1).astype(np.float32)
    large = max_exact + (np.log(df / np.float32(max_exact)) / np.float32(math.log(MAX_DISTANCE / max_exact))
                         * np.float32(N_BUCKETS - max_exact)).astype(np.int32)
    large = np.minimum(large, N_BUCKETS - 1)
    bucket = np.where(d < max_exact, d, large)
    ranges = []
    for b in range(N_BUCKETS):
        idx = np.nonzero(bucket == b)[0]
        if idx.size:
            assert idx[-1] - idx[0] + 1 == idx.size
            ranges.append((b, int(idx[0]), int(idx[-1])))
    return ranges


def _bias_kernel(rel_ref, d_ref, o_ref):
    h = pl.program_id(0)
    d = d_ref[...]
    val = jnp.full(d.shape, NEG_BIG, F32)
    for b, lo, hi in _bucket_ranges():
        val = jnp.where((d >= lo) & (d <= hi), rel_ref[b, h], val)
    o_ref[...] = val


def _bias_table(rel_bias, dmap):
    mq, nk = dmap.shape
    return pl.pallas_call(
        _bias_kernel,
        grid=(N_HEADS_A,),
        in_specs=[
            pl.BlockSpec(memory_space=pltpu.SMEM),
            pl.BlockSpec((mq, nk), lambda h: (0, 0)),
        ],
        out_specs=pl.BlockSpec((None, mq, nk), lambda h: (h, 0, 0)),
        out_shape=jax.ShapeDtypeStruct((N_HEADS_A, mq, nk), F32),
        compiler_params=_cparams(("parallel",)),
        name="bias_table",
    )(rel_bias, dmap)


def _dmap_prompt():
    i = np.arange(WINDOW)[:, None]
    j = np.arange(2 * WINDOW)[None, :]
    d = i + WINDOW - j
    return np.where((d >= 0) & (d < WINDOW), d, -1).astype(np.int32)


def _dmap_sample(t_new, group):
    nk = group * WINDOW + LANES
    rows = np.arange(group * t_new)
    rb, rt = rows // t_new, rows % t_new
    d = np.full((group * t_new, nk), -1, np.int64)
    cols = np.arange(group * WINDOW)
    cb, cs = cols // WINDOW, cols % WINDOW
    dw = rt[:, None] + WINDOW - cs[None, :]
    ok = (rb[:, None] == cb[None, :]) & (dw >= 0) & (dw < WINDOW)
    d[:, :group * WINDOW] = np.where(ok, dw, -1)
    ncols = np.arange(group * t_new)
    nb, nu = ncols // t_new, ncols % t_new
    dn = rt[:, None] - nu[None, :]
    okn = (rb[:, None] == nb[None, :]) & (dn >= 0)
    d[:, group * WINDOW:group * WINDOW + group * t_new] = np.where(okn, dn, -1)
    return d.astype(np.int32)


def _group_ones():
    r = lax.broadcasted_iota(jnp.int32, (LANES, LANES), 0)
    c = lax.broadcasted_iota(jnp.int32, (LANES, LANES), 1)
    low_c = jnp.where(c < HEAD_DIM, 1.0, 0.0)
    return jnp.where(r < HEAD_DIM, low_c, 1.0 - low_c).astype(BF16)


def _head_norm(x, gmat, ln2):
    sq = x * x
    hi = sq.astype(BF16)
    lo = (sq - hi.astype(F32)).astype(BF16)
    ms = (_dot(hi, gmat) + _dot(lo, gmat)) * (1.0 / HEAD_DIM)
    return x * lax.rsqrt(ms + EPS) * ln2


def _split_heads(x):
    lane = lax.broadcasted_iota(jnp.int32, (x.shape[0], LANES), 1)
    low = lane < HEAD_DIM
    lo_parts, hi_parts = [], []
    for c in range(KV_WIDTH_A // LANES):
        xc = x[:, c * LANES:(c + 1) * LANES]
        xr = pltpu.roll(xc, HEAD_DIM, 1)
        lo_parts += [jnp.where(low, xc, 0.0).astype(BF16), jnp.where(low, xr, 0.0).astype(BF16)]
        hi_parts += [jnp.where(low, 0.0, xr).astype(BF16), jnp.where(low, 0.0, xc).astype(BF16)]
    return lo_parts, hi_parts


def _attend(q_cols, k, v, bias_ref, sink_ref, pen):
    k_lo, k_hi = _split_heads(k)
    v_lo, v_hi = _split_heads(v)
    group = N_HEADS_A // N_KV_A
    outs = []
    for pc in range(N_HEADS_A // 2):
        acc = None
        for half in range(2):
            h = 2 * pc + half
            kh = h // group
            kk = (k_lo, k_hi)[half][kh]
            vv = (v_lo, v_hi)[half][kh]
            s = _dot_nt(q_cols[pc], kk) + bias_ref[h]
            if pen is not None:
                s = s + pen
            sink = sink_ref[h]
            m = jnp.maximum(jnp.max(s, axis=-1, keepdims=True), sink)
            e = jnp.exp(s - m)
            den = jnp.sum(e, axis=-1, keepdims=True) + jnp.exp(sink - m)
            o = _dot(e.astype(BF16), vv) * (1.0 / den)
            acc = o if acc is None else acc + o
        outs.append(acc)
    return outs


def _attn_prompt_kernel(sink_ref, q_ref, kp_ref, ko_ref, vp_ref, vo_ref, lnq_ref, lnk_ref, bias_ref,
                        o_ref, kn_ref):
    n = pl.program_id(1)
    gmat = _group_ones()
    lnq = lnq_ref[...]
    lnk = lnk_ref[...]
    scale = HEAD_DIM ** -0.5

    def norm_k(ref):
        return jnp.concatenate(
            [_head_norm(ref[:, c * LANES:(c + 1) * LANES], gmat, lnk) for c in range(KV_WIDTH_A // LANES)], axis=1)

    kn_own = norm_k(ko_ref)
    kn_ref[...] = kn_own
    k = jnp.concatenate([norm_k(kp_ref), kn_own], axis=0)
    v = jnp.concatenate([vp_ref[...], vo_ref[...]], axis=0)
    q_cols = [(_head_norm(q_ref[:, c * LANES:(c + 1) * LANES], gmat, lnq) * scale).astype(BF16)
              for c in range(WIDTH_A // LANES)]
    col = lax.broadcasted_iota(jnp.int32, (1, 2 * WINDOW), 1)
    pen = jnp.where(col < WINDOW, jnp.where(n == 0, NEG_BIG, 0.0), 0.0)
    outs = _attend(q_cols, k, v, bias_ref, sink_ref, pen)
    for c, o in enumerate(outs):
        o_ref[:, c * LANES:(c + 1) * LANES] = o.astype(o_ref.dtype)


def _attn_prompt(proj, sinks_l, lnq2, lnk2, bias_p, batch, seq):
    nb = seq // WINDOW
    r = batch * seq
    kcol = OFF_KA // KV_WIDTH_A
    vcol = OFF_VA // KV_WIDTH_A
    own = lambda b, n: (b * nb + n, 0)
    return pl.pallas_call(
        _attn_prompt_kernel,
        grid=(batch, nb),
        in_specs=[
            pl.BlockSpec(memory_space=pltpu.SMEM),
            pl.BlockSpec((WINDOW, WIDTH_A), own),
            pl.BlockSpec((WINDOW, KV_WIDTH_A), lambda b, n: (b * nb + jnp.maximum(n - 1, 0), kcol)),
            pl.BlockSpec((WINDOW, KV_WIDTH_A), lambda b, n: (b * nb + n, kcol)),
            pl.BlockSpec((WINDOW, KV_WIDTH_A), lambda b, n: (b * nb + jnp.maximum(n - 1, 0), vcol)),
            pl.BlockSpec((WINDOW, KV_WIDTH_A), lambda b, n: (b * nb + n, vcol)),
            pl.BlockSpec((1, LANES), lambda b, n: (0, 0)),
            pl.BlockSpec((1, LANES), lambda b, n: (0, 0)),
            pl.BlockSpec((N_HEADS_A, WINDOW, 2 * WINDOW), lambda b, n: (0, 0, 0)),
        ],
        out_specs=[
            pl.BlockSpec((WINDOW, WIDTH_A), own),
            pl.BlockSpec((WINDOW, KV_WIDTH_A), own),
        ],
        out_shape=[
            jax.ShapeDtypeStruct((r, WIDTH_A), BF16),
            jax.ShapeDtypeStruct((r, KV_WIDTH_A), F32),
        ],
        compiler_params=_cparams(("parallel", "arbitrary")),
        name="attn_prompt",
    )(sinks_l, proj, proj, proj, proj, proj, lnq2, lnk2, bias_p)


def _attn_sample_kernel(sink_ref, q_ref, kn_in_ref, vn_ref, wk_ref, wv_ref, lnq_ref, lnk_ref, bias_ref,
                        o_ref, kn_ref):
    gmat = _group_ones()
    lnq = lnq_ref[...]
    lnk = lnk_ref[...]
    scale = HEAD_DIM ** -0.5
    rows = q_ref.shape[0]
    g = wk_ref.shape[0]
    kn_new = jnp.concatenate(
        [_head_norm(kn_in_ref[:, c * LANES:(c + 1) * LANES], gmat, lnk) for c in range(KV_WIDTH_A // LANES)], axis=1)
    kn_ref[...] = kn_new
    pad = jnp.zeros((LANES - rows, KV_WIDTH_A), F32)
    k = jnp.concatenate([wk_ref[...].reshape(g * WINDOW, KV_WIDTH_A), kn_new, pad], axis=0)
    v = jnp.concatenate([wv_ref[...].reshape(g * WINDOW, KV_WIDTH_A), vn_ref[...], pad], axis=0)
    q_cols = [(_head_norm(q_ref[:, c * LANES:(c + 1) * LANES], gmat, lnq) * scale).astype(BF16)
              for c in range(WIDTH_A // LANES)]
    outs = _attend(q_cols, k, v, bias_ref, sink_ref, None)
    for c, o in enumerate(outs):
        o_ref[:, c * LANES:(c + 1) * LANES] = o.astype(o_ref.dtype)


def _attn_sample(proj, win_k, win_v, sinks_l, lnq2, lnk2, bias_s, batch, t_new):
    g = SAMPLE_GROUP_A
    rows = g * t_new
    nk = g * WINDOW + LANES
    kcol = OFF_KA // KV_WIDTH_A
    vcol = OFF_VA // KV_WIDTH_A
    return pl.pallas_call(
        _attn_sample_kernel,
        grid=(batch // g,),
        in_specs=[
            pl.BlockSpec(memory_space=pltpu.SMEM),
            pl.BlockSpec((rows, WIDTH_A), lambda i: (i, 0)),
            pl.BlockSpec((rows, KV_WIDTH_A), lambda i: (i, kcol)),
            pl.BlockSpec((rows, KV_WIDTH_A), lambda i: (i, vcol)),
            pl.BlockSpec((g, WINDOW, KV_WIDTH_A), lambda i: (i, 0, 0)),
            pl.BlockSpec((g, WINDOW, KV_WIDTH_A), lambda i: (i, 0, 0)),
            pl.BlockSpec((1, LANES), lambda i: (0, 0)),
            pl.BlockSpec((1, LANES), lambda i: (0, 0)),
            pl.BlockSpec((N_HEADS_A, rows, nk), lambda i: (0, 0, 0)),
        ],
        out_specs=[
            pl.BlockSpec((rows, WIDTH_A), lambda i: (i, 0)),
            pl.BlockSpec((rows, KV_WIDTH_A), lambda i: (i, 0)),
        ],
        out_shape=[
            jax.ShapeDtypeStruct((batch * t_new, WIDTH_A), BF16),
            jax.ShapeDtypeStruct((batch * t_new, KV_WIDTH_A), F32),
        ],
        compiler_params=_cparams(("parallel",)),
        name="attn_sample",
    )(sinks_l, proj, proj, proj, win_k, win_v, lnq2, lnk2, bias_s)


def _log_decay(gk, w2_ref, bgk_ref):
    x = _dot(gk.astype(BF16), w2_ref[...]) + bgk_ref[...]
    log_sig = jnp.minimum(x, 0.0) - jnp.log(1.0 + jnp.exp(-jnp.abs(x)))
    return log_sig * (1.0 / GATE_TAU)


def _column(row):
    r = lax.broadcasted_iota(jnp.int32, (LANES, LANES), 0)
    c = lax.broadcasted_iota(jnp.int32, (LANES, LANES), 1)
    return jnp.sum(jnp.where(r == c, row, 0.0), axis=1, keepdims=True)


def _gla_out(o, ln_o, rgate):
    return (_rms(o, ln_o) * (rgate * jax.nn.sigmoid(rgate))).astype(BF16)


def _gla_prompt_kernel(q_ref, k_ref, va_ref, vb_ref, ra_ref, rb_ref, gk_ref, w2_ref, bgk_ref, lno_ref,
                       o_ref, s_out_ref, s_scr):
    t = pl.program_id(1)
    c_len = GLA_CHUNK_P
    n_chunks = q_ref.shape[0] // c_len
    half = c_len // 2
    scale = DK_B ** -0.5

    @pl.when(t == 0)
    def _():
        s_scr[...] = jnp.zeros_like(s_scr)

    ri = lax.broadcasted_iota(jnp.int32, (c_len, c_len), 0)
    ci = lax.broadcasted_iota(jnp.int32, (c_len, c_len), 1)
    causal = ri >= ci
    tri = jnp.where(causal, 1.0, 0.0).astype(BF16)
    ln_o = lno_ref[...]
    v_refs = (va_ref, vb_ref)
    r_refs = (ra_ref, rb_ref)

    def chunk(c, carry):
        rows = pl.ds(pl.multiple_of(c * c_len, c_len), c_len)
        g = _log_decay(gk_ref[rows, :], w2_ref, bgk_ref)
        b = _dot_exact_lhs01(tri, g)
        for h in range(N_HEADS_B):
            ks = slice(h * DK_B, (h + 1) * DK_B)
            vs = slice((h % 2) * DV_B, (h % 2 + 1) * DV_B)
            bh = b[:, ks]
            b_last = bh[c_len - 1:c_len, :]
            b_mid = bh[half - 1:half, :]
            q = q_ref[rows, ks] * scale
            k = k_ref[rows, ks]
            v = v_refs[h // 2][rows, vs].astype(BF16)
            q_inter = (q * jnp.exp(bh)).astype(BF16)
            q_t = (q * jnp.exp(bh - b_mid)).astype(BF16)
            k_t = (k * jnp.exp(b_mid - bh)).astype(BF16)
            a = jnp.where(causal, _dot_nt(q_t, k_t), 0.0).astype(BF16)
            k_state = (k * jnp.exp(b_last - bh)).astype(BF16)
            s_old = s_scr[h]
            o = _dot(a, v) + _dot(q_inter, s_old.astype(BF16))
            s_scr[h] = _column(jnp.exp(b_last)) * s_old + _dot_tn(k_state, v)
            o_ref[rows, h * DV_B:(h + 1) * DV_B] = _gla_out(o, ln_o, r_refs[h // 2][rows, vs])
        return carry

    lax.fori_loop(0, n_chunks, chunk, 0)

    @pl.when(t == pl.num_programs(1) - 1)
    def _():
        s_out_ref[...] = s_scr[...]


def _gla_prompt(proj, gk, w2_l, bgk_l, lno_l, batch, seq):
    tb = GLA_TBLOCK
    nt = seq // tb
    w = 2 * DV_B
    row = lambda b, t: b * nt + t
    spec = lambda col: pl.BlockSpec((tb, w), lambda b, t: (row(b, t), col))
    return pl.pallas_call(
        _gla_prompt_kernel,
        grid=(batch, nt),
        in_specs=[
            spec(OFF_QB // w), spec(OFF_KB // w),
            spec(OFF_VB // w), spec(OFF_VB // w + 1),
            spec(OFF_RB // w), spec(OFF_RB // w + 1),
            pl.BlockSpec((tb, GATE_RANK), lambda b, t: (row(b, t), 0)),
            pl.BlockSpec((GATE_RANK, WIDTH_BK), lambda b, t: (0, 0)),
            pl.BlockSpec((1, WIDTH_BK), lambda b, t: (0, 0)),
            pl.BlockSpec((1, DV_B), lambda b, t: (0, 0)),
        ],
        out_specs=[
            pl.BlockSpec((tb, WIDTH_BV), lambda b, t: (row(b, t), 0)),
            pl.BlockSpec((None, N_HEADS_B, DK_B, DV_B), lambda b, t: (b, 0, 0, 0)),
        ],
        out_shape=[
            jax.ShapeDtypeStruct((batch * seq, WIDTH_BV), BF16),
            jax.ShapeDtypeStruct((batch, N_HEADS_B, DK_B, DV_B), F32),
        ],
        scratch_shapes=[pltpu.VMEM((N_HEADS_B, DK_B, DV_B), F32)],
        compiler_params=_cparams(("parallel", "arbitrary")),
        name="gla_prompt",
    )(proj, proj, proj, proj, proj, proj, gk, w2_l, bgk_l, lno_l)


def _gla_sample_kernel(t_new, q_ref, k_ref, va_ref, vb_ref, ra_ref, rb_ref, gk_ref, w2_ref, bgk_ref, lno_ref,
                       s_ref, o_ref, s_out_ref):
    rows = q_ref.shape[0]
    n_seq = rows // t_new
    scale = DK_B ** -0.5
    ri = lax.broadcasted_iota(jnp.int32, (rows, rows), 0)
    ci = lax.broadcasted_iota(jnp.int32, (rows, rows), 1)
    same = None
    for sq in range(n_seq):
        lo, hi = sq * t_new, (sq + 1) * t_new
        blk = (ri >= lo) & (ri < hi) & (ci >= lo) & (ci < hi)
        same = blk if same is None else (same | blk)
    causal = same & (ri >= ci)
    tri = jnp.where(causal, 1.0, 0.0).astype(BF16)
    rcol = lax.broadcasted_iota(jnp.int32, (rows, 1), 0)
    ln_o = lno_ref[...]
    v_refs = (va_ref, vb_ref)
    r_refs = (ra_ref, rb_ref)

    g = _log_decay(gk_ref[...], w2_ref, bgk_ref)
    b = _dot_exact_lhs01(tri, g)
    for h in range(N_HEADS_B):
        ks = slice(h * DK_B, (h + 1) * DK_B)
        vs = slice((h % 2) * DV_B, (h % 2 + 1) * DV_B)
        bh = b[:, ks]
        gh = g[:, ks]
        q = q_ref[:, ks] * scale
        k = k_ref[:, ks]
        v = v_refs[h // 2][:, vs].astype(BF16)
        q_inter = q * jnp.exp(bh)
        k_t = (k * jnp.exp(-bh)).astype(BF16)
        a = jnp.where(causal, _dot_nt(q_inter.astype(BF16), k_t), 0.0).astype(BF16)
        o = _dot(a, v)
        for sq in range(n_seq):
            mine = (rcol >= sq * t_new) & (rcol < (sq + 1) * t_new)
            b_last = jnp.sum(jnp.where(mine, gh, 0.0), axis=0, keepdims=True)
            k_state = jnp.where(mine, k * jnp.exp(b_last - bh), 0.0).astype(BF16)
            q_mine = jnp.where(mine, q_inter, 0.0).astype(BF16)
            s_old = s_ref[sq, h]
            o = o + _dot(q_mine, s_old.astype(BF16))
            s_out_ref[sq, h] = _column(jnp.exp(b_last)) * s_old + _dot_tn(k_state, v)
        o_ref[:, h * DV_B:(h + 1) * DV_B] = _gla_out(o, ln_o, r_refs[h // 2][:, vs])


def _gla_sample(proj, gk, w2_l, bgk_l, lno_l, state, l, batch, t_new):
    g = SAMPLE_GROUP_B
    rows = g * t_new
    w = 2 * DV_B
    spec = lambda col: pl.BlockSpec((rows, w), lambda i: (i, col))
    return pl.pallas_call(
        functools.partial(_gla_sample_kernel, t_new),
        grid=(batch // g,),
        in_specs=[
            spec(OFF_QB // w), spec(OFF_KB // w),
            spec(OFF_VB // w), spec(OFF_VB // w + 1),
            spec(OFF_RB // w), spec(OFF_RB // w + 1),
            pl.BlockSpec((rows, GATE_RANK), lambda i: (i, 0)),
            pl.BlockSpec((GATE_RANK, WIDTH_BK), lambda i: (0, 0)),
            pl.BlockSpec((1, WIDTH_BK), lambda i: (0, 0)),
            pl.BlockSpec((1, DV_B), lambda i: (0, 0)),
            pl.BlockSpec((None, g, N_HEADS_B, DK_B, DV_B), lambda i: (l, i, 0, 0, 0)),
        ],
        out_specs=[
            pl.BlockSpec((rows, WIDTH_BV), lambda i: (i, 0)),
            pl.BlockSpec((g, N_HEADS_B, DK_B, DV_B), lambda i: (i, 0, 0, 0)),
        ],
        out_shape=[
            jax.ShapeDtypeStruct((batch * t_new, WIDTH_BV), BF16),
            jax.ShapeDtypeStruct((batch, N_HEADS_B, DK_B, DV_B), F32),
        ],
        compiler_params=_cparams(("parallel",)),
        name="gla_sample",
    )(proj, proj, proj, proj, proj, proj, gk, w2_l, bgk_l, lno_l, state)


def _merge_kernel(oa_ref, ob_ref, sa_ref, sb_ref, x_ref, woa_ref, wob_ref, wout_ref, ln_ref, x_out_ref, h_ref):
    a = _dot(oa_ref[...], woa_ref[...])
    b = _dot(ob_ref[...], wob_ref[...])
    mix = (sa_ref[...].astype(F32) * a + sb_ref[...].astype(F32) * b).astype(BF16)
    x_new = x_ref[...] + _dot(mix, wout_ref[...])
    x_out_ref[...] = x_new
    h_ref[...] = _rms(x_new, ln_ref[...]).astype(BF16)


def _merge(oa, ob, gates, x, w_oa, w_ob, w_out, ln2, l, tm):
    r = x.shape[0]
    resident = dict(pipeline_mode=pl.Buffered(1))
    row = lambda i: (i, 0)
    return pl.pallas_call(
        _merge_kernel,
        grid=(r // tm,),
        in_specs=[
            pl.BlockSpec((tm, WIDTH_A), row),
            pl.BlockSpec((tm, WIDTH_BV), row),
            pl.BlockSpec((tm, D_MODEL), lambda i: (i, 0)),
            pl.BlockSpec((tm, D_MODEL), lambda i: (i, 1)),
            pl.BlockSpec((tm, D_MODEL), row),
            pl.BlockSpec((None, WIDTH_A, D_MODEL), lambda i: (l, 0, 0), **resident),
            pl.BlockSpec((None, WIDTH_BV, D_MODEL), lambda i: (l, 0, 0), **resident),
            pl.BlockSpec((None, D_MODEL, D_MODEL), lambda i: (l, 0, 0), **resident),
            _ln_spec(l, 1),
        ],
        out_specs=[pl.BlockSpec((tm, D_MODEL), row), pl.BlockSpec((tm, D_MODEL), row)],
        out_shape=[jax.ShapeDtypeStruct((r, D_MODEL), F32), jax.ShapeDtypeStruct((r, D_MODEL), BF16)],
        compiler_params=_cparams(("parallel",)),
        name="merge",
    )(oa, ob, gates, gates, x, w_oa, w_ob, w_out, ln2)


def _gelu_tanh(x):
    return x * (0.5 * (1.0 + jnp.tanh(math.sqrt(2.0 / math.pi) * (x + 0.044715 * (x * x * x)))))


def _conv3(u, p1, p2, w_ref, b_ref):
    return w_ref[0:1, :] * p2 + w_ref[1:2, :] * p1 + w_ref[2:3, :] * u + b_ref[...]


def _ffn_up_prompt_kernel(tiles_per_seq, h_ref, wv_ref, wg_ref, cv_ref, cg_ref, bv_ref, bg_ref,
                          act_ref, tv_ref, tg_ref, wvb_scr, wgb_scr, carry_v, carry_g):
    i = pl.program_id(1)

    @pl.when(i == 0)
    def _():
        wvb_scr[...] = wv_ref[...].astype(BF16)
        wgb_scr[...] = wg_ref[...].astype(BF16)

    h = h_ref[...]
    tm = h.shape[0]
    first = (i % tiles_per_seq) == 0
    row8 = lax.broadcasted_iota(jnp.int32, (8, 1), 0)

    def branch(wb_scr, carry, c_ref, b_ref, tail_ref):
        u = _dot(h, wb_scr[...])
        prev = jnp.where(first, 0.0, carry[...])
        tail = u[tm - 8:tm, :]
        carry[...] = tail
        tail_ref[...] = tail
        p1 = pltpu.roll(u, 1, 0)
        p2 = pltpu.roll(u, 2, 0)
        body = _conv3(u, p1, p2, c_ref, b_ref)
        p1_top = jnp.where(row8 < 1, pltpu.roll(prev, 1, 0), p1[0:8, :])
        p2_top = jnp.where(row8 < 2, pltpu.roll(prev, 2, 0), p2[0:8, :])
        top = _conv3(u[0:8, :], p1_top, p2_top, c_ref, b_ref)
        return jnp.concatenate([top, body[8:, :]], axis=0)

    val = branch(wvb_scr, carry_v, cv_ref, bv_ref, tv_ref)
    gate = branch(wgb_scr, carry_g, cg_ref, bg_ref, tg_ref)
    act_ref[...] = (_gelu_tanh(gate) * val).astype(act_ref.dtype)


def _ffn_up_prompt(h, w_up, w_conv, b_conv, l, tm, tiles_per_seq):
    r = h.shape[0]
    tf = TF_UP
    nt = r // tm
    nj = D_FF // tf
    w_spec = lambda off: pl.BlockSpec((None, D_MODEL, tf), lambda j, i: (l, 0, j + off))
    c_spec = lambda off: pl.BlockSpec((None, CONV_W, tf), lambda j, i: (l, 0, j + off))
    b_spec = lambda off: pl.BlockSpec((None, 1, tf), lambda j, i: (l, 0, j + off))
    tail_spec = pl.BlockSpec((None, 8, tf), lambda j, i: (i, 0, j))
    return pl.pallas_call(
        functools.partial(_ffn_up_prompt_kernel, tiles_per_seq),
        grid=(nj, nt),
        in_specs=[
            pl.BlockSpec((tm, D_MODEL), lambda j, i: (i, 0)),
            w_spec(0), w_spec(nj), c_spec(0), c_spec(nj), b_spec(0), b_spec(nj),
        ],
        out_specs=[pl.BlockSpec((None, tm, tf), lambda j, i: (j, i, 0)), tail_spec, tail_spec],
        out_shape=[
            jax.ShapeDtypeStruct((nj, r, tf), BF16),
            jax.ShapeDtypeStruct((nt, 8, D_FF), F32),
            jax.ShapeDtypeStruct((nt, 8, D_FF), F32),
        ],
        scratch_shapes=[
            pltpu.VMEM((D_MODEL, tf), BF16),
            pltpu.VMEM((D_MODEL, tf), BF16),
            pltpu.VMEM((8, tf), F32),
            pltpu.VMEM((8, tf), F32),
        ],
        compiler_params=_cparams(("arbitrary", "arbitrary")),
        name="ffn_up_prompt",
    )(h, w_up, w_up, w_conv, w_conv, b_conv, b_conv)


def _ffn_up_sample_kernel(pos_ref, h_ref, wv_ref, wg_ref, cv_ref, cg_ref, bv_ref, bg_ref,
                          h1v_ref, h2v_ref, h1g_ref, h2g_ref, act_ref, uv_ref, ug_ref):
    h = h_ref[...]
    pos = pos_ref[...]

    def branch(w_ref, c_ref, b_ref, h1_ref, h2_ref, u_ref):
        u = _dot(h, w_ref[...].astype(BF16))
        u_ref[...] = u
        p1 = jnp.where(pos >= 1, pltpu.roll(u, 1, 0), h1_ref[...])
        p2 = jnp.where(pos >= 2, pltpu.roll(u, 2, 0), h2_ref[...])
        return _conv3(u, p1, p2, c_ref, b_ref)

    val = branch(wv_ref, cv_ref, bv_ref, h1v_ref, h2v_ref, uv_ref)
    gate = branch(wg_ref, cg_ref, bg_ref, h1g_ref, h2g_ref, ug_ref)
    act_ref[...] = (_gelu_tanh(gate) * val).astype(act_ref.dtype)


def _ffn_up_sample(h, w_up, w_conv, b_conv, hist1, hist2, l, t_new):
    r = h.shape[0]
    tf = TF_UP
    nj = D_FF // tf
    w_spec = lambda off: pl.BlockSpec((None, D_MODEL, tf), lambda j: (l, 0, j + off))
    c_spec = lambda off: pl.BlockSpec((None, CONV_W, tf), lambda j: (l, 0, j + off))
    b_spec = lambda off: pl.BlockSpec((None, 1, tf), lambda j: (l, 0, j + off))
    h_spec = lambda off: pl.BlockSpec((None, r, tf), lambda j: (l, 0, j + off))
    u_spec = pl.BlockSpec((r, tf), lambda j: (0, j))
    pos = jnp.asarray((np.arange(r) % t_new).astype(np.int32).reshape(r, 1))
    return pl.pallas_call(
        _ffn_up_sample_kernel,
        grid=(nj,),
        in_specs=[
            pl.BlockSpec((r, 1), lambda j: (0, 0)),
            pl.BlockSpec((r, D_MODEL), lambda j: (0, 0)),
            w_spec(0), w_spec(nj), c_spec(0), c_spec(nj), b_spec(0), b_spec(nj),
            h_spec(0), h_spec(0), h_spec(nj), h_spec(nj),
        ],
        out_specs=[pl.BlockSpec((None, r, tf), lambda j: (j, 0, 0)), u_spec, u_spec],
        out_shape=[
            jax.ShapeDtypeStruct((nj, r, tf), BF16),
            jax.ShapeDtypeStruct((r, D_FF), F32),
            jax.ShapeDtypeStruct((r, D_FF), F32),
        ],
        compiler_params=_cparams(("arbitrary",)),
        name="ffn_up_sample",
    )(pos, h, w_up, w_up, w_conv, w_conv, b_conv, b_conv, hist1, hist2, hist1, hist2)


def _down_accumulate(act_ref, w_ref, x_ref, x_out_ref):
    @pl.when(pl.program_id(1) == 0)
    def _():
        x_out_ref[...] = x_ref[...]

    act = act_ref[...]
    step = 512
    for n in range(D_MODEL // step):
        cols = slice(n * step, (n + 1) * step)
        x_out_ref[:, cols] += _dot(act, w_ref[:, cols])


def _down_last_kernel(act_ref, w_ref, x_ref, x_out_ref):
    _down_accumulate(act_ref, w_ref, x_ref, x_out_ref)


def _down_norm_kernel(act_ref, w_ref, x_ref, ln_ref, wgk_ref, x_out_ref, h_ref, gk_ref):
    _down_accumulate(act_ref, w_ref, x_ref, x_out_ref)

    @pl.when(pl.program_id(1) == pl.num_programs(1) - 1)
    def _():
        h = _rms(x_out_ref[...], ln_ref[...]).astype(BF16)
        h_ref[...] = h
        gk_ref[...] = _dot_nt(h, wgk_ref[...].astype(BF16))


def _down(act, w_down, x, ln1, w_in_t, l, tm, with_norm):
    nk, r, tk = act.shape
    row = lambda i, k: (i, 0)
    in_specs = [
        pl.BlockSpec((None, tm, tk), lambda i, k: (k, i, 0)),
        pl.BlockSpec((None, tk, D_MODEL), lambda i, k: (l, k, 0)),
        pl.BlockSpec((tm, D_MODEL), row),
    ]
    out_specs = [pl.BlockSpec((tm, D_MODEL), row)]
    out_shape = [jax.ShapeDtypeStruct((r, D_MODEL), F32)]
    args = [act, w_down, x]
    if with_norm:
        in_specs += [_ln_spec(l + 1, 2), _wgk_spec(l + 1, 2)]
        out_specs += [pl.BlockSpec((tm, D_MODEL), row), pl.BlockSpec((tm, GATE_RANK), row)]
        out_shape += [jax.ShapeDtypeStruct((r, D_MODEL), BF16), jax.ShapeDtypeStruct((r, GATE_RANK), F32)]
        args += [ln1, w_in_t]
    out = pl.pallas_call(
        _down_norm_kernel if with_norm else _down_last_kernel,
        grid=(r // tm, nk),
        in_specs=in_specs,
        out_specs=out_specs,
        out_shape=out_shape,
        compiler_params=_cparams(("parallel", "arbitrary")),
        name="down_norm" if with_norm else "down_last",
    )(*args)
    return out if with_norm else (out[0], None, None)


def kernel(x_prompt, x_sample, cache_win_k, cache_win_v, state_gla, state_conv, rel_bias,
           ln1, w_in, ln_q, ln_k, sinks, w_gk2, b_gk, ln_o, w_oa, w_ob, w_out,
           ln2, w_up, w_conv, b_conv, w_down):
    bp, seq, _ = x_prompt.shape
    bs, t_new, _ = x_sample.shape
    tm_p = TM_PROMPT
    tm_s = bs * t_new
    tiles_per_seq = seq // tm_p

    ln1_3 = ln1.reshape(DEPTH, 1, D_MODEL)
    ln2_3 = ln2.reshape(DEPTH, 1, D_MODEL)
    w_in_t = jnp.swapaxes(w_in, 1, 2)
    w_oa_b, w_ob_b, w_out_b, w_down_b = (w.astype(BF16) for w in (w_oa, w_ob, w_out, w_down))
    w2_b = w_gk2.astype(BF16)
    b_conv3 = b_conv.reshape(DEPTH, 1, 2 * D_FF)
    lnq2 = jnp.tile(ln_q, (1, 2)).reshape(DEPTH, 1, LANES)
    lnk2 = jnp.tile(ln_k, (1, 2)).reshape(DEPTH, 1, LANES)

    bias_p = _bias_table(rel_bias, jnp.asarray(_dmap_prompt()))
    bias_s = _bias_table(rel_bias, jnp.asarray(_dmap_sample(t_new, SAMPLE_GROUP_A)))

    xp = x_prompt.reshape(bp * seq, D_MODEL)
    xs = x_sample.reshape(bs * t_new, D_MODEL)
    win_k = cache_win_k.reshape(DEPTH, bs, WINDOW, KV_WIDTH_A)
    win_v = cache_win_v.reshape(DEPTH, bs, WINDOW, KV_WIDTH_A)
    zeros = jnp.zeros((DEPTH, bs, 1, 2 * D_FF), F32)
    hist1 = jnp.concatenate([state_conv[:, :, 1:2], zeros, zeros, zeros], axis=2).reshape(DEPTH, tm_s, 2 * D_FF)
    hist2 = jnp.concatenate([state_conv[:, :, 0:1], state_conv[:, :, 1:2], zeros, zeros], axis=2)
    hist2 = hist2.reshape(DEPTH, tm_s, 2 * D_FF)

    def dense_in(h, l, tm):
        proj = _proj(h, w_in_t, l, tm, TN_MAIN, 0, MAIN_COLS, False, F32, "proj_main")
        gates = _proj(h, w_in_t, l, tm, TN_GATES, OFF_GA, GATE_COLS, True, BF16, "proj_gates")
        return proj, gates

    hp, gkp = _norm(xp, ln1_3, w_in_t, 0, 512)
    hs, gks = _norm(xs, ln1_3, w_in_t, 0, tm_s)

    kp, vp, gp, cp = [], [], [], []
    ksm, vsm, gsm, csm = [], [], [], []
    for l in range(DEPTH):
        lq, lk, sk = lnq2[l], lnk2[l], sinks[l]
        w2_l, bgk_l, lno_l = w2_b[l], b_gk[l].reshape(1, WIDTH_BK), ln_o[l].reshape(1, DV_B)
        more = l + 1 < DEPTH

        proj, gates = dense_in(hp, l, tm_p)
        oa, kn = _attn_prompt(proj, sk, lq, lk, bias_p, bp, seq)
        ob, s_new = _gla_prompt(proj, gkp, w2_l, bgk_l, lno_l, bp, seq)
        xp, h2 = _merge(oa, ob, gates, xp, w_oa_b, w_ob_b, w_out_b, ln2_3, l, TM_MERGE)
        act, tail_v, tail_g = _ffn_up_prompt(h2, w_up, w_conv, b_conv3, l, tm_p, tiles_per_seq)
        xp, hp, gkp = _down(act, w_down_b, xp, ln1_3, w_in_t, l, tm_p, more)
        kp.append(kn.reshape(bp, seq, N_KV_A, HEAD_DIM)[:, -WINDOW:])
        vp.append(proj[:, OFF_VA:OFF_VA + KV_WIDTH_A].reshape(bp, seq, N_KV_A, HEAD_DIM)[:, -WINDOW:])
        gp.append(s_new)
        tails = jnp.concatenate([tail_v, tail_g], axis=-1).reshape(bp, tiles_per_seq, 8, 2 * D_FF)
        cp.append(tails[:, -1, 8 - (CONV_W - 1):])

        proj, gates = dense_in(hs, l, tm_s)
        oa, kn = _attn_sample(proj, win_k[l], win_v[l], sk, lq, lk, bias_s, bs, t_new)
        ob, s_new = _gla_sample(proj, gks, w2_l, bgk_l, lno_l, state_gla, l, bs, t_new)
        xs, h2 = _merge(oa, ob, gates, xs, w_oa_b, w_ob_b, w_out_b, ln2_3, l, tm_s)
        act, u_v, u_g = _ffn_up_sample(h2, w_up, w_conv, b_conv3, hist1, hist2, l, t_new)
        xs, hs, gks = _down(act, w_down_b, xs, ln1_3, w_in_t, l, tm_s, more)
        k_new = kn.reshape(bs, t_new, N_KV_A, HEAD_DIM)
        v_new = proj[:, OFF_VA:OFF_VA + KV_WIDTH_A].reshape(bs, t_new, N_KV_A, HEAD_DIM)
        ksm.append(jnp.concatenate([cache_win_k[l][:, t_new:], k_new], axis=1))
        vsm.append(jnp.concatenate([cache_win_v[l][:, t_new:], v_new], axis=1))
        gsm.append(s_new)
        u = jnp.concatenate([u_v, u_g], axis=-1).reshape(bs, t_new, 2 * D_FF)
        csm.append(u[:, t_new - (CONV_W - 1):])

    return (xp.reshape(bp, seq, D_MODEL), xs.reshape(bs, t_new, D_MODEL),
            jnp.stack(kp), jnp.stack(vp), jnp.stack(gp), jnp.stack(cp),
            jnp.stack(ksm), jnp.stack(vsm), jnp.stack(gsm), jnp.stack(csm))
```

```python
import functools
import math

import numpy as np
import jax
import jax.numpy as jnp
from jax import lax
from jax.experimental import pallas as pl
from jax.experimental.pallas import tpu as pltpu

F32 = jnp.float32
BF16 = jnp.bfloat16

D_MODEL = 2048
DEPTH = 4
HEAD_DIM = 64
N_HEADS_A = 16
N_KV_A = 4
WIDTH_A = N_HEADS_A * HEAD_DIM
KV_WIDTH_A = N_KV_A * HEAD_DIM
WINDOW = 128
N_BUCKETS = 32
MAX_DISTANCE = 128
N_HEADS_B = 4
DK_B = 128
DV_B = 256
WIDTH_BK = N_HEADS_B * DK_B
WIDTH_BV = N_HEADS_B * DV_B
GATE_RANK = 16
GATE_TAU = 16.0
D_FF = 5632
CONV_W = 3
EPS = 1e-6

OFF_QA = 0
OFF_KA = OFF_QA + WIDTH_A
OFF_VA = OFF_KA + KV_WIDTH_A
OFF_QB = OFF_VA + KV_WIDTH_A
OFF_KB = OFF_QB + WIDTH_BK
OFF_VB = OFF_KB + WIDTH_BK
OFF_RB = OFF_VB + WIDTH_BV
OFF_GK = OFF_RB + WIDTH_BV
OFF_GA = OFF_GK + GATE_RANK
MAIN_COLS = OFF_GK
GATE_COLS = 2 * D_MODEL

LANES = 128
NEG_BIG = -1e30
VMEM_LIMIT = 56 * 1024 * 1024

TM_PROMPT = 1024
TM_MERGE = 256
TN_MAIN = 768
TN_GATES = 1024
TF_UP = 512
FFN_SUB_ROWS = 256
FFN_SUB_COLS = 512
GLA_CHUNK_P = 64
GLA_TBLOCK = 512
SAMPLE_GROUP_A = 8
SAMPLE_GROUP_B = 4


def _cparams(sem):
    return pltpu.CompilerParams(dimension_semantics=sem, vmem_limit_bytes=VMEM_LIMIT)


def _dot(a, b):
    return jnp.dot(a, b, preferred_element_type=F32)


def _dot_nt(a, b):
    return lax.dot_general(a, b, (((1,), (1,)), ((), ())), preferred_element_type=F32)


def _dot_tn(a, b):
    return lax.dot_general(a, b, (((0,), (0,)), ((), ())), preferred_element_type=F32)


def _split3(x):
    hi = x.astype(BF16)
    r = x - hi.astype(F32)
    mid = r.astype(BF16)
    lo = (r - mid.astype(F32)).astype(BF16)
    return hi, mid, lo


def _dot_exact_lhs01(m01, x):
    hi, mid, lo = _split3(x)
    return _dot(m01, hi) + _dot(m01, mid) + _dot(m01, lo)


def _rms(x, g):
    ms = jnp.mean(x * x, axis=-1, keepdims=True)
    return x * lax.rsqrt(ms + EPS) * g


def _norm_kernel(x_ref, ln_ref, wgk_ref, h_ref, gk_ref):
    h = _rms(x_ref[...], ln_ref[...]).astype(BF16)
    h_ref[...] = h
    gk_ref[...] = _dot_nt(h, wgk_ref[...].astype(BF16))


def _ln_spec(l, nargs):
    if nargs == 1:
        return pl.BlockSpec((None, 1, D_MODEL), lambda i: (l, 0, 0))
    return pl.BlockSpec((None, 1, D_MODEL), lambda i, j: (l, 0, 0))


def _wgk_spec(l, nargs):
    blk = OFF_GK // GATE_RANK
    if nargs == 1:
        return pl.BlockSpec((None, GATE_RANK, D_MODEL), lambda i: (l, blk, 0))
    return pl.BlockSpec((None, GATE_RANK, D_MODEL), lambda i, j: (l, blk, 0))


def _norm(x, ln1, w_in_t, l, tm):
    r = x.shape[0]
    return pl.pallas_call(
        _norm_kernel,
        grid=(r // tm,),
        in_specs=[pl.BlockSpec((tm, D_MODEL), lambda i: (i, 0)), _ln_spec(l, 1), _wgk_spec(l, 1)],
        out_specs=[pl.BlockSpec((tm, D_MODEL), lambda i: (i, 0)), pl.BlockSpec((tm, GATE_RANK), lambda i: (i, 0))],
        out_shape=[jax.ShapeDtypeStruct((r, D_MODEL), BF16), jax.ShapeDtypeStruct((r, GATE_RANK), F32)],
        compiler_params=_cparams(("parallel",)),
        name="norm",
    )(x, ln1, w_in_t)


def _proj_kernel(sigmoid, h_ref, hs_ref, w_ref, o_ref, os_ref, wb_scr):
    i = pl.program_id(1)
    n_prompt = pl.num_programs(1) - 1

    @pl.when(i == 0)
    def _():
        wb_scr[...] = w_ref[...].astype(BF16)

    def project(src_ref, dst_ref):
        y = _dot_nt(src_ref[...], wb_scr[...])
        if sigmoid:
            y = jax.nn.sigmoid(y)
        dst_ref[...] = y.astype(dst_ref.dtype)

    @pl.when(i < n_prompt)
    def _():
        project(h_ref, o_ref)

    @pl.when(i == n_prompt)
    def _():
        project(hs_ref, os_ref)


def _proj(h, hs, w_in_t, l, tm, tn, row_off, n_cols, sigmoid, out_dtype, name):
    r = h.shape[0]
    rs = hs.shape[0]
    nt = r // tm
    if row_off == 0:
        w_spec = pl.BlockSpec((None, tn, D_MODEL), lambda j, i: (l, j, 0))
    else:
        w_spec = pl.BlockSpec((pl.Squeezed(), pl.Element(tn), pl.Element(D_MODEL)),
                              lambda j, i: (l, pl.multiple_of(row_off + j * tn, GATE_RANK), 0))
    prompt_tile = lambda i: jnp.minimum(i, nt - 1)
    return pl.pallas_call(
        functools.partial(_proj_kernel, sigmoid),
        grid=(n_cols // tn, nt + 1),
        in_specs=[
            pl.BlockSpec((tm, D_MODEL), lambda j, i: (prompt_tile(i), 0)),
            pl.BlockSpec((rs, D_MODEL), lambda j, i: (0, 0)),
            w_spec,
        ],
        out_specs=[
            pl.BlockSpec((tm, tn), lambda j, i: (prompt_tile(i), j)),
            pl.BlockSpec((rs, tn), lambda j, i: (0, j)),
        ],
        out_shape=[jax.ShapeDtypeStruct((r, n_cols), out_dtype), jax.ShapeDtypeStruct((rs, n_cols), out_dtype)],
        scratch_shapes=[pltpu.VMEM((tn, D_MODEL), BF16)],
        compiler_params=_cparams(("arbitrary", "arbitrary")),
        name=name,
    )(h, hs, w_in_t)


def _bucket_ranges():
    d = np.arange(WINDOW)
    max_exact = N_BUCKETS // 2
    df = np.maximum(d, 1).astype(np.float32)
    large = max_exact + (np.log(df / np.float32(max_exact)) / np.float32(math.log(MAX_DISTANCE / max_exact))
                         * np.float32(N_BUCKETS - max_exact)).astype(np.int32)
    large = np.minimum(large, N_BUCKETS - 1)
    bucket = np.where(d < max_exact, d, large)
    ranges = []
    for b in range(N_BUCKETS):
        idx = np.nonzero(bucket == b)[0]
        if idx.size:
            assert idx[-1] - idx[0] + 1 == idx.size
            ranges.append((b, int(idx[0]), int(idx[-1])))
    return ranges


def _bias_kernel(rel_ref, d_ref, o_ref):
    h = pl.program_id(0)
    d = d_ref[...]
    val = jnp.full(d.shape, NEG_BIG, F32)
    for b, lo, hi in _bucket_ranges():
        val = jnp.where((d >= lo) & (d <= hi), rel_ref[b, h], val)
    o_ref[...] = val


def _bias_table(rel_bias, dmap):
    mq, nk = dmap.shape
    return pl.pallas_call(
        _bias_kernel,
        grid=(N_HEADS_A,),
        in_specs=[
            pl.BlockSpec(memory_space=pltpu.SMEM),
            pl.BlockSpec((mq, nk), lambda h: (0, 0)),
        ],
        out_specs=pl.BlockSpec((None, mq, nk), lambda h: (h, 0, 0)),
        out_shape=jax.ShapeDtypeStruct((N_HEADS_A, mq, nk), F32),
        compiler_params=_cparams(("parallel",)),
        name="bias_table",
    )(rel_bias, dmap)


def _dmap_prompt(first_block):
    i = np.arange(WINDOW)[:, None]
    j = np.arange(2 * WINDOW)[None, :]
    d = i + WINDOW - j
    ok = (d >= 0) & (d < WINDOW)
    if first_block:
        ok = ok & (j >= WINDOW)
    return np.where(ok, d, -1).astype(np.int32)


def _dmap_sample(t_new, group):
    nk = group * WINDOW + LANES
    rows = np.arange(group * t_new)
    rb, rt = rows // t_new, rows % t_new
    d = np.full((group * t_new, nk), -1, np.int64)
    cols = np.arange(group * WINDOW)
    cb, cs = cols // WINDOW, cols % WINDOW
    dw = rt[:, None] + WINDOW - cs[None, :]
    ok = (rb[:, None] == cb[None, :]) & (dw >= 0) & (dw < WINDOW)
    d[:, :group * WINDOW] = np.where(ok, dw, -1)
    ncols = np.arange(group * t_new)
    nb, nu = ncols // t_new, ncols % t_new
    dn = rt[:, None] - nu[None, :]
    okn = (rb[:, None] == nb[None, :]) & (dn >= 0)
    d[:, group * WINDOW:group * WINDOW + group * t_new] = np.where(okn, dn, -1)
    return d.astype(np.int32)


def _group_ones():
    r = lax.broadcasted_iota(jnp.int32, (LANES, LANES), 0)
    c = lax.broadcasted_iota(jnp.int32, (LANES, LANES), 1)
    low_c = jnp.where(c < HEAD_DIM, 1.0, 0.0)
    return jnp.where(r < HEAD_DIM, low_c, 1.0 - low_c).astype(BF16)


def _head_norm(x, gmat, ln2):
    sq = x * x
    hi = sq.astype(BF16)
    lo = (sq - hi.astype(F32)).astype(BF16)
    ms = (_dot(hi, gmat) + _dot(lo, gmat)) * (1.0 / HEAD_DIM)
    return x * lax.rsqrt(ms + EPS) * ln2


def _split_heads(x):
    lane = lax.broadcasted_iota(jnp.int32, (x.shape[0], LANES), 1)
    low = lane < HEAD_DIM
    lo_parts, hi_parts = [], []
    for c in range(KV_WIDTH_A // LANES):
        xc = x[:, c * LANES:(c + 1) * LANES]
        xr = pltpu.roll(xc, HEAD_DIM, 1)
        lo_parts += [jnp.where(low, xc, 0.0).astype(BF16), jnp.where(low, xr, 0.0).astype(BF16)]
        hi_parts += [jnp.where(low, 0.0, xr).astype(BF16), jnp.where(low, 0.0, xc).astype(BF16)]
    return lo_parts, hi_parts


def _attend(q_cols, k, v, bias_ref, sink_ref):
    k_lo, k_hi = _split_heads(k)
    v_lo, v_hi = _split_heads(v)
    group = N_HEADS_A // N_KV_A
    outs = []
    for pc in range(N_HEADS_A // 2):
        acc = None
        for half in range(2):
            h = 2 * pc + half
            kh = h // group
            kk = (k_lo, k_hi)[half][kh]
            vv = (v_lo, v_hi)[half][kh]
            s = _dot_nt(q_cols[pc], kk) + bias_ref[h]
            sink = sink_ref[h]
            m = jnp.maximum(jnp.max(s, axis=-1, keepdims=True), sink)
            e = jnp.exp(s - m)
            den = jnp.sum(e, axis=-1, keepdims=True) + jnp.exp(sink - m)
            o = _dot(e.astype(BF16), vv) * (1.0 / den)
            acc = o if acc is None else acc + o
        outs.append(acc)
    return outs


def _attn_prompt_kernel(sink_ref, q_ref, kp_ref, ko_ref, vp_ref, vo_ref, lnq_ref, lnk_ref, bias_ref,
                        o_ref, kn_ref):
    gmat = _group_ones()
    lnq = lnq_ref[...]
    lnk = lnk_ref[...]
    scale = HEAD_DIM ** -0.5

    def norm_k(ref):
        return jnp.concatenate(
            [_head_norm(ref[:, c * LANES:(c + 1) * LANES], gmat, lnk) for c in range(KV_WIDTH_A // LANES)], axis=1)

    kn_own = norm_k(ko_ref)
    kn_ref[...] = kn_own
    k = jnp.concatenate([norm_k(kp_ref), kn_own], axis=0)
    v = jnp.concatenate([vp_ref[...], vo_ref[...]], axis=0)
    q_cols = [(_head_norm(q_ref[:, c * LANES:(c + 1) * LANES], gmat, lnq) * scale).astype(BF16)
              for c in range(WIDTH_A // LANES)]
    outs = _attend(q_cols, k, v, bias_ref, sink_ref)
    for c, o in enumerate(outs):
        o_ref[:, c * LANES:(c + 1) * LANES] = o.astype(o_ref.dtype)


def _attn_prompt(proj, sinks_l, lnq2, lnk2, bias_p, batch, seq):
    nb = seq // WINDOW
    r = batch * seq
    kcol = OFF_KA // KV_WIDTH_A
    vcol = OFF_VA // KV_WIDTH_A
    own = lambda b, n: (b * nb + n, 0)
    return pl.pallas_call(
        _attn_prompt_kernel,
        grid=(batch, nb),
        in_specs=[
            pl.BlockSpec(memory_space=pltpu.SMEM),
            pl.BlockSpec((WINDOW, WIDTH_A), own),
            pl.BlockSpec((WINDOW, KV_WIDTH_A), lambda b, n: (b * nb + jnp.maximum(n - 1, 0), kcol)),
            pl.BlockSpec((WINDOW, KV_WIDTH_A), lambda b, n: (b * nb + n, kcol)),
            pl.BlockSpec((WINDOW, KV_WIDTH_A), lambda b, n: (b * nb + jnp.maximum(n - 1, 0), vcol)),
            pl.BlockSpec((WINDOW, KV_WIDTH_A), lambda b, n: (b * nb + n, vcol)),
            pl.BlockSpec((1, LANES), lambda b, n: (0, 0)),
            pl.BlockSpec((1, LANES), lambda b, n: (0, 0)),
            pl.BlockSpec((None, N_HEADS_A, WINDOW, 2 * WINDOW), lambda b, n: (jnp.minimum(n, 1), 0, 0, 0)),
        ],
        out_specs=[
            pl.BlockSpec((WINDOW, WIDTH_A), own),
            pl.BlockSpec((WINDOW, KV_WIDTH_A), own),
        ],
        out_shape=[
            jax.ShapeDtypeStruct((r, WIDTH_A), BF16),
            jax.ShapeDtypeStruct((r, KV_WIDTH_A), F32),
        ],
        compiler_params=_cparams(("parallel", "arbitrary")),
        name="attn_prompt",
    )(sinks_l, proj, proj, proj, proj, proj, lnq2, lnk2, bias_p)


def _attn_sample_kernel(sink_ref, q_ref, kn_in_ref, vn_ref, wk_ref, wv_ref, lnq_ref, lnk_ref, bias_ref,
                        o_ref, kn_ref):
    gmat = _group_ones()
    lnq = lnq_ref[...]
    lnk = lnk_ref[...]
    scale = HEAD_DIM ** -0.5
    rows = q_ref.shape[0]
    g = wk_ref.shape[0]
    kn_new = jnp.concatenate(
        [_head_norm(kn_in_ref[:, c * LANES:(c + 1) * LANES], gmat, lnk) for c in range(KV_WIDTH_A // LANES)], axis=1)
    kn_ref[...] = kn_new
    pad = jnp.zeros((LANES - rows, KV_WIDTH_A), F32)
    k = jnp.concatenate([wk_ref[...].reshape(g * WINDOW, KV_WIDTH_A), kn_new, pad], axis=0)
    v = jnp.concatenate([wv_ref[...].reshape(g * WINDOW, KV_WIDTH_A), vn_ref[...], pad], axis=0)
    q_cols = [(_head_norm(q_ref[:, c * LANES:(c + 1) * LANES], gmat, lnq) * scale).astype(BF16)
              for c in range(WIDTH_A // LANES)]
    outs = _attend(q_cols, k, v, bias_ref, sink_ref)
    for c, o in enumerate(outs):
        o_ref[:, c * LANES:(c + 1) * LANES] = o.astype(o_ref.dtype)


def _attn_sample(proj, win_k, win_v, sinks_l, lnq2, lnk2, bias_s, batch, t_new):
    g = SAMPLE_GROUP_A
    rows = g * t_new
    nk = g * WINDOW + LANES
    kcol = OFF_KA // KV_WIDTH_A
    vcol = OFF_VA // KV_WIDTH_A
    return pl.pallas_call(
        _attn_sample_kernel,
        grid=(batch // g,),
        in_specs=[
            pl.BlockSpec(memory_space=pltpu.SMEM),
            pl.BlockSpec((rows, WIDTH_A), lambda i: (i, 0)),
            pl.BlockSpec((rows, KV_WIDTH_A), lambda i: (i, kcol)),
            pl.BlockSpec((rows, KV_WIDTH_A), lambda i: (i, vcol)),
            pl.BlockSpec((g, WINDOW, KV_WIDTH_A), lambda i: (i, 0, 0)),
            pl.BlockSpec((g, WINDOW, KV_WIDTH_A), lambda i: (i, 0, 0)),
            pl.BlockSpec((1, LANES), lambda i: (0, 0)),
            pl.BlockSpec((1, LANES), lambda i: (0, 0)),
            pl.BlockSpec((N_HEADS_A, rows, nk), lambda i: (0, 0, 0)),
        ],
        out_specs=[
            pl.BlockSpec((rows, WIDTH_A), lambda i: (i, 0)),
            pl.BlockSpec((rows, KV_WIDTH_A), lambda i: (i, 0)),
        ],
        out_shape=[
            jax.ShapeDtypeStruct((batch * t_new, WIDTH_A), BF16),
            jax.ShapeDtypeStruct((batch * t_new, KV_WIDTH_A), F32),
        ],
        compiler_params=_cparams(("parallel",)),
        name="attn_sample",
    )(sinks_l, proj, proj, proj, win_k, win_v, lnq2, lnk2, bias_s)


def _log_decay(gk, w2_ref, bgk_ref):
    x = _dot(gk.astype(BF16), w2_ref[...]) + bgk_ref[...]
    log_sig = jnp.minimum(x, 0.0) - jnp.log(1.0 + jnp.exp(-jnp.abs(x)))
    return log_sig * (1.0 / GATE_TAU)


def _column(row):
    r = lax.broadcasted_iota(jnp.int32, (LANES, LANES), 0)
    c = lax.broadcasted_iota(jnp.int32, (LANES, LANES), 1)
    return jnp.sum(jnp.where(r == c, row, 0.0), axis=1, keepdims=True)


def _gla_out(o, ln_o, rgate):
    return (_rms(o, ln_o) * (rgate * jax.nn.sigmoid(rgate))).astype(BF16)


def _gla_prompt_kernel(q_ref, k_ref, va_ref, vb_ref, ra_ref, rb_ref, gk_ref, w2_ref, bgk_ref, lno_ref,
                       o_ref, s_out_ref, s_scr, b_scr):
    t = pl.program_id(1)
    c_len = GLA_CHUNK_P
    tb = q_ref.shape[0]
    n_chunks = tb // c_len
    half = c_len // 2
    scale = DK_B ** -0.5

    @pl.when(t == 0)
    def _():
        s_scr[...] = jnp.zeros_like(s_scr)

    shift = c_len.bit_length() - 1
    rb = lax.broadcasted_iota(jnp.int32, (tb, tb), 0)
    cb = lax.broadcasted_iota(jnp.int32, (tb, tb), 1)
    same_chunk = lax.shift_right_logical(rb, shift) == lax.shift_right_logical(cb, shift)
    tri_blocks = jnp.where(same_chunk & (rb >= cb), 1.0, 0.0).astype(BF16)
    b_scr[...] = _dot_exact_lhs01(tri_blocks, _log_decay(gk_ref[...], w2_ref, bgk_ref))

    ri = lax.broadcasted_iota(jnp.int32, (c_len, c_len), 0)
    ci = lax.broadcasted_iota(jnp.int32, (c_len, c_len), 1)
    causal = ri >= ci
    ln_o = lno_ref[...]
    v_refs = (va_ref, vb_ref)
    r_refs = (ra_ref, rb_ref)

    def chunk(c, carry):
        rows = pl.ds(pl.multiple_of(c * c_len, c_len), c_len)
        for h in range(N_HEADS_B):
            ks = slice(h * DK_B, (h + 1) * DK_B)
            vs = slice((h % 2) * DV_B, (h % 2 + 1) * DV_B)
            bh = b_scr[rows, ks]
            b_last = bh[c_len - 1:c_len, :]
            b_mid = bh[half - 1:half, :]
            q = q_ref[rows, ks] * scale
            k = k_ref[rows, ks]
            v = v_refs[h // 2][rows, vs].astype(BF16)
            q_inter = (q * jnp.exp(bh)).astype(BF16)
            q_t = (q * jnp.exp(bh - b_mid)).astype(BF16)
            k_t = (k * jnp.exp(b_mid - bh)).astype(BF16)
            a = jnp.where(causal, _dot_nt(q_t, k_t), 0.0).astype(BF16)
            k_state = (k * jnp.exp(b_last - bh)).astype(BF16)
            s_old = s_scr[h]
            o = _dot(a, v) + _dot(q_inter, s_old.astype(BF16))
            s_scr[h] = _column(jnp.exp(b_last)) * s_old + _dot_tn(k_state, v)
            o_ref[rows, h * DV_B:(h + 1) * DV_B] = _gla_out(o, ln_o, r_refs[h // 2][rows, vs])
        return carry

    lax.fori_loop(0, n_chunks, chunk, 0, unroll=4)

    @pl.when(t == pl.num_programs(1) - 1)
    def _():
        s_out_ref[...] = s_scr[...]


def _gla_prompt(proj, gk, w2_l, bgk_l, lno_l, batch, seq):
    tb = GLA_TBLOCK
    nt = seq // tb
    w = 2 * DV_B
    row = lambda b, t: b * nt + t
    spec = lambda col: pl.BlockSpec((tb, w), lambda b, t: (row(b, t), col))
    return pl.pallas_call(
        _gla_prompt_kernel,
        grid=(batch, nt),
        in_specs=[
            spec(OFF_QB // w), spec(OFF_KB // w),
            spec(OFF_VB // w), spec(OFF_VB // w + 1),
            spec(OFF_RB // w), spec(OFF_RB // w + 1),
            pl.BlockSpec((tb, GATE_RANK), lambda b, t: (row(b, t), 0)),
            pl.BlockSpec((GATE_RANK, WIDTH_BK), lambda b, t: (0, 0)),
            pl.BlockSpec((1, WIDTH_BK), lambda b, t: (0, 0)),
            pl.BlockSpec((1, DV_B), lambda b, t: (0, 0)),
        ],
        out_specs=[
            pl.BlockSpec((tb, WIDTH_BV), lambda b, t: (row(b, t), 0)),
            pl.BlockSpec((None, N_HEADS_B, DK_B, DV_B), lambda b, t: (b, 0, 0, 0)),
        ],
        out_shape=[
            jax.ShapeDtypeStruct((batch * seq, WIDTH_BV), BF16),
            jax.ShapeDtypeStruct((batch, N_HEADS_B, DK_B, DV_B), F32),
        ],
        scratch_shapes=[pltpu.VMEM((N_HEADS_B, DK_B, DV_B), F32), pltpu.VMEM((tb, WIDTH_BK), F32)],
        compiler_params=_cparams(("parallel", "arbitrary")),
        name="gla_prompt",
    )(proj, proj, proj, proj, proj, proj, gk, w2_l, bgk_l, lno_l)


def _gla_sample_kernel(t_new, q_ref, k_ref, va_ref, vb_ref, ra_ref, rb_ref, gk_ref, w2_ref, bgk_ref, lno_ref,
                       s_ref, o_ref, s_out_ref):
    rows = q_ref.shape[0]
    n_seq = rows // t_new
    scale = DK_B ** -0.5
    ri = lax.broadcasted_iota(jnp.int32, (rows, rows), 0)
    ci = lax.broadcasted_iota(jnp.int32, (rows, rows), 1)
    same = None
    for sq in range(n_seq):
        lo, hi = sq * t_new, (sq + 1) * t_new
        blk = (ri >= lo) & (ri < hi) & (ci >= lo) & (ci < hi)
        same = blk if same is None else (same | blk)
    causal = same & (ri >= ci)
    tri = jnp.where(causal, 1.0, 0.0).astype(BF16)
    rcol = lax.broadcasted_iota(jnp.int32, (rows, 1), 0)
    ln_o = lno_ref[...]
    v_refs = (va_ref, vb_ref)
    r_refs = (ra_ref, rb_ref)

    g = _log_decay(gk_ref[...], w2_ref, bgk_ref)
    b = _dot_exact_lhs01(tri, g)
    for h in range(N_HEADS_B):
        ks = slice(h * DK_B, (h + 1) * DK_B)
        vs = slice((h % 2) * DV_B, (h % 2 + 1) * DV_B)
        bh = b[:, ks]
        gh = g[:, ks]
        q = q_ref[:, ks] * scale
        k = k_ref[:, ks]
        v = v_refs[h // 2][:, vs].astype(BF16)
        q_inter = q * jnp.exp(bh)
        k_t = (k * jnp.exp(-bh)).astype(BF16)
        a = jnp.where(causal, _dot_nt(q_inter.astype(BF16), k_t), 0.0).astype(BF16)
        o = _dot(a, v)
        for sq in range(n_seq):
            mine = (rcol >= sq * t_new) & (rcol < (sq + 1) * t_new)
            b_last = jnp.sum(jnp.where(mine, gh, 0.0), axis=0, keepdims=True)
            k_state = jnp.where(mine, k * jnp.exp(b_last - bh), 0.0).astype(BF16)
            q_mine = jnp.where(mine, q_inter, 0.0).astype(BF16)
            s_old = s_ref[sq, h]
            o = o + _dot(q_mine, s_old.astype(BF16))
            s_out_ref[sq, h] = _column(jnp.exp(b_last)) * s_old + _dot_tn(k_state, v)
        o_ref[:, h * DV_B:(h + 1) * DV_B] = _gla_out(o, ln_o, r_refs[h // 2][:, vs])


def _gla_sample(proj, gk, w2_l, bgk_l, lno_l, state, l, batch, t_new):
    g = SAMPLE_GROUP_B
    rows = g * t_new
    w = 2 * DV_B
    spec = lambda col: pl.BlockSpec((rows, w), lambda i: (i, col))
    return pl.pallas_call(
        functools.partial(_gla_sample_kernel, t_new),
        grid=(batch // g,),
        in_specs=[
            spec(OFF_QB // w), spec(OFF_KB // w),
            spec(OFF_VB // w), spec(OFF_VB // w + 1),
            spec(OFF_RB // w), spec(OFF_RB // w + 1),
            pl.BlockSpec((rows, GATE_RANK), lambda i: (i, 0)),
            pl.BlockSpec((GATE_RANK, WIDTH_BK), lambda i: (0, 0)),
            pl.BlockSpec((1, WIDTH_BK), lambda i: (0, 0)),
            pl.BlockSpec((1, DV_B), lambda i: (0, 0)),
            pl.BlockSpec((None, g, N_HEADS_B, DK_B, DV_B), lambda i: (l, i, 0, 0, 0)),
        ],
        out_specs=[
            pl.BlockSpec((rows, WIDTH_BV), lambda i: (i, 0)),
            pl.BlockSpec((g, N_HEADS_B, DK_B, DV_B), lambda i: (i, 0, 0, 0)),
        ],
        out_shape=[
            jax.ShapeDtypeStruct((batch * t_new, WIDTH_BV), BF16),
            jax.ShapeDtypeStruct((batch, N_HEADS_B, DK_B, DV_B), F32),
        ],
        compiler_params=_cparams(("parallel",)),
        name="gla_sample",
    )(proj, proj, proj, proj, proj, proj, gk, w2_l, bgk_l, lno_l, state)


def _merge_kernel(oa_ref, ob_ref, sa_ref, sb_ref, x_ref, woa_ref, wob_ref, wout_ref, ln_ref, x_out_ref, h_ref):
    a = _dot(oa_ref[...], woa_ref[...])
    b = _dot(ob_ref[...], wob_ref[...])
    mix = (sa_ref[...].astype(F32) * a + sb_ref[...].astype(F32) * b).astype(BF16)
    x_new = x_ref[...] + _dot(mix, wout_ref[...])
    x_out_ref[...] = x_new
    h_ref[...] = _rms(x_new, ln_ref[...]).astype(BF16)


def _merge(oa, ob, gates, x, w_oa, w_ob, w_out, ln2, l, tm):
    r = x.shape[0]
    resident = dict(pipeline_mode=pl.Buffered(1))
    row = lambda i: (i, 0)
    return pl.pallas_call(
        _merge_kernel,
        grid=(r // tm,),
        in_specs=[
            pl.BlockSpec((tm, WIDTH_A), row),
            pl.BlockSpec((tm, WIDTH_BV), row),
            pl.BlockSpec((tm, D_MODEL), lambda i: (i, 0)),
            pl.BlockSpec((tm, D_MODEL), lambda i: (i, 1)),
            pl.BlockSpec((tm, D_MODEL), row),
            pl.BlockSpec((None, WIDTH_A, D_MODEL), lambda i: (l, 0, 0), **resident),
            pl.BlockSpec((None, WIDTH_BV, D_MODEL), lambda i: (l, 0, 0), **resident),
            pl.BlockSpec((None, D_MODEL, D_MODEL), lambda i: (l, 0, 0), **resident),
            _ln_spec(l, 1),
        ],
        out_specs=[pl.BlockSpec((tm, D_MODEL), row), pl.BlockSpec((tm, D_MODEL), row)],
        out_shape=[jax.ShapeDtypeStruct((r, D_MODEL), F32), jax.ShapeDtypeStruct((r, D_MODEL), BF16)],
        compiler_params=_cparams(("parallel",)),
        name="merge",
    )(oa, ob, gates, gates, x, w_oa, w_ob, w_out, ln2)


def _gelu_tanh(x):
    return x * (0.5 * (1.0 + jnp.tanh(math.sqrt(2.0 / math.pi) * (x + 0.044715 * (x * x * x)))))


def _conv3(u, p1, p2, w_ref, b_ref):
    return w_ref[0:1, :] * p2 + w_ref[1:2, :] * p1 + w_ref[2:3, :] * u + b_ref[...]


def _history_selectors(n_seq, t_new):
    hist = CONV_W - 1
    e1 = np.zeros((n_seq * t_new, n_seq * hist), np.float32)
    e2 = np.zeros((n_seq * t_new, n_seq * hist), np.float32)
    for b in range(n_seq):
        for t in range(min(t_new, hist)):
            if t < 1:
                e1[b * t_new + t, b * hist + hist - 1 + t] = 1.0
            e2[b * t_new + t, b * hist + t] = 1.0
    return jnp.asarray(e1, BF16), jnp.asarray(e2, BF16)


def _ffn_up_kernel(tiles_per_seq, h_ref, hs_ref, pos_ref, e1_ref, e2_ref, wv_ref, wg_ref, cv_ref, cg_ref,
                   bv_ref, bg_ref, sv_ref, sg_ref,
                   act_ref, tv_ref, tg_ref, acts_ref, uv_ref, ug_ref,
                   wvb_scr, wgb_scr, carry_v, carry_g):
    i = pl.program_id(1)
    n_prompt = pl.num_programs(1) - 1

    @pl.when(i == 0)
    def _():
        wvb_scr[...] = wv_ref[...].astype(BF16)
        wgb_scr[...] = wg_ref[...].astype(BF16)

    @pl.when(i < n_prompt)
    def _():
        tm = h_ref.shape[0]
        sub = FFN_SUB_ROWS
        first = (i % tiles_per_seq) == 0
        row8 = lax.broadcasted_iota(jnp.int32, (8, 1), 0)

        def branch(hs, wb_scr, prev, c_ref, b_ref, cols):
            u = _dot(hs, wb_scr[:, cols])
            p1 = pltpu.roll(u, 1, 0)
            p2 = pltpu.roll(u, 2, 0)
            w0, w1, w2, bias = c_ref[0:1, cols], c_ref[1:2, cols], c_ref[2:3, cols], b_ref[:, cols]
            body = w0 * p2 + w1 * p1 + w2 * u + bias
            p1_top = jnp.where(row8 < 1, pltpu.roll(prev, 1, 0), p1[0:8, :])
            p2_top = jnp.where(row8 < 2, pltpu.roll(prev, 2, 0), p2[0:8, :])
            top = w0 * p2_top + w1 * p1_top + w2 * u[0:8, :] + bias
            return jnp.concatenate([top, body[8:, :]], axis=0), u[sub - 8:sub, :]

        col_blocks = [slice(c, c + FFN_SUB_COLS) for c in range(0, act_ref.shape[1], FFN_SUB_COLS)]
        prev_v = [jnp.where(first, 0.0, carry_v[:, cols]) for cols in col_blocks]
        prev_g = [jnp.where(first, 0.0, carry_g[:, cols]) for cols in col_blocks]
        for rb in range(tm // sub):
            rows = slice(rb * sub, (rb + 1) * sub)
            hs = h_ref[rows, :]
            for cb, cols in enumerate(col_blocks):
                val, prev_v[cb] = branch(hs, wvb_scr, prev_v[cb], cv_ref, bv_ref, cols)
                gate, prev_g[cb] = branch(hs, wgb_scr, prev_g[cb], cg_ref, bg_ref, cols)
                act_ref[rows, cols] = (_gelu_tanh(gate) * val).astype(act_ref.dtype)
        for cb, cols in enumerate(col_blocks):
            carry_v[:, cols] = prev_v[cb]
            carry_g[:, cols] = prev_g[cb]
            tv_ref[:, cols] = prev_v[cb]
            tg_ref[:, cols] = prev_g[cb]

    @pl.when(i == n_prompt)
    def _():
        h = hs_ref[...]
        pos = pos_ref[...]
        e1 = e1_ref[...]
        e2 = e2_ref[...]

        def branch(wb_scr, c_ref, b_ref, s_ref, u_ref):
            u = _dot(h, wb_scr[...])
            u_ref[...] = u
            cached = s_ref[...]
            p1 = jnp.where(pos >= 1, pltpu.roll(u, 1, 0), _dot_exact_lhs01(e1, cached))
            p2 = jnp.where(pos >= 2, pltpu.roll(u, 2, 0), _dot_exact_lhs01(e2, cached))
            return _conv3(u, p1, p2, c_ref, b_ref)

        val = branch(wvb_scr, cv_ref, bv_ref, sv_ref, uv_ref)
        gate = branch(wgb_scr, cg_ref, bg_ref, sg_ref, ug_ref)
        acts_ref[...] = (_gelu_tanh(gate) * val).astype(acts_ref.dtype)


def _ffn_up(h, hs, w_up, w_conv, b_conv, conv_state, l, tm, tiles_per_seq, t_new):
    r = h.shape[0]
    rs = hs.shape[0]
    tf = TF_UP
    nt = r // tm
    nj = D_FF // tf
    ns = conv_state.shape[1]
    prompt_tile = lambda i: jnp.minimum(i, nt - 1)
    w_spec = lambda off: pl.BlockSpec((None, D_MODEL, tf), lambda j, i: (l, 0, j + off))
    c_spec = lambda off: pl.BlockSpec((None, CONV_W, tf), lambda j, i: (l, 0, j + off))
    b_spec = lambda off: pl.BlockSpec((None, 1, tf), lambda j, i: (l, 0, j + off))
    s_spec = lambda off: pl.BlockSpec((None, ns, tf), lambda j, i: (l, 0, j + off))
    tail_spec = pl.BlockSpec((None, 8, tf), lambda j, i: (prompt_tile(i), 0, j))
    u_spec = pl.BlockSpec((rs, tf), lambda j, i: (0, j))
    fixed = lambda shape: pl.BlockSpec(shape, lambda j, i: (0, 0))
    pos = jnp.asarray((np.arange(rs) % t_new).astype(np.int32).reshape(rs, 1))
    e1, e2 = _history_selectors(rs // t_new, t_new)
    return pl.pallas_call(
        functools.partial(_ffn_up_kernel, tiles_per_seq),
        grid=(nj, nt + 1),
        in_specs=[
            pl.BlockSpec((tm, D_MODEL), lambda j, i: (prompt_tile(i), 0)),
            fixed((rs, D_MODEL)), fixed((rs, 1)), fixed((rs, ns)), fixed((rs, ns)),
            w_spec(0), w_spec(nj), c_spec(0), c_spec(nj), b_spec(0), b_spec(nj), s_spec(0), s_spec(nj),
        ],
        out_specs=[
            pl.BlockSpec((None, tm, tf), lambda j, i: (j, prompt_tile(i), 0)), tail_spec, tail_spec,
            pl.BlockSpec((None, rs, tf), lambda j, i: (j, 0, 0)), u_spec, u_spec,
        ],
        out_shape=[
            jax.ShapeDtypeStruct((nj, r, tf), BF16),
            jax.ShapeDtypeStruct((nt, 8, D_FF), F32),
            jax.ShapeDtypeStruct((nt, 8, D_FF), F32),
            jax.ShapeDtypeStruct((nj, rs, tf), BF16),
            jax.ShapeDtypeStruct((rs, D_FF), F32),
            jax.ShapeDtypeStruct((rs, D_FF), F32),
        ],
        scratch_shapes=[
            pltpu.VMEM((D_MODEL, tf), BF16),
            pltpu.VMEM((D_MODEL, tf), BF16),
            pltpu.VMEM((8, tf), F32),
            pltpu.VMEM((8, tf), F32),
        ],
        compiler_params=_cparams(("arbitrary", "arbitrary")),
        name="ffn_up",
    )(h, hs, pos, e1, e2, w_up, w_up, w_conv, w_conv, b_conv, b_conv, conv_state, conv_state)


def _down_accumulate(act_ref, w_ref, x_ref, x_out_ref):
    @pl.when(pl.program_id(1) == 0)
    def _():
        x_out_ref[...] = x_ref[...]

    act = act_ref[...]
    step = 512
    for n in range(D_MODEL // step):
        cols = slice(n * step, (n + 1) * step)
        x_out_ref[:, cols] += _dot(act, w_ref[:, cols])


def _down_last_kernel(act_ref, w_ref, x_ref, x_out_ref):
    _down_accumulate(act_ref, w_ref, x_ref, x_out_ref)


def _down_norm_kernel(act_ref, w_ref, x_ref, ln_ref, wgk_ref, x_out_ref, h_ref, gk_ref):
    _down_accumulate(act_ref, w_ref, x_ref, x_out_ref)

    @pl.when(pl.program_id(1) == pl.num_programs(1) - 1)
    def _():
        h = _rms(x_out_ref[...], ln_ref[...]).astype(BF16)
        h_ref[...] = h
        gk_ref[...] = _dot_nt(h, wgk_ref[...].astype(BF16))


def _down(act, w_down, x, ln1, w_in_t, l, tm, with_norm):
    nk, r, tk = act.shape
    row = lambda i, k: (i, 0)
    in_specs = [
        pl.BlockSpec((None, tm, tk), lambda i, k: (k, i, 0)),
        pl.BlockSpec((None, tk, D_MODEL), lambda i, k: (l, k, 0)),
        pl.BlockSpec((tm, D_MODEL), row),
    ]
    out_specs = [pl.BlockSpec((tm, D_MODEL), row)]
    out_shape = [jax.ShapeDtypeStruct((r, D_MODEL), F32)]
    args = [act, w_down, x]
    if with_norm:
        in_specs += [_ln_spec(l + 1, 2), _wgk_spec(l + 1, 2)]
        out_specs += [pl.BlockSpec((tm, D_MODEL), row), pl.BlockSpec((tm, GATE_RANK), row)]
        out_shape += [jax.ShapeDtypeStruct((r, D_MODEL), BF16), jax.ShapeDtypeStruct((r, GATE_RANK), F32)]
        args += [ln1, w_in_t]
    out = pl.pallas_call(
        _down_norm_kernel if with_norm else _down_last_kernel,
        grid=(r // tm, nk),
        in_specs=in_specs,
        out_specs=out_specs,
        out_shape=out_shape,
        compiler_params=_cparams(("parallel", "arbitrary")),
        name="down_norm" if with_norm else "down_last",
    )(*args)
    return out if with_norm else (out[0], None, None)


def kernel(x_prompt, x_sample, cache_win_k, cache_win_v, state_gla, state_conv, rel_bias,
           ln1, w_in, ln_q, ln_k, sinks, w_gk2, b_gk, ln_o, w_oa, w_ob, w_out,
           ln2, w_up, w_conv, b_conv, w_down):
    bp, seq, _ = x_prompt.shape
    bs, t_new, _ = x_sample.shape
    tm_p = TM_PROMPT
    tm_s = bs * t_new
    tiles_per_seq = seq // tm_p

    ln1_3 = ln1.reshape(DEPTH, 1, D_MODEL)
    ln2_3 = ln2.reshape(DEPTH, 1, D_MODEL)
    w_in_t = jnp.swapaxes(w_in, 1, 2)
    w_oa_b, w_ob_b, w_out_b, w_down_b = (w.astype(BF16) for w in (w_oa, w_ob, w_out, w_down))
    w2_b = w_gk2.astype(BF16)
    b_conv3 = b_conv.reshape(DEPTH, 1, 2 * D_FF)
    lnq2 = jnp.tile(ln_q, (1, 2)).reshape(DEPTH, 1, LANES)
    lnk2 = jnp.tile(ln_k, (1, 2)).reshape(DEPTH, 1, LANES)

    bias_p = jnp.stack([_bias_table(rel_bias, jnp.asarray(_dmap_prompt(first))) for first in (True, False)])
    bias_s = _bias_table(rel_bias, jnp.asarray(_dmap_sample(t_new, SAMPLE_GROUP_A)))

    xp = x_prompt.reshape(bp * seq, D_MODEL)
    xs = x_sample.reshape(bs * t_new, D_MODEL)
    win_k = cache_win_k.reshape(DEPTH, bs, WINDOW, KV_WIDTH_A)
    win_v = cache_win_v.reshape(DEPTH, bs, WINDOW, KV_WIDTH_A)
    conv_state = state_conv.reshape(DEPTH, bs * (CONV_W - 1), 2 * D_FF)

    hp, gkp = _norm(xp, ln1_3, w_in_t, 0, 512)
    hs, gks = _norm(xs, ln1_3, w_in_t, 0, tm_s)

    kp, vp, gp, cp = [], [], [], []
    ksm, vsm, gsm, csm = [], [], [], []
    for l in range(DEPTH):
        lq, lk, sk = lnq2[l], lnk2[l], sinks[l]
        w2_l, bgk_l, lno_l = w2_b[l], b_gk[l].reshape(1, WIDTH_BK), ln_o[l].reshape(1, DV_B)
        more = l + 1 < DEPTH

        proj, proj_s = _proj(hp, hs, w_in_t, l, tm_p, TN_MAIN, 0, MAIN_COLS, False, F32, "proj_main")
        gates, gates_s = _proj(hp, hs, w_in_t, l, tm_p, TN_GATES, OFF_GA, GATE_COLS, True, BF16, "proj_gates")

        oa, kn = _attn_prompt(proj, sk, lq, lk, bias_p, bp, seq)
        ob, s_new = _gla_prompt(proj, gkp, w2_l, bgk_l, lno_l, bp, seq)
        xp, h2 = _merge(oa, ob, gates, xp, w_oa_b, w_ob_b, w_out_b, ln2_3, l, TM_MERGE)
        kp.append(kn.reshape(bp, seq, N_KV_A, HEAD_DIM)[:, -WINDOW:])
        vp.append(proj[:, OFF_VA:OFF_VA + KV_WIDTH_A].reshape(bp, seq, N_KV_A, HEAD_DIM)[:, -WINDOW:])
        gp.append(s_new)

        oa, kn = _attn_sample(proj_s, win_k[l], win_v[l], sk, lq, lk, bias_s, bs, t_new)
        ob, s_new = _gla_sample(proj_s, gks, w2_l, bgk_l, lno_l, state_gla, l, bs, t_new)
        xs, h2_s = _merge(oa, ob, gates_s, xs, w_oa_b, w_ob_b, w_out_b, ln2_3, l, tm_s)

        act, tail_v, tail_g, act_s, u_v, u_g = _ffn_up(h2, h2_s, w_up, w_conv, b_conv3, conv_state, l, tm_p,
                                                       tiles_per_seq, t_new)
        xp, hp, gkp = _down(act, w_down_b, xp, ln1_3, w_in_t, l, tm_p, more)
        xs, hs, gks = _down(act_s, w_down_b, xs, ln1_3, w_in_t, l, tm_s, more)
        tails = jnp.concatenate([tail_v, tail_g], axis=-1).reshape(bp, tiles_per_seq, 8, 2 * D_FF)
        cp.append(tails[:, -1, 8 - (CONV_W - 1):])
        k_new = kn.reshape(bs, t_new, N_KV_A, HEAD_DIM)
        v_new = proj_s[:, OFF_VA:OFF_VA + KV_WIDTH_A].reshape(bs, t_new, N_KV_A, HEAD_DIM)
        ksm.append(jnp.concatenate([cache_win_k[l][:, t_new:], k_new], axis=1))
        vsm.append(jnp.concatenate([cache_win_v[l][:, t_new:], v_new], axis=1))
        gsm.append(s_new)
        u = jnp.concatenate([u_v, u_g], axis=-1).reshape(bs, t_new, 2 * D_FF)
        csm.append(u[:, t_new - (CONV_W - 1):])

    return (xp.reshape(bp, seq, D_MODEL), xs.reshape(bs, t_new, D_MODEL),
            jnp.stack(kp), jnp.stack(vp), jnp.stack(gp), jnp.stack(cp),
            jnp.stack(ksm), jnp.stack(vsm), jnp.stack(gsm), jnp.stack(csm))
```

```python
import functools
import math

import numpy as np
import jax
import jax.numpy as jnp
from jax import lax
from jax.experimental import pallas as pl
from jax.experimental.pallas import tpu as pltpu

F32 = jnp.float32
BF16 = jnp.bfloat16

D_MODEL = 2048
DEPTH = 4
HEAD_DIM = 64
N_HEADS_A = 16
N_KV_A = 4
WIDTH_A = N_HEADS_A * HEAD_DIM
KV_WIDTH_A = N_KV_A * HEAD_DIM
WINDOW = 128
N_BUCKETS = 32
MAX_DISTANCE = 128
N_HEADS_B = 4
DK_B = 128
DV_B = 256
WIDTH_BK = N_HEADS_B * DK_B
WIDTH_BV = N_HEADS_B * DV_B
GATE_RANK = 16
GATE_TAU = 16.0
D_FF = 5632
CONV_W = 3
EPS = 1e-6

OFF_QA = 0
OFF_KA = OFF_QA + WIDTH_A
OFF_VA = OFF_KA + KV_WIDTH_A
OFF_QB = OFF_VA + KV_WIDTH_A
OFF_KB = OFF_QB + WIDTH_BK
OFF_VB = OFF_KB + WIDTH_BK
OFF_RB = OFF_VB + WIDTH_BV
OFF_GK = OFF_RB + WIDTH_BV
OFF_GA = OFF_GK + GATE_RANK
MAIN_COLS = OFF_GK
GATE_COLS = 2 * D_MODEL

LANES = 128
NEG_BIG = -1e30
VMEM_LIMIT = 56 * 1024 * 1024

TM_PROMPT = 1024
TM_MERGE = 256
TN_MAIN = 1536
TN_GATES = 1024
TF_UP = 512
GLA_CHUNK_P = 128
GLA_TBLOCK = 512
SAMPLE_GROUP_A = 8
SAMPLE_GROUP_B = 4


def _cparams(sem):
    return pltpu.CompilerParams(dimension_semantics=sem, vmem_limit_bytes=VMEM_LIMIT)


def _dot(a, b):
    return jnp.dot(a, b, preferred_element_type=F32)


def _dot_nt(a, b):
    return lax.dot_general(a, b, (((1,), (1,)), ((), ())), preferred_element_type=F32)


def _dot_tn(a, b):
    return lax.dot_general(a, b, (((0,), (0,)), ((), ())), preferred_element_type=F32)


def _split3(x):
    hi = x.astype(BF16)
    r = x - hi.astype(F32)
    mid = r.astype(BF16)
    lo = (r - mid.astype(F32)).astype(BF16)
    return hi, mid, lo


def _dot_exact_lhs01(m01, x):
    hi, mid, lo = _split3(x)
    return _dot(m01, hi) + _dot(m01, mid) + _dot(m01, lo)


def _rms(x, g):
    ms = jnp.mean(x * x, axis=-1, keepdims=True)
    return x * lax.rsqrt(ms + EPS) * g


def _norm_kernel(x_ref, ln_ref, wgk_ref, h_ref, gk_ref):
    h = _rms(x_ref[...], ln_ref[...]).astype(BF16)
    h_ref[...] = h
    gk_ref[...] = _dot_nt(h, wgk_ref[...].astype(BF16))


def _ln_spec(l, nargs):
    if nargs == 1:
        return pl.BlockSpec((None, 1, D_MODEL), lambda i: (l, 0, 0))
    return pl.BlockSpec((None, 1, D_MODEL), lambda i, j: (l, 0, 0))


def _wgk_spec(l, nargs):
    blk = OFF_GK // GATE_RANK
    if nargs == 1:
        return pl.BlockSpec((None, GATE_RANK, D_MODEL), lambda i: (l, blk, 0))
    return pl.BlockSpec((None, GATE_RANK, D_MODEL), lambda i, j: (l, blk, 0))


def _norm(x, ln1, w_in_t, l, tm):
    r = x.shape[0]
    return pl.pallas_call(
        _norm_kernel,
        grid=(r // tm,),
        in_specs=[pl.BlockSpec((tm, D_MODEL), lambda i: (i, 0)), _ln_spec(l, 1), _wgk_spec(l, 1)],
        out_specs=[pl.BlockSpec((tm, D_MODEL), lambda i: (i, 0)), pl.BlockSpec((tm, GATE_RANK), lambda i: (i, 0))],
        out_shape=[jax.ShapeDtypeStruct((r, D_MODEL), BF16), jax.ShapeDtypeStruct((r, GATE_RANK), F32)],
        compiler_params=_cparams(("parallel",)),
        name="norm",
    )(x, ln1, w_in_t)


def _proj_kernel(sigmoid, h_ref, w_ref, o_ref, wb_scr):
    @pl.when(pl.program_id(1) == 0)
    def _():
        wb_scr[...] = w_ref[...].astype(BF16)

    y = _dot_nt(h_ref[...], wb_scr[...])
    if sigmoid:
        y = jax.nn.sigmoid(y)
    o_ref[...] = y.astype(o_ref.dtype)


def _proj(h, w_in_t, l, tm, tn, row_off, n_cols, sigmoid, out_dtype, name):
    r = h.shape[0]
    if row_off == 0:
        w_spec = pl.BlockSpec((None, tn, D_MODEL), lambda j, i: (l, j, 0))
    else:
        w_spec = pl.BlockSpec((pl.Squeezed(), pl.Element(tn), pl.Element(D_MODEL)),
                              lambda j, i: (l, pl.multiple_of(row_off + j * tn, GATE_RANK), 0))
    return pl.pallas_call(
        functools.partial(_proj_kernel, sigmoid),
        grid=(n_cols // tn, r // tm),
        in_specs=[pl.BlockSpec((tm, D_MODEL), lambda j, i: (i, 0)), w_spec],
        out_specs=pl.BlockSpec((tm, tn), lambda j, i: (i, j)),
        out_shape=jax.ShapeDtypeStruct((r, n_cols), out_dtype),
        scratch_shapes=[pltpu.VMEM((tn, D_MODEL), BF16)],
        compiler_params=_cparams(("arbitrary", "arbitrary")),
        name=name,
    )(h, w_in_t)


def _bucket_ranges():
    d = np.arange(WINDOW)
    max_exact = N_BUCKETS // 2
    df = np.maximum(d, 1).astype(np.float32)
    large = max_exact + (np.log(df / np.float32(max_exact)) / np.float32(math.log(MAX_DISTANCE / max_exact))
                         * np.float32(N_BUCKETS - max_exact)).astype(np.int32)
    large = np.minimum(large, N_BUCKETS - 1)
    bucket = np.where(d < max_exact, d, large)
    ranges = []
    for b in range(N_BUCKETS):
        idx = np.nonzero(bucket == b)[0]
        if idx.size:
            assert idx[-1] - idx[0] + 1 == idx.size
            ranges.append((b, int(idx[0]), int(idx[-1])))
    return ranges


def _bias_kernel(rel_ref, d_ref, o_ref):
    h = pl.program_id(0)
    d = d_ref[...]
    val = jnp.full(d.shape, NEG_BIG, F32)
    for b, lo, hi in _bucket_ranges():
        val = jnp.where((d >= lo) & (d <= hi), rel_ref[b, h], val)
    o_ref[...] = val


def _bias_table(rel_bias, dmap):
    mq, nk = dmap.shape
    return pl.pallas_call(
        _bias_kernel,
        grid=(N_HEADS_A,),
        in_specs=[
            pl.BlockSpec(memory_space=pltpu.SMEM),
            pl.BlockSpec((mq, nk), lambda h: (0, 0)),
        ],
        out_specs=pl.BlockSpec((None, mq, nk), lambda h: (h, 0, 0)),
        out_shape=jax.ShapeDtypeStruct((N_HEADS_A, mq, nk), F32),
        compiler_params=_cparams(("parallel",)),
        name="bias_table",
    )(rel_bias, dmap)


def _dmap_prompt(first_block):
    i = np.arange(WINDOW)[:, None]
    j = np.arange(2 * WINDOW)[None, :]
    d = i + WINDOW - j
    ok = (d >= 0) & (d < WINDOW)
    if first_block:
        ok = ok & (j >= WINDOW)
    return np.where(ok, d, -1).astype(np.int32)


def _dmap_sample(t_new, group):
    nk = group * WINDOW + LANES
    rows = np.arange(group * t_new)
    rb, rt = rows // t_new, rows % t_new
    d = np.full((group * t_new, nk), -1, np.int64)
    cols = np.arange(group * WINDOW)
    cb, cs = cols // WINDOW, cols % WINDOW
    dw = rt[:, None] + WINDOW - cs[None, :]
    ok = (rb[:, None] == cb[None, :]) & (dw >= 0) & (dw < WINDOW)
    d[:, :group * WINDOW] = np.where(ok, dw, -1)
    ncols = np.arange(group * t_new)
    nb, nu = ncols // t_new, ncols % t_new
    dn = rt[:, None] - nu[None, :]
    okn = (rb[:, None] == nb[None, :]) & (dn >= 0)
    d[:, group * WINDOW:group * WINDOW + group * t_new] = np.where(okn, dn, -1)
    return d.astype(np.int32)


def _group_ones():
    r = lax.broadcasted_iota(jnp.int32, (LANES, LANES), 0)
    c = lax.broadcasted_iota(jnp.int32, (LANES, LANES), 1)
    low_c = jnp.where(c < HEAD_DIM, 1.0, 0.0)
    return jnp.where(r < HEAD_DIM, low_c, 1.0 - low_c).astype(BF16)


def _head_norm(x, gmat, ln2):
    sq = x * x
    hi = sq.astype(BF16)
    lo = (sq - hi.astype(F32)).astype(BF16)
    ms = (_dot(hi, gmat) + _dot(lo, gmat)) * (1.0 / HEAD_DIM)
    return x * lax.rsqrt(ms + EPS) * ln2


def _split_heads(x):
    lane = lax.broadcasted_iota(jnp.int32, (x.shape[0], LANES), 1)
    low = lane < HEAD_DIM
    lo_parts, hi_parts = [], []
    for c in range(KV_WIDTH_A // LANES):
        xc = x[:, c * LANES:(c + 1) * LANES]
        xr = pltpu.roll(xc, HEAD_DIM, 1)
        lo_parts += [jnp.where(low, xc, 0.0).astype(BF16), jnp.where(low, xr, 0.0).astype(BF16)]
        hi_parts += [jnp.where(low, 0.0, xr).astype(BF16), jnp.where(low, 0.0, xc).astype(BF16)]
    return lo_parts, hi_parts


def _attend(q_cols, k, v, bias_ref, sink_ref):
    k_lo, k_hi = _split_heads(k)
    v_lo, v_hi = _split_heads(v)
    group = N_HEADS_A // N_KV_A
    outs = []
    for pc in range(N_HEADS_A // 2):
        acc = None
        for half in range(2):
            h = 2 * pc + half
            kh = h // group
            kk = (k_lo, k_hi)[half][kh]
            vv = (v_lo, v_hi)[half][kh]
            s = _dot_nt(q_cols[pc], kk) + bias_ref[h]
            sink = sink_ref[h]
            m = jnp.maximum(jnp.max(s, axis=-1, keepdims=True), sink)
            e = jnp.exp(s - m)
            den = jnp.sum(e, axis=-1, keepdims=True) + jnp.exp(sink - m)
            o = _dot(e.astype(BF16), vv) * (1.0 / den)
            acc = o if acc is None else acc + o
        outs.append(acc)
    return outs


def _attn_prompt_kernel(sink_ref, q_ref, kp_ref, ko_ref, vp_ref, vo_ref, lnq_ref, lnk_ref, bias_ref,
                        o_ref, kn_ref):
    gmat = _group_ones()
    lnq = lnq_ref[...]
    lnk = lnk_ref[...]
    scale = HEAD_DIM ** -0.5

    def norm_k(ref):
        return jnp.concatenate(
            [_head_norm(ref[:, c * LANES:(c + 1) * LANES], gmat, lnk) for c in range(KV_WIDTH_A // LANES)], axis=1)

    kn_own = norm_k(ko_ref)
    kn_ref[...] = kn_own
    k = jnp.concatenate([norm_k(kp_ref), kn_own], axis=0)
    v = jnp.concatenate([vp_ref[...], vo_ref[...]], axis=0)
    q_cols = [(_head_norm(q_ref[:, c * LANES:(c + 1) * LANES], gmat, lnq) * scale).astype(BF16)
              for c in range(WIDTH_A // LANES)]
    outs = _attend(q_cols, k, v, bias_ref, sink_ref)
    for c, o in enumerate(outs):
        o_ref[:, c * LANES:(c + 1) * LANES] = o.astype(o_ref.dtype)


def _attn_prompt(proj, sinks_l, lnq2, lnk2, bias_p, batch, seq):
    nb = seq // WINDOW
    r = batch * seq
    kcol = OFF_KA // KV_WIDTH_A
    vcol = OFF_VA // KV_WIDTH_A
    own = lambda b, n: (b * nb + n, 0)
    return pl.pallas_call(
        _attn_prompt_kernel,
        grid=(batch, nb),
        in_specs=[
            pl.BlockSpec(memory_space=pltpu.SMEM),
            pl.BlockSpec((WINDOW, WIDTH_A), own),
            pl.BlockSpec((WINDOW, KV_WIDTH_A), lambda b, n: (b * nb + jnp.maximum(n - 1, 0), kcol)),
            pl.BlockSpec((WINDOW, KV_WIDTH_A), lambda b, n: (b * nb + n, kcol)),
            pl.BlockSpec((WINDOW, KV_WIDTH_A), lambda b, n: (b * nb + jnp.maximum(n - 1, 0), vcol)),
            pl.BlockSpec((WINDOW, KV_WIDTH_A), lambda b, n: (b * nb + n, vcol)),
            pl.BlockSpec((1, LANES), lambda b, n: (0, 0)),
            pl.BlockSpec((1, LANES), lambda b, n: (0, 0)),
            pl.BlockSpec((None, N_HEADS_A, WINDOW, 2 * WINDOW), lambda b, n: (jnp.minimum(n, 1), 0, 0, 0)),
        ],
        out_specs=[
            pl.BlockSpec((WINDOW, WIDTH_A), own),
            pl.BlockSpec((WINDOW, KV_WIDTH_A), own),
        ],
        out_shape=[
            jax.ShapeDtypeStruct((r, WIDTH_A), BF16),
            jax.ShapeDtypeStruct((r, KV_WIDTH_A), F32),
        ],
        compiler_params=_cparams(("parallel", "arbitrary")),
        name="attn_prompt",
    )(sinks_l, proj, proj, proj, proj, proj, lnq2, lnk2, bias_p)


def _attn_sample_kernel(sink_ref, q_ref, kn_in_ref, vn_ref, wk_ref, wv_ref, lnq_ref, lnk_ref, bias_ref,
                        o_ref, kn_ref):
    gmat = _group_ones()
    lnq = lnq_ref[...]
    lnk = lnk_ref[...]
    scale = HEAD_DIM ** -0.5
    rows = q_ref.shape[0]
    g = wk_ref.shape[0]
    kn_new = jnp.concatenate(
        [_head_norm(kn_in_ref[:, c * LANES:(c + 1) * LANES], gmat, lnk) for c in range(KV_WIDTH_A // LANES)], axis=1)
    kn_ref[...] = kn_new
    pad = jnp.zeros((LANES - rows, KV_WIDTH_A), F32)
    k = jnp.concatenate([wk_ref[...].reshape(g * WINDOW, KV_WIDTH_A), kn_new, pad], axis=0)
    v = jnp.concatenate([wv_ref[...].reshape(g * WINDOW, KV_WIDTH_A), vn_ref[...], pad], axis=0)
    q_cols = [(_head_norm(q_ref[:, c * LANES:(c + 1) * LANES], gmat, lnq) * scale).astype(BF16)
              for c in range(WIDTH_A // LANES)]
    outs = _attend(q_cols, k, v, bias_ref, sink_ref)
    for c, o in enumerate(outs):
        o_ref[:, c * LANES:(c + 1) * LANES] = o.astype(o_ref.dtype)


def _attn_sample(proj, win_k, win_v, sinks_l, lnq2, lnk2, bias_s, batch, t_new):
    g = SAMPLE_GROUP_A
    rows = g * t_new
    nk = g * WINDOW + LANES
    kcol = OFF_KA // KV_WIDTH_A
    vcol = OFF_VA // KV_WIDTH_A
    return pl.pallas_call(
        _attn_sample_kernel,
        grid=(batch // g,),
        in_specs=[
            pl.BlockSpec(memory_space=pltpu.SMEM),
            pl.BlockSpec((rows, WIDTH_A), lambda i: (i, 0)),
            pl.BlockSpec((rows, KV_WIDTH_A), lambda i: (i, kcol)),
            pl.BlockSpec((rows, KV_WIDTH_A), lambda i: (i, vcol)),
            pl.BlockSpec((g, WINDOW, KV_WIDTH_A), lambda i: (i, 0, 0)),
            pl.BlockSpec((g, WINDOW, KV_WIDTH_A), lambda i: (i, 0, 0)),
            pl.BlockSpec((1, LANES), lambda i: (0, 0)),
            pl.BlockSpec((1, LANES), lambda i: (0, 0)),
            pl.BlockSpec((N_HEADS_A, rows, nk), lambda i: (0, 0, 0)),
        ],
        out_specs=[
            pl.BlockSpec((rows, WIDTH_A), lambda i: (i, 0)),
            pl.BlockSpec((rows, KV_WIDTH_A), lambda i: (i, 0)),
        ],
        out_shape=[
            jax.ShapeDtypeStruct((batch * t_new, WIDTH_A), BF16),
            jax.ShapeDtypeStruct((batch * t_new, KV_WIDTH_A), F32),
        ],
        compiler_params=_cparams(("parallel",)),
        name="attn_sample",
    )(sinks_l, proj, proj, proj, win_k, win_v, lnq2, lnk2, bias_s)


def _log_decay(gk, w2_ref, bgk_ref):
    x = _dot(gk.astype(BF16), w2_ref[...]) + bgk_ref[...]
    log_sig = jnp.minimum(x, 0.0) - jnp.log(1.0 + jnp.exp(-jnp.abs(x)))
    return log_sig * (1.0 / GATE_TAU)


def _column(row):
    r = lax.broadcasted_iota(jnp.int32, (LANES, LANES), 0)
    c = lax.broadcasted_iota(jnp.int32, (LANES, LANES), 1)
    return jnp.sum(jnp.where(r == c, row, 0.0), axis=1, keepdims=True)


def _gla_out(o, ln_o, rgate):
    return (_rms(o, ln_o) * (rgate * jax.nn.sigmoid(rgate))).astype(BF16)


def _gla_prompt_kernel(q_ref, k_ref, va_ref, vb_ref, ra_ref, rb_ref, gk_ref, w2_ref, bgk_ref, lno_ref,
                       o_ref, s_out_ref, s_scr, b_scr):
    t = pl.program_id(1)
    c_len = GLA_CHUNK_P
    tb = q_ref.shape[0]
    n_chunks = tb // c_len
    half = c_len // 2
    scale = DK_B ** -0.5

    @pl.when(t == 0)
    def _():
        s_scr[...] = jnp.zeros_like(s_scr)

    shift = c_len.bit_length() - 1
    rb = lax.broadcasted_iota(jnp.int32, (tb, tb), 0)
    cb = lax.broadcasted_iota(jnp.int32, (tb, tb), 1)
    same_chunk = lax.shift_right_logical(rb, shift) == lax.shift_right_logical(cb, shift)
    tri_blocks = jnp.where(same_chunk & (rb >= cb), 1.0, 0.0).astype(BF16)
    b_scr[...] = _dot_exact_lhs01(tri_blocks, _log_decay(gk_ref[...], w2_ref, bgk_ref))

    ri = lax.broadcasted_iota(jnp.int32, (c_len, c_len), 0)
    ci = lax.broadcasted_iota(jnp.int32, (c_len, c_len), 1)
    quarter = half // 2
    same_half = jnp.where(ci < half, jnp.where(ri < half, 1, 0), jnp.where(ri < half, 0, 1))
    own_causal = (ri >= ci) & (same_half == 1)
    upper_row = lax.broadcasted_iota(jnp.int32, (c_len, 1), 0) >= half
    ln_o = lno_ref[...]
    v_refs = (va_ref, vb_ref)
    r_refs = (ra_ref, rb_ref)

    def chunk(c, carry):
        rows = pl.ds(pl.multiple_of(c * c_len, c_len), c_len)
        for h in range(N_HEADS_B):
            ks = slice(h * DK_B, (h + 1) * DK_B)
            vs = slice((h % 2) * DV_B, (h % 2 + 1) * DV_B)
            bh = b_scr[rows, ks]
            b_last = bh[c_len - 1:c_len, :]
            b_half = bh[half - 1:half, :]
            q = q_ref[rows, ks] * scale
            k = k_ref[rows, ks]
            v = v_refs[h // 2][rows, vs].astype(BF16)
            q_inter = (q * jnp.exp(bh)).astype(BF16)
            r_own = jnp.where(upper_row, bh[half + quarter - 1:half + quarter, :], bh[quarter - 1:quarter, :])
            a_own = _dot_nt((q * jnp.exp(bh - r_own)).astype(BF16), (k * jnp.exp(r_own - bh)).astype(BF16))
            q_x = jnp.where(upper_row, q * jnp.exp(bh - b_half), 0.0).astype(BF16)
            k_x = jnp.where(upper_row, 0.0, k * jnp.exp(b_half - bh)).astype(BF16)
            a = (jnp.where(own_causal, a_own, 0.0) + _dot_nt(q_x, k_x)).astype(BF16)
            k_state = (k * jnp.exp(b_last - bh)).astype(BF16)
            s_old = s_scr[h]
            o = _dot(a, v) + _dot(q_inter, s_old.astype(BF16))
            s_scr[h] = _column(jnp.exp(b_last)) * s_old + _dot_tn(k_state, v)
            o_ref[rows, h * DV_B:(h + 1) * DV_B] = _gla_out(o, ln_o, r_refs[h // 2][rows, vs])
        return carry

    lax.fori_loop(0, n_chunks, chunk, 0, unroll=4)

    @pl.when(t == pl.num_programs(1) - 1)
    def _():
        s_out_ref[...] = s_scr[...]


def _gla_prompt(proj, gk, w2_l, bgk_l, lno_l, batch, seq):
    tb = GLA_TBLOCK
    nt = seq // tb
    w = 2 * DV_B
    row = lambda b, t: b * nt + t
    spec = lambda col: pl.BlockSpec((tb, w), lambda b, t: (row(b, t), col))
    return pl.pallas_call(
        _gla_prompt_kernel,
        grid=(batch, nt),
        in_specs=[
            spec(OFF_QB // w), spec(OFF_KB // w),
            spec(OFF_VB // w), spec(OFF_VB // w + 1),
            spec(OFF_RB // w), spec(OFF_RB // w + 1),
            pl.BlockSpec((tb, GATE_RANK), lambda b, t: (row(b, t), 0)),
            pl.BlockSpec((GATE_RANK, WIDTH_BK), lambda b, t: (0, 0)),
            pl.BlockSpec((1, WIDTH_BK), lambda b, t: (0, 0)),
            pl.BlockSpec((1, DV_B), lambda b, t: (0, 0)),
        ],
        out_specs=[
            pl.BlockSpec((tb, WIDTH_BV), lambda b, t: (row(b, t), 0)),
            pl.BlockSpec((None, N_HEADS_B, DK_B, DV_B), lambda b, t: (b, 0, 0, 0)),
        ],
        out_shape=[
            jax.ShapeDtypeStruct((batch * seq, WIDTH_BV), BF16),
            jax.ShapeDtypeStruct((batch, N_HEADS_B, DK_B, DV_B), F32),
        ],
        scratch_shapes=[pltpu.VMEM((N_HEADS_B, DK_B, DV_B), F32), pltpu.VMEM((tb, WIDTH_BK), F32)],
        compiler_params=_cparams(("parallel", "arbitrary")),
        name="gla_prompt",
    )(proj, proj, proj, proj, proj, proj, gk, w2_l, bgk_l, lno_l)


def _gla_sample_kernel(t_new, q_ref, k_ref, va_ref, vb_ref, ra_ref, rb_ref, gk_ref, w2_ref, bgk_ref, lno_ref,
                       s_ref, o_ref, s_out_ref):
    rows = q_ref.shape[0]
    n_seq = rows // t_new
    scale = DK_B ** -0.5
    ri = lax.broadcasted_iota(jnp.int32, (rows, rows), 0)
    ci = lax.broadcasted_iota(jnp.int32, (rows, rows), 1)
    same = None
    for sq in range(n_seq):
        lo, hi = sq * t_new, (sq + 1) * t_new
        blk = (ri >= lo) & (ri < hi) & (ci >= lo) & (ci < hi)
        same = blk if same is None else (same | blk)
    causal = same & (ri >= ci)
    tri = jnp.where(causal, 1.0, 0.0).astype(BF16)
    rcol = lax.broadcasted_iota(jnp.int32, (rows, 1), 0)
    ln_o = lno_ref[...]
    v_refs = (va_ref, vb_ref)
    r_refs = (ra_ref, rb_ref)

    g = _log_decay(gk_ref[...], w2_ref, bgk_ref)
    b = _dot_exact_lhs01(tri, g)
    for h in range(N_HEADS_B):
        ks = slice(h * DK_B, (h + 1) * DK_B)
        vs = slice((h % 2) * DV_B, (h % 2 + 1) * DV_B)
        bh = b[:, ks]
        gh = g[:, ks]
        q = q_ref[:, ks] * scale
        k = k_ref[:, ks]
        v = v_refs[h // 2][:, vs].astype(BF16)
        q_inter = q * jnp.exp(bh)
        k_t = (k * jnp.exp(-bh)).astype(BF16)
        a = jnp.where(causal, _dot_nt(q_inter.astype(BF16), k_t), 0.0).astype(BF16)
        o = _dot(a, v)
        for sq in range(n_seq):
            mine = (rcol >= sq * t_new) & (rcol < (sq + 1) * t_new)
            b_last = jnp.sum(jnp.where(mine, gh, 0.0), axis=0, keepdims=True)
            k_state = jnp.where(mine, k * jnp.exp(b_last - bh), 0.0).astype(BF16)
            q_mine = jnp.where(mine, q_inter, 0.0).astype(BF16)
            s_old = s_ref[sq, h]
            o = o + _dot(q_mine, s_old.astype(BF16))
            s_out_ref[sq, h] = _column(jnp.exp(b_last)) * s_old + _dot_tn(k_state, v)
        o_ref[:, h * DV_B:(h + 1) * DV_B] = _gla_out(o, ln_o, r_refs[h // 2][:, vs])


def _gla_sample(proj, gk, w2_l, bgk_l, lno_l, state, l, batch, t_new):
    g = SAMPLE_GROUP_B
    rows = g * t_new
    w = 2 * DV_B
    spec = lambda col: pl.BlockSpec((rows, w), lambda i: (i, col))
    return pl.pallas_call(
        functools.partial(_gla_sample_kernel, t_new),
        grid=(batch // g,),
        in_specs=[
            spec(OFF_QB // w), spec(OFF_KB // w),
            spec(OFF_VB // w), spec(OFF_VB // w + 1),
            spec(OFF_RB // w), spec(OFF_RB // w + 1),
            pl.BlockSpec((rows, GATE_RANK), lambda i: (i, 0)),
            pl.BlockSpec((GATE_RANK, WIDTH_BK), lambda i: (0, 0)),
            pl.BlockSpec((1, WIDTH_BK), lambda i: (0, 0)),
            pl.BlockSpec((1, DV_B), lambda i: (0, 0)),
            pl.BlockSpec((None, g, N_HEADS_B, DK_B, DV_B), lambda i: (l, i, 0, 0, 0)),
        ],
        out_specs=[
            pl.BlockSpec((rows, WIDTH_BV), lambda i: (i, 0)),
            pl.BlockSpec((g, N_HEADS_B, DK_B, DV_B), lambda i: (i, 0, 0, 0)),
        ],
        out_shape=[
            jax.ShapeDtypeStruct((batch * t_new, WIDTH_BV), BF16),
            jax.ShapeDtypeStruct((batch, N_HEADS_B, DK_B, DV_B), F32),
        ],
        compiler_params=_cparams(("parallel",)),
        name="gla_sample",
    )(proj, proj, proj, proj, proj, proj, gk, w2_l, bgk_l, lno_l, state)


def _merge_kernel(oa_ref, ob_ref, sa_ref, sb_ref, x_ref, woa_ref, wob_ref, wout_ref, ln_ref, x_out_ref, h_ref):
    a = _dot(oa_ref[...], woa_ref[...])
    b = _dot(ob_ref[...], wob_ref[...])
    mix = (sa_ref[...].astype(F32) * a + sb_ref[...].astype(F32) * b).astype(BF16)
    x_new = x_ref[...] + _dot(mix, wout_ref[...])
    x_out_ref[...] = x_new
    h_ref[...] = _rms(x_new, ln_ref[...]).astype(BF16)


def _merge(oa, ob, gates, x, w_oa, w_ob, w_out, ln2, l, tm):
    r = x.shape[0]
    resident = dict(pipeline_mode=pl.Buffered(1))
    row = lambda i: (i, 0)
    return pl.pallas_call(
        _merge_kernel,
        grid=(r // tm,),
        in_specs=[
            pl.BlockSpec((tm, WIDTH_A), row),
            pl.BlockSpec((tm, WIDTH_BV), row),
            pl.BlockSpec((tm, D_MODEL), lambda i: (i, 0)),
            pl.BlockSpec((tm, D_MODEL), lambda i: (i, 1)),
            pl.BlockSpec((tm, D_MODEL), row),
            pl.BlockSpec((None, WIDTH_A, D_MODEL), lambda i: (l, 0, 0), **resident),
            pl.BlockSpec((None, WIDTH_BV, D_MODEL), lambda i: (l, 0, 0), **resident),
            pl.BlockSpec((None, D_MODEL, D_MODEL), lambda i: (l, 0, 0), **resident),
            _ln_spec(l, 1),
        ],
        out_specs=[pl.BlockSpec((tm, D_MODEL), row), pl.BlockSpec((tm, D_MODEL), row)],
        out_shape=[jax.ShapeDtypeStruct((r, D_MODEL), F32), jax.ShapeDtypeStruct((r, D_MODEL), BF16)],
        compiler_params=_cparams(("parallel",)),
        name="merge",
    )(oa, ob, gates, gates, x, w_oa, w_ob, w_out, ln2)


def _gelu_tanh(x):
    return x * (0.5 * (1.0 + jnp.tanh(math.sqrt(2.0 / math.pi) * (x + 0.044715 * (x * x * x)))))


def _conv3(u, p1, p2, w_ref, b_ref):
    return w_ref[0:1, :] * p2 + w_ref[1:2, :] * p1 + w_ref[2:3, :] * u + b_ref[...]


def _history_selectors(n_seq, t_new):
    hist = CONV_W - 1
    e1 = np.zeros((n_seq * t_new, n_seq * hist), np.float32)
    e2 = np.zeros((n_seq * t_new, n_seq * hist), np.float32)
    for b in range(n_seq):
        for t in range(min(t_new, hist)):
            if t < 1:
                e1[b * t_new + t, b * hist + hist - 1 + t] = 1.0
            e2[b * t_new + t, b * hist + t] = 1.0
    return jnp.asarray(e1, BF16), jnp.asarray(e2, BF16)


def _ffn_up_prompt_kernel(tiles_per_seq, h_ref, wv_ref, wg_ref, cv_ref, cg_ref, bv_ref, bg_ref,
                          act_ref, tv_ref, tg_ref, wvb_scr, wgb_scr, carry_v, carry_g):
    i = pl.program_id(1)

    @pl.when(i == 0)
    def _():
        wvb_scr[...] = wv_ref[...].astype(BF16)
        wgb_scr[...] = wg_ref[...].astype(BF16)

    h = h_ref[...]
    tm = h.shape[0]
    first = (i % tiles_per_seq) == 0
    row8 = lax.broadcasted_iota(jnp.int32, (8, 1), 0)

    def branch(wb_scr, carry, c_ref, b_ref, tail_ref):
        u = _dot(h, wb_scr[...])
        prev = jnp.where(first, 0.0, carry[...])
        tail = u[tm - 8:tm, :]
        carry[...] = tail
        tail_ref[...] = tail
        p1 = pltpu.roll(u, 1, 0)
        p2 = pltpu.roll(u, 2, 0)
        body = _conv3(u, p1, p2, c_ref, b_ref)
        p1_top = jnp.where(row8 < 1, pltpu.roll(prev, 1, 0), p1[0:8, :])
        p2_top = jnp.where(row8 < 2, pltpu.roll(prev, 2, 0), p2[0:8, :])
        top = _conv3(u[0:8, :], p1_top, p2_top, c_ref, b_ref)
        return jnp.concatenate([top, body[8:, :]], axis=0)

    val = branch(wvb_scr, carry_v, cv_ref, bv_ref, tv_ref)
    gate = branch(wgb_scr, carry_g, cg_ref, bg_ref, tg_ref)
    act_ref[...] = (_gelu_tanh(gate) * val).astype(act_ref.dtype)


def _ffn_up_sample_kernel(pos_ref, e1_ref, e2_ref, h_ref, wv_ref, wg_ref, cv_ref, cg_ref, bv_ref, bg_ref,
                          sv_ref, sg_ref, act_ref, uv_ref, ug_ref):
    h = h_ref[...]
    pos = pos_ref[...]
    e1 = e1_ref[...]
    e2 = e2_ref[...]

    def branch(w_ref, c_ref, b_ref, s_ref, u_ref):
        u = _dot(h, w_ref[...].astype(BF16))
        u_ref[...] = u
        cached = s_ref[...]
        p1 = jnp.where(pos >= 1, pltpu.roll(u, 1, 0), _dot_exact_lhs01(e1, cached))
        p2 = jnp.where(pos >= 2, pltpu.roll(u, 2, 0), _dot_exact_lhs01(e2, cached))
        return _conv3(u, p1, p2, c_ref, b_ref)

    val = branch(wv_ref, cv_ref, bv_ref, sv_ref, uv_ref)
    gate = branch(wg_ref, cg_ref, bg_ref, sg_ref, ug_ref)
    act_ref[...] = (_gelu_tanh(gate) * val).astype(act_ref.dtype)


def _ffn_up_prompt(h, w_up, w_conv, b_conv, l, tm, tiles_per_seq):
    r = h.shape[0]
    tf = TF_UP
    nt = r // tm
    nj = D_FF // tf
    w_spec = lambda off: pl.BlockSpec((None, D_MODEL, tf), lambda j, i: (l, 0, j + off))
    c_spec = lambda off: pl.BlockSpec((None, CONV_W, tf), lambda j, i: (l, 0, j + off))
    b_spec = lambda off: pl.BlockSpec((None, 1, tf), lambda j, i: (l, 0, j + off))
    tail_spec = pl.BlockSpec((None, 8, tf), lambda j, i: (i, 0, j))
    return pl.pallas_call(
        functools.partial(_ffn_up_prompt_kernel, tiles_per_seq),
        grid=(nj, nt),
        in_specs=[
            pl.BlockSpec((tm, D_MODEL), lambda j, i: (i, 0)),
            w_spec(0), w_spec(nj), c_spec(0), c_spec(nj), b_spec(0), b_spec(nj),
        ],
        out_specs=[pl.BlockSpec((None, tm, tf), lambda j, i: (j, i, 0)), tail_spec, tail_spec],
        out_shape=[
            jax.ShapeDtypeStruct((nj, r, tf), BF16),
            jax.ShapeDtypeStruct((nt, 8, D_FF), F32),
            jax.ShapeDtypeStruct((nt, 8, D_FF), F32),
        ],
        scratch_shapes=[
            pltpu.VMEM((D_MODEL, tf), BF16),
            pltpu.VMEM((D_MODEL, tf), BF16),
            pltpu.VMEM((8, tf), F32),
            pltpu.VMEM((8, tf), F32),
        ],
        compiler_params=_cparams(("arbitrary", "arbitrary")),
        name="ffn_up_prompt",
    )(h, w_up, w_up, w_conv, w_conv, b_conv, b_conv)


def _ffn_up_sample(h, w_up, w_conv, b_conv, conv_state, l, t_new):
    r = h.shape[0]
    tf = TF_UP
    nj = D_FF // tf
    ns = conv_state.shape[1]
    w_spec = lambda off: pl.BlockSpec((None, D_MODEL, tf), lambda j: (l, 0, j + off))
    c_spec = lambda off: pl.BlockSpec((None, CONV_W, tf), lambda j: (l, 0, j + off))
    b_spec = lambda off: pl.BlockSpec((None, 1, tf), lambda j: (l, 0, j + off))
    s_spec = lambda off: pl.BlockSpec((None, ns, tf), lambda j: (l, 0, j + off))
    u_spec = pl.BlockSpec((r, tf), lambda j: (0, j))
    pos = jnp.asarray((np.arange(r) % t_new).astype(np.int32).reshape(r, 1))
    e1, e2 = _history_selectors(r // t_new, t_new)
    return pl.pallas_call(
        _ffn_up_sample_kernel,
        grid=(nj,),
        in_specs=[
            pl.BlockSpec((r, 1), lambda j: (0, 0)),
            pl.BlockSpec((r, ns), lambda j: (0, 0)),
            pl.BlockSpec((r, ns), lambda j: (0, 0)),
            pl.BlockSpec((r, D_MODEL), lambda j: (0, 0)),
            w_spec(0), w_spec(nj), c_spec(0), c_spec(nj), b_spec(0), b_spec(nj),
            s_spec(0), s_spec(nj),
        ],
        out_specs=[pl.BlockSpec((None, r, tf), lambda j: (j, 0, 0)), u_spec, u_spec],
        out_shape=[
            jax.ShapeDtypeStruct((nj, r, tf), BF16),
            jax.ShapeDtypeStruct((r, D_FF), F32),
            jax.ShapeDtypeStruct((r, D_FF), F32),
        ],
        compiler_params=_cparams(("arbitrary",)),
        name="ffn_up_sample",
    )(pos, e1, e2, h, w_up, w_up, w_conv, w_conv, b_conv, b_conv, conv_state, conv_state)


def _down_accumulate(act_ref, w_ref, x_ref, x_out_ref):
    @pl.when(pl.program_id(1) == 0)
    def _():
        x_out_ref[...] = x_ref[...]

    act = act_ref[...]
    step = 512
    for n in range(D_MODEL // step):
        cols = slice(n * step, (n + 1) * step)
        x_out_ref[:, cols] += _dot(act, w_ref[:, cols])


def _down_last_kernel(act_ref, w_ref, x_ref, x_out_ref):
    _down_accumulate(act_ref, w_ref, x_ref, x_out_ref)


def _down_norm_kernel(act_ref, w_ref, x_ref, ln_ref, wgk_ref, x_out_ref, h_ref, gk_ref):
    _down_accumulate(act_ref, w_ref, x_ref, x_out_ref)

    @pl.when(pl.program_id(1) == pl.num_programs(1) - 1)
    def _():
        h = _rms(x_out_ref[...], ln_ref[...]).astype(BF16)
        h_ref[...] = h
        gk_ref[...] = _dot_nt(h, wgk_ref[...].astype(BF16))


def _down(act, w_down, x, ln1, w_in_t, l, tm, with_norm):
    nk, r, tk = act.shape
    row = lambda i, k: (i, 0)
    in_specs = [
        pl.BlockSpec((None, tm, tk), lambda i, k: (k, i, 0)),
        pl.BlockSpec((None, tk, D_MODEL), lambda i, k: (l, k, 0)),
        pl.BlockSpec((tm, D_MODEL), row),
    ]
    out_specs = [pl.BlockSpec((tm, D_MODEL), row)]
    out_shape = [jax.ShapeDtypeStruct((r, D_MODEL), F32)]
    args = [act, w_down, x]
    if with_norm:
        in_specs += [_ln_spec(l + 1, 2), _wgk_spec(l + 1, 2)]
        out_specs += [pl.BlockSpec((tm, D_MODEL), row), pl.BlockSpec((tm, GATE_RANK), row)]
        out_shape += [jax.ShapeDtypeStruct((r, D_MODEL), BF16), jax.ShapeDtypeStruct((r, GATE_RANK), F32)]
        args += [ln1, w_in_t]
    out = pl.pallas_call(
        _down_norm_kernel if with_norm else _down_last_kernel,
        grid=(r // tm, nk),
        in_specs=in_specs,
        out_specs=out_specs,
        out_shape=out_shape,
        compiler_params=_cparams(("parallel", "arbitrary")),
        name="down_norm" if with_norm else "down_last",
    )(*args)
    return out if with_norm else (out[0], None, None)


def kernel(x_prompt, x_sample, cache_win_k, cache_win_v, state_gla, state_conv, rel_bias,
           ln1, w_in, ln_q, ln_k, sinks, w_gk2, b_gk, ln_o, w_oa, w_ob, w_out,
           ln2, w_up, w_conv, b_conv, w_down):
    bp, seq, _ = x_prompt.shape
    bs, t_new, _ = x_sample.shape
    tm_p = TM_PROMPT
    tm_s = bs * t_new
    tiles_per_seq = seq // tm_p

    ln1_3 = ln1.reshape(DEPTH, 1, D_MODEL)
    ln2_3 = ln2.reshape(DEPTH, 1, D_MODEL)
    w_in_t = jnp.swapaxes(w_in, 1, 2)
    w_oa_b, w_ob_b, w_out_b, w_down_b = (w.astype(BF16) for w in (w_oa, w_ob, w_out, w_down))
    w2_b = w_gk2.astype(BF16)
    b_conv3 = b_conv.reshape(DEPTH, 1, 2 * D_FF)
    lnq2 = jnp.tile(ln_q, (1, 2)).reshape(DEPTH, 1, LANES)
    lnk2 = jnp.tile(ln_k, (1, 2)).reshape(DEPTH, 1, LANES)

    bias_p = jnp.stack([_bias_table(rel_bias, jnp.asarray(_dmap_prompt(first))) for first in (True, False)])
    bias_s = _bias_table(rel_bias, jnp.asarray(_dmap_sample(t_new, SAMPLE_GROUP_A)))

    xp = x_prompt.reshape(bp * seq, D_MODEL)
    xs = x_sample.reshape(bs * t_new, D_MODEL)
    win_k = cache_win_k.reshape(DEPTH, bs, WINDOW, KV_WIDTH_A)
    win_v = cache_win_v.reshape(DEPTH, bs, WINDOW, KV_WIDTH_A)
    conv_state = state_conv.reshape(DEPTH, bs * (CONV_W - 1), 2 * D_FF)

    def dense_in(h, l, tm):
        proj = _proj(h, w_in_t, l, tm, TN_MAIN, 0, MAIN_COLS, False, F32, "proj_main")
        gates = _proj(h, w_in_t, l, tm, TN_GATES, OFF_GA, GATE_COLS, True, BF16, "proj_gates")
        return proj, gates

    hp, gkp = _norm(xp, ln1_3, w_in_t, 0, 512)
    hs, gks = _norm(xs, ln1_3, w_in_t, 0, tm_s)

    kp, vp, gp, cp = [], [], [], []
    ksm, vsm, gsm, csm = [], [], [], []
    for l in range(DEPTH):
        lq, lk, sk = lnq2[l], lnk2[l], sinks[l]
        w2_l, bgk_l, lno_l = w2_b[l], b_gk[l].reshape(1, WIDTH_BK), ln_o[l].reshape(1, DV_B)
        more = l + 1 < DEPTH

        proj, gates = dense_in(hp, l, tm_p)
        oa, kn = _attn_prompt(proj, sk, lq, lk, bias_p, bp, seq)
        ob, s_new = _gla_prompt(proj, gkp, w2_l, bgk_l, lno_l, bp, seq)
        xp, h2 = _merge(oa, ob, gates, xp, w_oa_b, w_ob_b, w_out_b, ln2_3, l, TM_MERGE)
        act, tail_v, tail_g = _ffn_up_prompt(h2, w_up, w_conv, b_conv3, l, tm_p, tiles_per_seq)
        xp, hp, gkp = _down(act, w_down_b, xp, ln1_3, w_in_t, l, tm_p, more)
        kp.append(kn.reshape(bp, seq, N_KV_A, HEAD_DIM)[:, -WINDOW:])
        vp.append(proj[:, OFF_VA:OFF_VA + KV_WIDTH_A].reshape(bp, seq, N_KV_A, HEAD_DIM)[:, -WINDOW:])
        gp.append(s_new)
        tails = jnp.concatenate([tail_v, tail_g], axis=-1).reshape(bp, tiles_per_seq, 8, 2 * D_FF)
        cp.append(tails[:, -1, 8 - (CONV_W - 1):])

        proj, gates = dense_in(hs, l, tm_s)
        oa, kn = _attn_sample(proj, win_k[l], win_v[l], sk, lq, lk, bias_s, bs, t_new)
        ob, s_new = _gla_sample(proj, gks, w2_l, bgk_l, lno_l, state_gla, l, bs, t_new)
        xs, h2 = _merge(oa, ob, gates, xs, w_oa_b, w_ob_b, w_out_b, ln2_3, l, tm_s)
        act, u_v, u_g = _ffn_up_sample(h2, w_up, w_conv, b_conv3, conv_state, l, t_new)
        xs, hs, gks = _down(act, w_down_b, xs, ln1_3, w_in_t, l, tm_s, more)
        k_new = kn.reshape(bs, t_new, N_KV_A, HEAD_DIM)
        v_new = proj[:, OFF_VA:OFF_VA + KV_WIDTH_A].reshape(bs, t_new, N_KV_A, HEAD_DIM)
        ksm.append(jnp.concatenate([cache_win_k[l][:, t_new:], k_new], axis=1))
        vsm.append(jnp.concatenate([cache_win_v[l][:, t_new:], v_new], axis=1))
        gsm.append(s_new)
        u = jnp.concatenate([u_v, u_g], axis=-1).reshape(bs, t_new, 2 * D_FF)
        csm.append(u[:, t_new - (CONV_W - 1):])

    return (xp.reshape(bp, seq, D_MODEL), xs.reshape(bs, t_new, D_MODEL),
            jnp.stack(kp), jnp.stack(vp), jnp.stack(gp), jnp.stack(cp),
            jnp.stack(ksm), jnp.stack(vsm), jnp.stack(gsm), jnp.stack(csm))
```

```python
import functools
import math

import numpy as np
import jax
import jax.numpy as jnp
from jax import lax
from jax.experimental import pallas as pl
from jax.experimental.pallas import tpu as pltpu

F32 = jnp.float32
BF16 = jnp.bfloat16

D_MODEL = 2048
DEPTH = 4
HEAD_DIM = 64
N_HEADS_A = 16
N_KV_A = 4
WIDTH_A = N_HEADS_A * HEAD_DIM
KV_WIDTH_A = N_KV_A * HEAD_DIM
WINDOW = 128
N_BUCKETS = 32
MAX_DISTANCE = 128
N_HEADS_B = 4
DK_B = 128
DV_B = 256
WIDTH_BK = N_HEADS_B * DK_B
WIDTH_BV = N_HEADS_B * DV_B
GATE_RANK = 16
GATE_TAU = 16.0
D_FF = 5632
CONV_W = 3
EPS = 1e-6

OFF_QA = 0
OFF_KA = OFF_QA + WIDTH_A
OFF_VA = OFF_KA + KV_WIDTH_A
OFF_QB = OFF_VA + KV_WIDTH_A
OFF_KB = OFF_QB + WIDTH_BK
OFF_VB = OFF_KB + WIDTH_BK
OFF_RB = OFF_VB + WIDTH_BV
OFF_GK = OFF_RB + WIDTH_BV
OFF_GA = OFF_GK + GATE_RANK
MAIN_COLS = OFF_GK
GATE_COLS = 2 * D_MODEL

LANES = 128
LOG2E = math.log2(math.e)
NEG_BIG = -1e30
VMEM_LIMIT = 56 * 1024 * 1024

TM_PROMPT = 1024
TM_MERGE = 256
TN_MAIN = 1536
TN_GATES = 1024
TF_UP = 512
GLA_CHUNK_P = 128
GLA_TBLOCK = 512
SAMPLE_GROUP_A = 8
SAMPLE_GROUP_B = 4


def _cparams(sem):
    return pltpu.CompilerParams(dimension_semantics=sem, vmem_limit_bytes=VMEM_LIMIT)


def _dot(a, b):
    return jnp.dot(a, b, preferred_element_type=F32)


def _dot_nt(a, b):
    return lax.dot_general(a, b, (((1,), (1,)), ((), ())), preferred_element_type=F32)


def _dot_tn(a, b):
    return lax.dot_general(a, b, (((0,), (0,)), ((), ())), preferred_element_type=F32)


def _split3(x):
    hi = x.astype(BF16)
    r = x - hi.astype(F32)
    mid = r.astype(BF16)
    lo = (r - mid.astype(F32)).astype(BF16)
    return hi, mid, lo


def _dot_exact_lhs01(m01, x):
    hi, mid, lo = _split3(x)
    return _dot(m01, hi) + _dot(m01, mid) + _dot(m01, lo)


def _rms(x, g):
    ms = jnp.mean(x * x, axis=-1, keepdims=True)
    return x * lax.rsqrt(ms + EPS) * g


def _norm_kernel(x_ref, ln_ref, wgk_ref, h_ref, gk_ref):
    h = _rms(x_ref[...], ln_ref[...]).astype(BF16)
    h_ref[...] = h
    gk_ref[...] = _dot_nt(h, wgk_ref[...].astype(BF16))


def _ln_spec(l, nargs):
    if nargs == 1:
        return pl.BlockSpec((None, 1, D_MODEL), lambda i: (l, 0, 0))
    return pl.BlockSpec((None, 1, D_MODEL), lambda i, j: (l, 0, 0))


def _wgk_spec(l, nargs):
    blk = OFF_GK // GATE_RANK
    if nargs == 1:
        return pl.BlockSpec((None, GATE_RANK, D_MODEL), lambda i: (l, blk, 0))
    return pl.BlockSpec((None, GATE_RANK, D_MODEL), lambda i, j: (l, blk, 0))


def _norm(x, ln1, w_in_t, l, tm):
    r = x.shape[0]
    return pl.pallas_call(
        _norm_kernel,
        grid=(r // tm,),
        in_specs=[pl.BlockSpec((tm, D_MODEL), lambda i: (i, 0)), _ln_spec(l, 1), _wgk_spec(l, 1)],
        out_specs=[pl.BlockSpec((tm, D_MODEL), lambda i: (i, 0)), pl.BlockSpec((tm, GATE_RANK), lambda i: (i, 0))],
        out_shape=[jax.ShapeDtypeStruct((r, D_MODEL), BF16), jax.ShapeDtypeStruct((r, GATE_RANK), F32)],
        compiler_params=_cparams(("parallel",)),
        name="norm",
    )(x, ln1, w_in_t)


def _proj_kernel(sigmoid, h_ref, w_ref, o_ref, wb_scr):
    @pl.when(pl.program_id(1) == 0)
    def _():
        wb_scr[...] = w_ref[...].astype(BF16)

    y = _dot_nt(h_ref[...], wb_scr[...])
    if sigmoid:
        y = jax.nn.sigmoid(y)
    o_ref[...] = y.astype(o_ref.dtype)


def _proj(h, w_in_t, l, tm, tn, row_off, n_cols, sigmoid, out_dtype, name):
    r = h.shape[0]
    if row_off == 0:
        w_spec = pl.BlockSpec((None, tn, D_MODEL), lambda j, i: (l, j, 0))
    else:
        w_spec = pl.BlockSpec((pl.Squeezed(), pl.Element(tn), pl.Element(D_MODEL)),
                              lambda j, i: (l, pl.multiple_of(row_off + j * tn, GATE_RANK), 0))
    return pl.pallas_call(
        functools.partial(_proj_kernel, sigmoid),
        grid=(n_cols // tn, r // tm),
        in_specs=[pl.BlockSpec((tm, D_MODEL), lambda j, i: (i, 0)), w_spec],
        out_specs=pl.BlockSpec((tm, tn), lambda j, i: (i, j)),
        out_shape=jax.ShapeDtypeStruct((r, n_cols), out_dtype),
        scratch_shapes=[pltpu.VMEM((tn, D_MODEL), BF16)],
        compiler_params=_cparams(("arbitrary", "arbitrary")),
        name=name,
    )(h, w_in_t)


def _bucket_ranges():
    d = np.arange(WINDOW)
    max_exact = N_BUCKETS // 2
    df = np.maximum(d, 1).astype(np.float32)
    large = max_exact + (np.log(df / np.float32(max_exact)) / np.float32(math.log(MAX_DISTANCE / max_exact))
                         * np.float32(N_BUCKETS - max_exact)).astype(np.int32)
    large = np.minimum(large, N_BUCKETS - 1)
    bucket = np.where(d < max_exact, d, large)
    ranges = []
    for b in range(N_BUCKETS):
        idx = np.nonzero(bucket == b)[0]
        if idx.size:
            assert idx[-1] - idx[0] + 1 == idx.size
            ranges.append((b, int(idx[0]), int(idx[-1])))
    return ranges


def _bias_kernel(rel_ref, d_ref, o_ref):
    h = pl.program_id(0)
    d = d_ref[...]
    val = jnp.full(d.shape, NEG_BIG, F32)
    for b, lo, hi in _bucket_ranges():
        val = jnp.where((d >= lo) & (d <= hi), rel_ref[b, h] * LOG2E, val)
    o_ref[...] = val


def _bias_table(rel_bias, dmap):
    mq, nk = dmap.shape
    return pl.pallas_call(
        _bias_kernel,
        grid=(N_HEADS_A,),
        in_specs=[
            pl.BlockSpec(memory_space=pltpu.SMEM),
            pl.BlockSpec((mq, nk), lambda h: (0, 0)),
        ],
        out_specs=pl.BlockSpec((None, mq, nk), lambda h: (h, 0, 0)),
        out_shape=jax.ShapeDtypeStruct((N_HEADS_A, mq, nk), F32),
        compiler_params=_cparams(("parallel",)),
        name="bias_table",
    )(rel_bias, dmap)


def _dmap_prompt(first_block):
    i = np.arange(WINDOW)[:, None]
    j = np.arange(2 * WINDOW)[None, :]
    d = i + WINDOW - j
    ok = (d >= 0) & (d < WINDOW)
    if first_block:
        ok = ok & (j >= WINDOW)
    return np.where(ok, d, -1).astype(np.int32)


def _dmap_sample(t_new, group):
    nk = group * WINDOW + LANES
    rows = np.arange(group * t_new)
    rb, rt = rows // t_new, rows % t_new
    d = np.full((group * t_new, nk), -1, np.int64)
    cols = np.arange(group * WINDOW)
    cb, cs = cols // WINDOW, cols % WINDOW
    dw = rt[:, None] + WINDOW - cs[None, :]
    ok = (rb[:, None] == cb[None, :]) & (dw >= 0) & (dw < WINDOW)
    d[:, :group * WINDOW] = np.where(ok, dw, -1)
    ncols = np.arange(group * t_new)
    nb, nu = ncols // t_new, ncols % t_new
    dn = rt[:, None] - nu[None, :]
    okn = (rb[:, None] == nb[None, :]) & (dn >= 0)
    d[:, group * WINDOW:group * WINDOW + group * t_new] = np.where(okn, dn, -1)
    return d.astype(np.int32)


def _group_ones():
    r = lax.broadcasted_iota(jnp.int32, (LANES, LANES), 0)
    c = lax.broadcasted_iota(jnp.int32, (LANES, LANES), 1)
    low_c = jnp.where(c < HEAD_DIM, 1.0, 0.0)
    return jnp.where(r < HEAD_DIM, low_c, 1.0 - low_c).astype(BF16)


def _head_norm(x, gmat, ln2):
    sq = x * x
    hi = sq.astype(BF16)
    lo = (sq - hi.astype(F32)).astype(BF16)
    ms = (_dot(hi, gmat) + _dot(lo, gmat)) * (1.0 / HEAD_DIM)
    return x * lax.rsqrt(ms + EPS) * ln2


def _split_heads(x):
    lane = lax.broadcasted_iota(jnp.int32, (x.shape[0], LANES), 1)
    low = lane < HEAD_DIM
    lo_parts, hi_parts = [], []
    for c in range(KV_WIDTH_A // LANES):
        xc = x[:, c * LANES:(c + 1) * LANES]
        xr = pltpu.roll(xc, HEAD_DIM, 1)
        lo_parts += [jnp.where(low, xc, 0.0).astype(BF16), jnp.where(low, xr, 0.0).astype(BF16)]
        hi_parts += [jnp.where(low, 0.0, xr).astype(BF16), jnp.where(low, 0.0, xc).astype(BF16)]
    return lo_parts, hi_parts


def _attend(q_cols, k_parts, v_parts, bias_ref, sink_ref):
    mq = q_cols[0].shape[0]
    pairs_per_kv = N_HEADS_A // N_KV_A // 2
    assert pairs_per_kv == 2
    first_rows = lax.broadcasted_iota(jnp.int32, (pairs_per_kv * mq, 1), 0) < mq
    outs = [None] * (N_HEADS_A // 2)
    for kh in range(N_KV_A):
        pcs = [kh * pairs_per_kv + p for p in range(pairs_per_kv)]
        q_stack = jnp.concatenate([q_cols[pc] for pc in pcs], axis=0)
        for half in range(2):
            heads = [2 * pc + half for pc in pcs]
            bias = jnp.concatenate([bias_ref[h] for h in heads], axis=0)
            sink = jnp.where(first_rows, sink_ref[heads[0]], sink_ref[heads[1]]) * LOG2E
            s = _dot_nt(q_stack, k_parts[half][kh]) + bias
            m = jnp.maximum(jnp.max(s, axis=-1, keepdims=True), sink)
            e = jnp.exp2(s - m)
            den = jnp.sum(e, axis=-1, keepdims=True) + jnp.exp2(sink - m)
            o = _dot(e.astype(BF16), v_parts[half][kh]) * (1.0 / den)
            for p, pc in enumerate(pcs):
                o_p = o[p * mq:(p + 1) * mq, :]
                outs[pc] = o_p if outs[pc] is None else outs[pc] + o_p
    return outs


def _attn_prompt_kernel(sink_ref, q_ref, ko_ref, vo_ref, lnq_ref, lnk_ref, bias_ref,
                        o_ref, kn_ref, kprev_scr, vprev_scr):
    n = pl.program_id(1)
    gmat = _group_ones()
    lnq = lnq_ref[...]
    lnk = lnk_ref[...]
    scale = HEAD_DIM ** -0.5 * LOG2E

    @pl.when(n == 0)
    def _():
        kprev_scr[...] = jnp.zeros_like(kprev_scr)
        vprev_scr[...] = jnp.zeros_like(vprev_scr)

    kn_own = jnp.concatenate(
        [_head_norm(ko_ref[:, c * LANES:(c + 1) * LANES], gmat, lnk) for c in range(KV_WIDTH_A // LANES)], axis=1)
    kn_ref[...] = kn_own
    k_own = _split_heads(kn_own)
    v_own = _split_heads(vo_ref[...])
    k_parts = [[jnp.concatenate([kprev_scr[half, kh], k_own[half][kh]], axis=0) for kh in range(N_KV_A)]
               for half in range(2)]
    v_parts = [[jnp.concatenate([vprev_scr[half, kh], v_own[half][kh]], axis=0) for kh in range(N_KV_A)]
               for half in range(2)]
    q_cols = [(_head_norm(q_ref[:, c * LANES:(c + 1) * LANES], gmat, lnq) * scale).astype(BF16)
              for c in range(WIDTH_A // LANES)]
    outs = _attend(q_cols, k_parts, v_parts, bias_ref, sink_ref)
    for c, o in enumerate(outs):
        o_ref[:, c * LANES:(c + 1) * LANES] = o.astype(o_ref.dtype)
    for half in range(2):
        for kh in range(N_KV_A):
            kprev_scr[half, kh] = k_own[half][kh]
            vprev_scr[half, kh] = v_own[half][kh]


def _attn_prompt(proj, sinks_l, lnq2, lnk2, bias_p, batch, seq):
    nb = seq // WINDOW
    r = batch * seq
    kcol = OFF_KA // KV_WIDTH_A
    vcol = OFF_VA // KV_WIDTH_A
    own = lambda b, n: (b * nb + n, 0)
    return pl.pallas_call(
        _attn_prompt_kernel,
        grid=(batch, nb),
        in_specs=[
            pl.BlockSpec(memory_space=pltpu.SMEM),
            pl.BlockSpec((WINDOW, WIDTH_A), own),
            pl.BlockSpec((WINDOW, KV_WIDTH_A), lambda b, n: (b * nb + n, kcol)),
            pl.BlockSpec((WINDOW, KV_WIDTH_A), lambda b, n: (b * nb + n, vcol)),
            pl.BlockSpec((1, LANES), lambda b, n: (0, 0)),
            pl.BlockSpec((1, LANES), lambda b, n: (0, 0)),
            pl.BlockSpec((None, N_HEADS_A, WINDOW, 2 * WINDOW), lambda b, n: (jnp.minimum(n, 1), 0, 0, 0)),
        ],
        out_specs=[
            pl.BlockSpec((WINDOW, WIDTH_A), own),
            pl.BlockSpec((WINDOW, KV_WIDTH_A), own),
        ],
        out_shape=[
            jax.ShapeDtypeStruct((r, WIDTH_A), BF16),
            jax.ShapeDtypeStruct((r, KV_WIDTH_A), F32),
        ],
        scratch_shapes=[pltpu.VMEM((2, N_KV_A, WINDOW, LANES), BF16), pltpu.VMEM((2, N_KV_A, WINDOW, LANES), BF16)],
        compiler_params=_cparams(("parallel", "arbitrary")),
        name="attn_prompt",
    )(sinks_l, proj, proj, proj, lnq2, lnk2, bias_p)


def _attn_sample_kernel(sink_ref, q_ref, kn_in_ref, vn_ref, wk_ref, wv_ref, lnq_ref, lnk_ref, bias_ref,
                        o_ref, kn_ref):
    gmat = _group_ones()
    lnq = lnq_ref[...]
    lnk = lnk_ref[...]
    scale = HEAD_DIM ** -0.5 * LOG2E
    rows = q_ref.shape[0]
    g = wk_ref.shape[0]
    kn_new = jnp.concatenate(
        [_head_norm(kn_in_ref[:, c * LANES:(c + 1) * LANES], gmat, lnk) for c in range(KV_WIDTH_A // LANES)], axis=1)
    kn_ref[...] = kn_new
    pad = jnp.zeros((LANES - rows, KV_WIDTH_A), F32)
    k = jnp.concatenate([wk_ref[...].reshape(g * WINDOW, KV_WIDTH_A), kn_new, pad], axis=0)
    v = jnp.concatenate([wv_ref[...].reshape(g * WINDOW, KV_WIDTH_A), vn_ref[...], pad], axis=0)
    q_cols = [(_head_norm(q_ref[:, c * LANES:(c + 1) * LANES], gmat, lnq) * scale).astype(BF16)
              for c in range(WIDTH_A // LANES)]
    outs = _attend(q_cols, _split_heads(k), _split_heads(v), bias_ref, sink_ref)
    for c, o in enumerate(outs):
        o_ref[:, c * LANES:(c + 1) * LANES] = o.astype(o_ref.dtype)


def _attn_sample(proj, win_k, win_v, sinks_l, lnq2, lnk2, bias_s, batch, t_new):
    g = SAMPLE_GROUP_A
    rows = g * t_new
    nk = g * WINDOW + LANES
    kcol = OFF_KA // KV_WIDTH_A
    vcol = OFF_VA // KV_WIDTH_A
    return pl.pallas_call(
        _attn_sample_kernel,
        grid=(batch // g,),
        in_specs=[
            pl.BlockSpec(memory_space=pltpu.SMEM),
            pl.BlockSpec((rows, WIDTH_A), lambda i: (i, 0)),
            pl.BlockSpec((rows, KV_WIDTH_A), lambda i: (i, kcol)),
            pl.BlockSpec((rows, KV_WIDTH_A), lambda i: (i, vcol)),
            pl.BlockSpec((g, WINDOW, KV_WIDTH_A), lambda i: (i, 0, 0)),
            pl.BlockSpec((g, WINDOW, KV_WIDTH_A), lambda i: (i, 0, 0)),
            pl.BlockSpec((1, LANES), lambda i: (0, 0)),
            pl.BlockSpec((1, LANES), lambda i: (0, 0)),
            pl.BlockSpec((N_HEADS_A, rows, nk), lambda i: (0, 0, 0)),
        ],
        out_specs=[
            pl.BlockSpec((rows, WIDTH_A), lambda i: (i, 0)),
            pl.BlockSpec((rows, KV_WIDTH_A), lambda i: (i, 0)),
        ],
        out_shape=[
            jax.ShapeDtypeStruct((batch * t_new, WIDTH_A), BF16),
            jax.ShapeDtypeStruct((batch * t_new, KV_WIDTH_A), F32),
        ],
        compiler_params=_cparams(("parallel",)),
        name="attn_sample",
    )(sinks_l, proj, proj, proj, win_k, win_v, lnq2, lnk2, bias_s)


def _log_decay(gk, w2_ref, bgk_ref):
    x = _dot(gk.astype(BF16), w2_ref[...]) + bgk_ref[...]
    log_sig = jnp.minimum(x, 0.0) - jnp.log(1.0 + jnp.exp(-jnp.abs(x)))
    return log_sig * (1.0 / GATE_TAU)


def _column(row):
    r = lax.broadcasted_iota(jnp.int32, (LANES, LANES), 0)
    c = lax.broadcasted_iota(jnp.int32, (LANES, LANES), 1)
    return jnp.sum(jnp.where(r == c, row, 0.0), axis=1, keepdims=True)


def _gla_out(o, ln_o, rgate):
    return (_rms(o, ln_o) * (rgate * jax.nn.sigmoid(rgate))).astype(BF16)


def _gla_prompt_kernel(q_ref, k_ref, va_ref, vb_ref, ra_ref, rb_ref, gk_ref, w2_ref, bgk_ref, lno_ref,
                       o_ref, s_out_ref, s_scr, b_scr):
    t = pl.program_id(1)
    c_len = GLA_CHUNK_P
    tb = q_ref.shape[0]
    n_chunks = tb // c_len
    half = c_len // 2
    scale = DK_B ** -0.5

    @pl.when(t == 0)
    def _():
        s_scr[...] = jnp.zeros_like(s_scr)

    shift = c_len.bit_length() - 1
    rb = lax.broadcasted_iota(jnp.int32, (tb, tb), 0)
    cb = lax.broadcasted_iota(jnp.int32, (tb, tb), 1)
    same_chunk = lax.shift_right_logical(rb, shift) == lax.shift_right_logical(cb, shift)
    tri_blocks = jnp.where(same_chunk & (rb >= cb), 1.0, 0.0).astype(BF16)
    b_scr[...] = _dot_exact_lhs01(tri_blocks, _log_decay(gk_ref[...], w2_ref, bgk_ref))

    ri = lax.broadcasted_iota(jnp.int32, (c_len, c_len), 0)
    ci = lax.broadcasted_iota(jnp.int32, (c_len, c_len), 1)
    quarter = half // 2
    same_half = jnp.where(ci < half, jnp.where(ri < half, 1, 0), jnp.where(ri < half, 0, 1))
    own_causal = (ri >= ci) & (same_half == 1)
    upper_row = lax.broadcasted_iota(jnp.int32, (c_len, 1), 0) >= half
    ln_o = lno_ref[...]
    v_refs = (va_ref, vb_ref)
    r_refs = (ra_ref, rb_ref)

    def chunk(c, carry):
        rows = pl.ds(pl.multiple_of(c * c_len, c_len), c_len)
        for h in range(N_HEADS_B):
            ks = slice(h * DK_B, (h + 1) * DK_B)
            vs = slice((h % 2) * DV_B, (h % 2 + 1) * DV_B)
            bh = b_scr[rows, ks]
            b_last = bh[c_len - 1:c_len, :]
            b_half = bh[half - 1:half, :]
            q = q_ref[rows, ks] * scale
            k = k_ref[rows, ks]
            v = v_refs[h // 2][rows, vs].astype(BF16)
            q_inter = (q * jnp.exp(bh)).astype(BF16)
            r_own = jnp.where(upper_row, bh[half + quarter - 1:half + quarter, :], bh[quarter - 1:quarter, :])
            a_own = _dot_nt((q * jnp.exp(bh - r_own)).astype(BF16), (k * jnp.exp(r_own - bh)).astype(BF16))
            q_x = jnp.where(upper_row, q * jnp.exp(bh - b_half), 0.0).astype(BF16)
            k_x = jnp.where(upper_row, 0.0, k * jnp.exp(b_half - bh)).astype(BF16)
            a = (jnp.where(own_causal, a_own, 0.0) + _dot_nt(q_x, k_x)).astype(BF16)
            k_state = (k * jnp.exp(b_last - bh)).astype(BF16)
            s_old = s_scr[h]
            o = _dot(a, v) + _dot(q_inter, s_old.astype(BF16))
            s_scr[h] = _column(jnp.exp(b_last)) * s_old + _dot_tn(k_state, v)
            o_ref[rows, h * DV_B:(h + 1) * DV_B] = _gla_out(o, ln_o, r_refs[h // 2][rows, vs])
        return carry

    lax.fori_loop(0, n_chunks, chunk, 0, unroll=4)

    @pl.when(t == pl.num_programs(1) - 1)
    def _():
        s_out_ref[...] = s_scr[...]


def _gla_prompt(proj, gk, w2_l, bgk_l, lno_l, batch, seq):
    tb = GLA_TBLOCK
    nt = seq // tb
    w = 2 * DV_B
    row = lambda b, t: b * nt + t
    spec = lambda col: pl.BlockSpec((tb, w), lambda b, t: (row(b, t), col))
    return pl.pallas_call(
        _gla_prompt_kernel,
        grid=(batch, nt),
        in_specs=[
            spec(OFF_QB // w), spec(OFF_KB // w),
            spec(OFF_VB // w), spec(OFF_VB // w + 1),
            spec(OFF_RB // w), spec(OFF_RB // w + 1),
            pl.BlockSpec((tb, GATE_RANK), lambda b, t: (row(b, t), 0)),
            pl.BlockSpec((GATE_RANK, WIDTH_BK), lambda b, t: (0, 0)),
            pl.BlockSpec((1, WIDTH_BK), lambda b, t: (0, 0)),
            pl.BlockSpec((1, DV_B), lambda b, t: (0, 0)),
        ],
        out_specs=[
            pl.BlockSpec((tb, WIDTH_BV), lambda b, t: (row(b, t), 0)),
            pl.BlockSpec((None, N_HEADS_B, DK_B, DV_B), lambda b, t: (b, 0, 0, 0)),
        ],
        out_shape=[
            jax.ShapeDtypeStruct((batch * seq, WIDTH_BV), BF16),
            jax.ShapeDtypeStruct((batch, N_HEADS_B, DK_B, DV_B), F32),
        ],
        scratch_shapes=[pltpu.VMEM((N_HEADS_B, DK_B, DV_B), F32), pltpu.VMEM((tb, WIDTH_BK), F32)],
        compiler_params=_cparams(("parallel", "arbitrary")),
        name="gla_prompt",
    )(proj, proj, proj, proj, proj, proj, gk, w2_l, bgk_l, lno_l)


def _gla_sample_kernel(t_new, q_ref, k_ref, va_ref, vb_ref, ra_ref, rb_ref, gk_ref, w2_ref, bgk_ref, lno_ref,
                       s_ref, o_ref, s_out_ref):
    rows = q_ref.shape[0]
    n_seq = rows // t_new
    scale = DK_B ** -0.5
    ri = lax.broadcasted_iota(jnp.int32, (rows, rows), 0)
    ci = lax.broadcasted_iota(jnp.int32, (rows, rows), 1)
    same = None
    for sq in range(n_seq):
        lo, hi = sq * t_new, (sq + 1) * t_new
        blk = (ri >= lo) & (ri < hi) & (ci >= lo) & (ci < hi)
        same = blk if same is None else (same | blk)
    causal = same & (ri >= ci)
    tri = jnp.where(causal, 1.0, 0.0).astype(BF16)
    rcol = lax.broadcasted_iota(jnp.int32, (rows, 1), 0)
    ln_o = lno_ref[...]
    v_refs = (va_ref, vb_ref)
    r_refs = (ra_ref, rb_ref)

    g = _log_decay(gk_ref[...], w2_ref, bgk_ref)
    b = _dot_exact_lhs01(tri, g)
    for h in range(N_HEADS_B):
        ks = slice(h * DK_B, (h + 1) * DK_B)
        vs = slice((h % 2) * DV_B, (h % 2 + 1) * DV_B)
        bh = b[:, ks]
        gh = g[:, ks]
        q = q_ref[:, ks] * scale
        k = k_ref[:, ks]
        v = v_refs[h // 2][:, vs].astype(BF16)
        q_inter = q * jnp.exp(bh)
        k_t = (k * jnp.exp(-bh)).astype(BF16)
        a = jnp.where(causal, _dot_nt(q_inter.astype(BF16), k_t), 0.0).astype(BF16)
        o = _dot(a, v)
        for sq in range(n_seq):
            mine = (rcol >= sq * t_new) & (rcol < (sq + 1) * t_new)
            b_last = jnp.sum(jnp.where(mine, gh, 0.0), axis=0, keepdims=True)
            k_state = jnp.where(mine, k * jnp.exp(b_last - bh), 0.0).astype(BF16)
            q_mine = jnp.where(mine, q_inter, 0.0).astype(BF16)
            s_old = s_ref[sq, h]
            o = o + _dot(q_mine, s_old.astype(BF16))
            s_out_ref[sq, h] = _column(jnp.exp(b_last)) * s_old + _dot_tn(k_state, v)
        o_ref[:, h * DV_B:(h + 1) * DV_B] = _gla_out(o, ln_o, r_refs[h // 2][:, vs])


def _gla_sample(proj, gk, w2_l, bgk_l, lno_l, state, l, batch, t_new):
    g = SAMPLE_GROUP_B
    rows = g * t_new
    w = 2 * DV_B
    spec = lambda col: pl.BlockSpec((rows, w), lambda i: (i, col))
    return pl.pallas_call(
        functools.partial(_gla_sample_kernel, t_new),
        grid=(batch // g,),
        in_specs=[
            spec(OFF_QB // w), spec(OFF_KB // w),
            spec(OFF_VB // w), spec(OFF_VB // w + 1),
            spec(OFF_RB // w), spec(OFF_RB // w + 1),
            pl.BlockSpec((rows, GATE_RANK), lambda i: (i, 0)),
            pl.BlockSpec((GATE_RANK, WIDTH_BK), lambda i: (0, 0)),
            pl.BlockSpec((1, WIDTH_BK), lambda i: (0, 0)),
            pl.BlockSpec((1, DV_B), lambda i: (0, 0)),
            pl.BlockSpec((None, g, N_HEADS_B, DK_B, DV_B), lambda i: (l, i, 0, 0, 0)),
        ],
        out_specs=[
            pl.BlockSpec((rows, WIDTH_BV), lambda i: (i, 0)),
            pl.BlockSpec((g, N_HEADS_B, DK_B, DV_B), lambda i: (i, 0, 0, 0)),
        ],
        out_shape=[
            jax.ShapeDtypeStruct((batch * t_new, WIDTH_BV), BF16),
            jax.ShapeDtypeStruct((batch, N_HEADS_B, DK_B, DV_B), F32),
        ],
        compiler_params=_cparams(("parallel",)),
        name="gla_sample",
    )(proj, proj, proj, proj, proj, proj, gk, w2_l, bgk_l, lno_l, state)


def _merge_kernel(oa_ref, ob_ref, sa_ref, sb_ref, x_ref, woa_ref, wob_ref, wout_ref, ln_ref, x_out_ref, h_ref):
    a = _dot(oa_ref[...], woa_ref[...])
    b = _dot(ob_ref[...], wob_ref[...])
    mix = (sa_ref[...].astype(F32) * a + sb_ref[...].astype(F32) * b).astype(BF16)
    x_new = x_ref[...] + _dot(mix, wout_ref[...])
    x_out_ref[...] = x_new
    h_ref[...] = _rms(x_new, ln_ref[...]).astype(BF16)


def _merge(oa, ob, gates, x, w_oa, w_ob, w_out, ln2, l, tm):
    r = x.shape[0]
    resident = dict(pipeline_mode=pl.Buffered(1))
    row = lambda i: (i, 0)
    return pl.pallas_call(
        _merge_kernel,
        grid=(r // tm,),
        in_specs=[
            pl.BlockSpec((tm, WIDTH_A), row),
            pl.BlockSpec((tm, WIDTH_BV), row),
            pl.BlockSpec((tm, D_MODEL), lambda i: (i, 0)),
            pl.BlockSpec((tm, D_MODEL), lambda i: (i, 1)),
            pl.BlockSpec((tm, D_MODEL), row),
            pl.BlockSpec((None, WIDTH_A, D_MODEL), lambda i: (l, 0, 0), **resident),
            pl.BlockSpec((None, WIDTH_BV, D_MODEL), lambda i: (l, 0, 0), **resident),
            pl.BlockSpec((None, D_MODEL, D_MODEL), lambda i: (l, 0, 0), **resident),
            _ln_spec(l, 1),
        ],
        out_specs=[pl.BlockSpec((tm, D_MODEL), row), pl.BlockSpec((tm, D_MODEL), row)],
        out_shape=[jax.ShapeDtypeStruct((r, D_MODEL), F32), jax.ShapeDtypeStruct((r, D_MODEL), BF16)],
        compiler_params=_cparams(("parallel",)),
        name="merge",
    )(oa, ob, gates, gates, x, w_oa, w_ob, w_out, ln2)


def _gelu_tanh(x):
    return x * (0.5 * (1.0 + jnp.tanh(math.sqrt(2.0 / math.pi) * (x + 0.044715 * (x * x * x)))))


def _conv3(u, p1, p2, w_ref, b_ref):
    return w_ref[0:1, :] * p2 + w_ref[1:2, :] * p1 + w_ref[2:3, :] * u + b_ref[...]


def _history_selectors(n_seq, t_new):
    hist = CONV_W - 1
    e1 = np.zeros((n_seq * t_new, n_seq * hist), np.float32)
    e2 = np.zeros((n_seq * t_new, n_seq * hist), np.float32)
    for b in range(n_seq):
        for t in range(min(t_new, hist)):
            if t < 1:
                e1[b * t_new + t, b * hist + hist - 1 + t] = 1.0
            e2[b * t_new + t, b * hist + t] = 1.0
    return jnp.asarray(e1, BF16), jnp.asarray(e2, BF16)


def _ffn_up_prompt_kernel(tiles_per_seq, h_ref, wv_ref, wg_ref, cv_ref, cg_ref, bv_ref, bg_ref,
                          act_ref, tv_ref, tg_ref, wvb_scr, wgb_scr, carry_v, carry_g):
    i = pl.program_id(1)

    @pl.when(i == 0)
    def _():
        wvb_scr[...] = wv_ref[...].astype(BF16)
        wgb_scr[...] = wg_ref[...].astype(BF16)

    h = h_ref[...]
    tm = h.shape[0]
    first = (i % tiles_per_seq) == 0
    row8 = lax.broadcasted_iota(jnp.int32, (8, 1), 0)

    def branch(wb_scr, carry, c_ref, b_ref, tail_ref):
        u = _dot(h, wb_scr[...])
        prev = jnp.where(first, 0.0, carry[...])
        tail = u[tm - 8:tm, :]
        carry[...] = tail
        tail_ref[...] = tail
        p1 = pltpu.roll(u, 1, 0)
        p2 = pltpu.roll(u, 2, 0)
        body = _conv3(u, p1, p2, c_ref, b_ref)
        p1_top = jnp.where(row8 < 1, pltpu.roll(prev, 1, 0), p1[0:8, :])
        p2_top = jnp.where(row8 < 2, pltpu.roll(prev, 2, 0), p2[0:8, :])
        top = _conv3(u[0:8, :], p1_top, p2_top, c_ref, b_ref)
        return jnp.concatenate([top, body[8:, :]], axis=0)

    gate = _gelu_tanh(branch(wgb_scr, carry_g, cg_ref, bg_ref, tg_ref))
    val = branch(wvb_scr, carry_v, cv_ref, bv_ref, tv_ref)
    act_ref[...] = (gate * val).astype(act_ref.dtype)


def _ffn_up_sample_kernel(pos_ref, e1_ref, e2_ref, h_ref, wv_ref, wg_ref, cv_ref, cg_ref, bv_ref, bg_ref,
                          sv_ref, sg_ref, act_ref, uv_ref, ug_ref):
    h = h_ref[...]
    pos = pos_ref[...]
    e1 = e1_ref[...]
    e2 = e2_ref[...]

    def branch(w_ref, c_ref, b_ref, s_ref, u_ref):
        u = _dot(h, w_ref[...].astype(BF16))
        u_ref[...] = u
        cached = s_ref[...]
        p1 = jnp.where(pos >= 1, pltpu.roll(u, 1, 0), _dot_exact_lhs01(e1, cached))
        p2 = jnp.where(pos >= 2, pltpu.roll(u, 2, 0), _dot_exact_lhs01(e2, cached))
        return _conv3(u, p1, p2, c_ref, b_ref)

    val = branch(wv_ref, cv_ref, bv_ref, sv_ref, uv_ref)
    gate = branch(wg_ref, cg_ref, bg_ref, sg_ref, ug_ref)
    act_ref[...] = (_gelu_tanh(gate) * val).astype(act_ref.dtype)


def _ffn_up_prompt(h, w_up, w_conv, b_conv, l, tm, tiles_per_seq):
    r = h.shape[0]
    tf = TF_UP
    nt = r // tm
    nj = D_FF // tf
    w_spec = lambda off: pl.BlockSpec((None, D_MODEL, tf), lambda j, i: (l, 0, j + off))
    c_spec = lambda off: pl.BlockSpec((None, CONV_W, tf), lambda j, i: (l, 0, j + off))
    b_spec = lambda off: pl.BlockSpec((None, 1, tf), lambda j, i: (l, 0, j + off))
    tail_spec = pl.BlockSpec((None, 8, tf), lambda j, i: (i, 0, j))
    return pl.pallas_call(
        functools.partial(_ffn_up_prompt_kernel, tiles_per_seq),
        grid=(nj, nt),
        in_specs=[
            pl.BlockSpec((tm, D_MODEL), lambda j, i: (i, 0)),
            w_spec(0), w_spec(nj), c_spec(0), c_spec(nj), b_spec(0), b_spec(nj),
        ],
        out_specs=[pl.BlockSpec((None, tm, tf), lambda j, i: (j, i, 0)), tail_spec, tail_spec],
        out_shape=[
            jax.ShapeDtypeStruct((nj, r, tf), BF16),
            jax.ShapeDtypeStruct((nt, 8, D_FF), F32),
            jax.ShapeDtypeStruct((nt, 8, D_FF), F32),
        ],
        scratch_shapes=[
            pltpu.VMEM((D_MODEL, tf), BF16),
            pltpu.VMEM((D_MODEL, tf), BF16),
            pltpu.VMEM((8, tf), F32),
            pltpu.VMEM((8, tf), F32),
        ],
        compiler_params=_cparams(("arbitrary", "arbitrary")),
        name="ffn_up_prompt",
    )(h, w_up, w_up, w_conv, w_conv, b_conv, b_conv)


def _ffn_up_sample(h, w_up, w_conv, b_conv, conv_state, l, t_new):
    r = h.shape[0]
    tf = TF_UP
    nj = D_FF // tf
    ns = conv_state.shape[1]
    w_spec = lambda off: pl.BlockSpec((None, D_MODEL, tf), lambda j: (l, 0, j + off))
    c_spec = lambda off: pl.BlockSpec((None, CONV_W, tf), lambda j: (l, 0, j + off))
    b_spec = lambda off: pl.BlockSpec((None, 1, tf), lambda j: (l, 0, j + off))
    s_spec = lambda off: pl.BlockSpec((None, ns, tf), lambda j: (l, 0, j + off))
    u_spec = pl.BlockSpec((r, tf), lambda j: (0, j))
    pos = jnp.asarray((np.arange(r) % t_new).astype(np.int32).reshape(r, 1))
    e1, e2 = _history_selectors(r // t_new, t_new)
    return pl.pallas_call(
        _ffn_up_sample_kernel,
        grid=(nj,),
        in_specs=[
            pl.BlockSpec((r, 1), lambda j: (0, 0)),
            pl.BlockSpec((r, ns), lambda j: (0, 0)),
            pl.BlockSpec((r, ns), lambda j: (0, 0)),
            pl.BlockSpec((r, D_MODEL), lambda j: (0, 0)),
            w_spec(0), w_spec(nj), c_spec(0), c_spec(nj), b_spec(0), b_spec(nj),
            s_spec(0), s_spec(nj),
        ],
        out_specs=[pl.BlockSpec((None, r, tf), lambda j: (j, 0, 0)), u_spec, u_spec],
        out_shape=[
            jax.ShapeDtypeStruct((nj, r, tf), BF16),
            jax.ShapeDtypeStruct((r, D_FF), F32),
            jax.ShapeDtypeStruct((r, D_FF), F32),
        ],
        compiler_params=_cparams(("arbitrary",)),
        name="ffn_up_sample",
    )(pos, e1, e2, h, w_up, w_up, w_conv, w_conv, b_conv, b_conv, conv_state, conv_state)


def _down_accumulate(act_ref, w_ref, x_ref, x_out_ref):
    @pl.when(pl.program_id(1) == 0)
    def _():
        x_out_ref[...] = x_ref[...]

    act = act_ref[...]
    step = 512
    for n in range(D_MODEL // step):
        cols = slice(n * step, (n + 1) * step)
        x_out_ref[:, cols] += _dot(act, w_ref[:, cols])


def _down_last_kernel(act_ref, w_ref, x_ref, x_out_ref):
    _down_accumulate(act_ref, w_ref, x_ref, x_out_ref)


def _down_norm_kernel(act_ref, w_ref, x_ref, ln_ref, wgk_ref, x_out_ref, h_ref, gk_ref):
    _down_accumulate(act_ref, w_ref, x_ref, x_out_ref)

    @pl.when(pl.program_id(1) == pl.num_programs(1) - 1)
    def _():
        h = _rms(x_out_ref[...], ln_ref[...]).astype(BF16)
        h_ref[...] = h
        gk_ref[...] = _dot_nt(h, wgk_ref[...].astype(BF16))


def _down(act, w_down, x, ln1, w_in_t, l, tm, with_norm):
    nk, r, tk = act.shape
    row = lambda i, k: (i, 0)
    in_specs = [
        pl.BlockSpec((None, tm, tk), lambda i, k: (k, i, 0)),
        pl.BlockSpec((None, tk, D_MODEL), lambda i, k: (l, k, 0)),
        pl.BlockSpec((tm, D_MODEL), row),
    ]
    out_specs = [pl.BlockSpec((tm, D_MODEL), row)]
    out_shape = [jax.ShapeDtypeStruct((r, D_MODEL), F32)]
    args = [act, w_down, x]
    if with_norm:
        in_specs += [_ln_spec(l + 1, 2), _wgk_spec(l + 1, 2)]
        out_specs += [pl.BlockSpec((tm, D_MODEL), row), pl.BlockSpec((tm, GATE_RANK), row)]
        out_shape += [jax.ShapeDtypeStruct((r, D_MODEL), BF16), jax.ShapeDtypeStruct((r, GATE_RANK), F32)]
        args += [ln1, w_in_t]
    out = pl.pallas_call(
        _down_norm_kernel if with_norm else _down_last_kernel,
        grid=(r // tm, nk),
        in_specs=in_specs,
        out_specs=out_specs,
        out_shape=out_shape,
        compiler_params=_cparams(("parallel", "arbitrary")),
        name="down_norm" if with_norm else "down_last",
    )(*args)
    return out if with_norm else (out[0], None, None)


def kernel(x_prompt, x_sample, cache_win_k, cache_win_v, state_gla, state_conv, rel_bias,
           ln1, w_in, ln_q, ln_k, sinks, w_gk2, b_gk, ln_o, w_oa, w_ob, w_out,
           ln2, w_up, w_conv, b_conv, w_down):
    bp, seq, _ = x_prompt.shape
    bs, t_new, _ = x_sample.shape
    tm_p = TM_PROMPT
    tm_s = bs * t_new
    tiles_per_seq = seq // tm_p

    ln1_3 = ln1.reshape(DEPTH, 1, D_MODEL)
    ln2_3 = ln2.reshape(DEPTH, 1, D_MODEL)
    w_in_t = jnp.swapaxes(w_in, 1, 2)
    w_oa_b, w_ob_b, w_out_b, w_down_b = (w.astype(BF16) for w in (w_oa, w_ob, w_out, w_down))
    w2_b = w_gk2.astype(BF16)
    b_conv3 = b_conv.reshape(DEPTH, 1, 2 * D_FF)
    lnq2 = jnp.tile(ln_q, (1, 2)).reshape(DEPTH, 1, LANES)
    lnk2 = jnp.tile(ln_k, (1, 2)).reshape(DEPTH, 1, LANES)

    bias_p = jnp.stack([_bias_table(rel_bias, jnp.asarray(_dmap_prompt(first))) for first in (True, False)])
    bias_s = _bias_table(rel_bias, jnp.asarray(_dmap_sample(t_new, SAMPLE_GROUP_A)))

    xp = x_prompt.reshape(bp * seq, D_MODEL)
    xs = x_sample.reshape(bs * t_new, D_MODEL)
    win_k = cache_win_k.reshape(DEPTH, bs, WINDOW, KV_WIDTH_A)
    win_v = cache_win_v.reshape(DEPTH, bs, WINDOW, KV_WIDTH_A)
    conv_state = state_conv.reshape(DEPTH, bs * (CONV_W - 1), 2 * D_FF)

    def dense_in(h, l, tm):
        proj = _proj(h, w_in_t, l, tm, TN_MAIN, 0, MAIN_COLS, False, F32, "proj_main")
        gates = _proj(h, w_in_t, l, tm, TN_GATES, OFF_GA, GATE_COLS, True, BF16, "proj_gates")
        return proj, gates

    hp, gkp = _norm(xp, ln1_3, w_in_t, 0, 512)
    hs, gks = _norm(xs, ln1_3, w_in_t, 0, tm_s)

    kp, vp, gp, cp = [], [], [], []
    ksm, vsm, gsm, csm = [], [], [], []
    for l in range(DEPTH):
        lq, lk, sk = lnq2[l], lnk2[l], sinks[l]
        w2_l, bgk_l, lno_l = w2_b[l], b_gk[l].reshape(1, WIDTH_BK), ln_o[l].reshape(1, DV_B)
        more = l + 1 < DEPTH

        proj, gates = dense_in(hp, l, tm_p)
        oa, kn = _attn_prompt(proj, sk, lq, lk, bias_p, bp, seq)
        ob, s_new = _gla_prompt(proj, gkp, w2_l, bgk_l, lno_l, bp, seq)
        xp, h2 = _merge(oa, ob, gates, xp, w_oa_b, w_ob_b, w_out_b, ln2_3, l, TM_MERGE)
        act, tail_v, tail_g = _ffn_up_prompt(h2, w_up, w_conv, b_conv3, l, tm_p, tiles_per_seq)
        xp, hp, gkp = _down(act, w_down_b, xp, ln1_3, w_in_t, l, tm_p, more)
        kp.append(kn.reshape(bp, seq, N_KV_A, HEAD_DIM)[:, -WINDOW:])
        vp.append(proj[:, OFF_VA:OFF_VA + KV_WIDTH_A].reshape(bp, seq, N_KV_A, HEAD_DIM)[:, -WINDOW:])
        gp.append(s_new)
        tails = jnp.concatenate([tail_v, tail_g], axis=-1).reshape(bp, tiles_per_seq, 8, 2 * D_FF)
        cp.append(tails[:, -1, 8 - (CONV_W - 1):])

        proj, gates = dense_in(hs, l, tm_s)
        oa, kn = _attn_sample(proj, win_k[l], win_v[l], sk, lq, lk, bias_s, bs, t_new)
        ob, s_new = _gla_sample(proj, gks, w2_l, bgk_l, lno_l, state_gla, l, bs, t_new)
        xs, h2 = _merge(oa, ob, gates, xs, w_oa_b, w_ob_b, w_out_b, ln2_3, l, tm_s)
        act, u_v, u_g = _ffn_up_sample(h2, w_up, w_conv, b_conv3, conv_state, l, t_new)
        xs, hs, gks = _down(act, w_down_b, xs, ln1_3, w_in_t, l, tm_s, more)
        k_new = kn.reshape(bs, t_new, N_KV_A, HEAD_DIM)
        v_new = proj[:, OFF_VA:OFF_VA + KV_WIDTH_A].reshape(bs, t_new, N_KV_A, HEAD_DIM)
        ksm.append(jnp.concatenate([cache_win_k[l][:, t_new:], k_new], axis=1))
        vsm.append(jnp.concatenate([cache_win_v[l][:, t_new:], v_new], axis=1))
        gsm.append(s_new)
        u = jnp.concatenate([u_v, u_g], axis=-1).reshape(bs, t_new, 2 * D_FF)
        csm.append(u[:, t_new - (CONV_W - 1):])

    return (xp.reshape(bp, seq, D_MODEL), xs.reshape(bs, t_new, D_MODEL),
            jnp.stack(kp), jnp.stack(vp), jnp.stack(gp), jnp.stack(cp),
            jnp.stack(ksm), jnp.stack(vsm), jnp.stack(gsm), jnp.stack(csm))
```

```python
import functools
import math

import numpy as np
import jax
import jax.numpy as jnp
from jax import lax
from jax.experimental import pallas as pl
from jax.experimental.pallas import tpu as pltpu

F32 = jnp.float32
BF16 = jnp.bfloat16

D_MODEL = 2048
DEPTH = 4
HEAD_DIM = 64
N_HEADS_A = 16
N_KV_A = 4
WIDTH_A = N_HEADS_A * HEAD_DIM
KV_WIDTH_A = N_KV_A * HEAD_DIM
WINDOW = 128
N_BUCKETS = 32
MAX_DISTANCE = 128
N_HEADS_B = 4
DK_B = 128
DV_B = 256
WIDTH_BK = N_HEADS_B * DK_B
WIDTH_BV = N_HEADS_B * DV_B
GATE_RANK = 16
GATE_TAU = 16.0
D_FF = 5632
CONV_W = 3
EPS = 1e-6

OFF_QA = 0
OFF_KA = OFF_QA + WIDTH_A
OFF_VA = OFF_KA + KV_WIDTH_A
OFF_QB = OFF_VA + KV_WIDTH_A
OFF_KB = OFF_QB + WIDTH_BK
OFF_VB = OFF_KB + WIDTH_BK
OFF_RB = OFF_VB + WIDTH_BV
OFF_GK = OFF_RB + WIDTH_BV
OFF_GA = OFF_GK + GATE_RANK
MAIN_COLS = OFF_GK
GATE_COLS = 2 * D_MODEL

LANES = 128
LOG2E = math.log2(math.e)
NEG_BIG = -1e30
VMEM_LIMIT = 56 * 1024 * 1024

TM_PROMPT = 1024
TM_FFN_UP = 1024
FFN_SUB_ROWS = 1024
TM_MERGE = 256
TN_MAIN = 1536
TN_GATES = 1024
ATTN_BLOCKS_PER_STEP = 2
TF_UP = 512
GLA_CHUNK_P = 128
GLA_TBLOCK = 512
SAMPLE_GROUP_A = 8
SAMPLE_GROUP_B = 4


def _cparams(sem):
    return pltpu.CompilerParams(dimension_semantics=sem, vmem_limit_bytes=VMEM_LIMIT)


def _dot(a, b):
    return jnp.dot(a, b, preferred_element_type=F32)


def _dot_nt(a, b):
    return lax.dot_general(a, b, (((1,), (1,)), ((), ())), preferred_element_type=F32)


def _dot_tn(a, b):
    return lax.dot_general(a, b, (((0,), (0,)), ((), ())), preferred_element_type=F32)


def _split3(x):
    hi = x.astype(BF16)
    r = x - hi.astype(F32)
    mid = r.astype(BF16)
    lo = (r - mid.astype(F32)).astype(BF16)
    return hi, mid, lo


def _dot_exact_lhs01(m01, x):
    hi, mid, lo = _split3(x)
    return _dot(m01, hi) + _dot(m01, mid) + _dot(m01, lo)


def _rms(x, g):
    ms = jnp.mean(x * x, axis=-1, keepdims=True)
    return x * lax.rsqrt(ms + EPS) * g


def _norm_kernel(x_ref, ln_ref, wgk_ref, h_ref, gk_ref):
    h = _rms(x_ref[...], ln_ref[...]).astype(BF16)
    h_ref[...] = h
    gk_ref[...] = _dot_nt(h, wgk_ref[...].astype(BF16))


def _ln_spec(l, nargs):
    if nargs == 1:
        return pl.BlockSpec((None, 1, D_MODEL), lambda i: (l, 0, 0))
    return pl.BlockSpec((None, 1, D_MODEL), lambda i, j: (l, 0, 0))


def _wgk_spec(l, nargs):
    blk = OFF_GK // GATE_RANK
    if nargs == 1:
        return pl.BlockSpec((None, GATE_RANK, D_MODEL), lambda i: (l, blk, 0))
    return pl.BlockSpec((None, GATE_RANK, D_MODEL), lambda i, j: (l, blk, 0))


def _norm(x, ln1, w_in_t, l, tm):
    r = x.shape[0]
    return pl.pallas_call(
        _norm_kernel,
        grid=(r // tm,),
        in_specs=[pl.BlockSpec((tm, D_MODEL), lambda i: (i, 0)), _ln_spec(l, 1), _wgk_spec(l, 1)],
        out_specs=[pl.BlockSpec((tm, D_MODEL), lambda i: (i, 0)), pl.BlockSpec((tm, GATE_RANK), lambda i: (i, 0))],
        out_shape=[jax.ShapeDtypeStruct((r, D_MODEL), BF16), jax.ShapeDtypeStruct((r, GATE_RANK), F32)],
        compiler_params=_cparams(("parallel",)),
        name="norm",
    )(x, ln1, w_in_t)


def _proj_kernel(sigmoid, h_ref, w_ref, o_ref, wb_scr):
    @pl.when(pl.program_id(1) == 0)
    def _():
        wb_scr[...] = w_ref[...].astype(BF16)

    y = _dot_nt(h_ref[...], wb_scr[...])
    if sigmoid:
        y = jax.nn.sigmoid(y)
    o_ref[...] = y.astype(o_ref.dtype)


def _proj(h, w_in_t, l, tm, tn, row_off, n_cols, sigmoid, out_dtype, name):
    r = h.shape[0]
    if row_off == 0:
        w_spec = pl.BlockSpec((None, tn, D_MODEL), lambda j, i: (l, j, 0))
    else:
        w_spec = pl.BlockSpec((pl.Squeezed(), pl.Element(tn), pl.Element(D_MODEL)),
                              lambda j, i: (l, pl.multiple_of(row_off + j * tn, GATE_RANK), 0))
    return pl.pallas_call(
        functools.partial(_proj_kernel, sigmoid),
        grid=(n_cols // tn, r // tm),
        in_specs=[pl.BlockSpec((tm, D_MODEL), lambda j, i: (i, 0)), w_spec],
        out_specs=pl.BlockSpec((tm, tn), lambda j, i: (i, j)),
        out_shape=jax.ShapeDtypeStruct((r, n_cols), out_dtype),
        scratch_shapes=[pltpu.VMEM((tn, D_MODEL), BF16)],
        compiler_params=_cparams(("arbitrary", "arbitrary")),
        name=name,
    )(h, w_in_t)


def _bucket_ranges():
    d = np.arange(WINDOW)
    max_exact = N_BUCKETS // 2
    df = np.maximum(d, 1).astype(np.float32)
    large = max_exact + (np.log(df / np.float32(max_exact)) / np.float32(math.log(MAX_DISTANCE / max_exact))
                         * np.float32(N_BUCKETS - max_exact)).astype(np.int32)
    large = np.minimum(large, N_BUCKETS - 1)
    bucket = np.where(d < max_exact, d, large)
    ranges = []
    for b in range(N_BUCKETS):
        idx = np.nonzero(bucket == b)[0]
        if idx.size:
            assert idx[-1] - idx[0] + 1 == idx.size
            ranges.append((b, int(idx[0]), int(idx[-1])))
    return ranges


def _bias_kernel(rel_ref, d_ref, o_ref):
    h = pl.program_id(0)
    d = d_ref[...]
    val = jnp.full(d.shape, NEG_BIG, F32)
    for b, lo, hi in _bucket_ranges():
        val = jnp.where((d >= lo) & (d <= hi), rel_ref[b, h] * LOG2E, val)
    o_ref[...] = val


def _bias_table(rel_bias, dmap):
    mq, nk = dmap.shape
    return pl.pallas_call(
        _bias_kernel,
        grid=(N_HEADS_A,),
        in_specs=[
            pl.BlockSpec(memory_space=pltpu.SMEM),
            pl.BlockSpec((mq, nk), lambda h: (0, 0)),
        ],
        out_specs=pl.BlockSpec((None, mq, nk), lambda h: (h, 0, 0)),
        out_shape=jax.ShapeDtypeStruct((N_HEADS_A, mq, nk), F32),
        compiler_params=_cparams(("parallel",)),
        name="bias_table",
    )(rel_bias, dmap)


def _dmap_prompt(first_block):
    i = np.arange(WINDOW)[:, None]
    j = np.arange(2 * WINDOW)[None, :]
    d = i + WINDOW - j
    ok = (d >= 0) & (d < WINDOW)
    if first_block:
        ok = ok & (j >= WINDOW)
    return np.where(ok, d, -1).astype(np.int32)


def _dmap_sample(t_new, group):
    nk = group * WINDOW + LANES
    rows = np.arange(group * t_new)
    rb, rt = rows // t_new, rows % t_new
    d = np.full((group * t_new, nk), -1, np.int64)
    cols = np.arange(group * WINDOW)
    cb, cs = cols // WINDOW, cols % WINDOW
    dw = rt[:, None] + WINDOW - cs[None, :]
    ok = (rb[:, None] == cb[None, :]) & (dw >= 0) & (dw < WINDOW)
    d[:, :group * WINDOW] = np.where(ok, dw, -1)
    ncols = np.arange(group * t_new)
    nb, nu = ncols // t_new, ncols % t_new
    dn = rt[:, None] - nu[None, :]
    okn = (rb[:, None] == nb[None, :]) & (dn >= 0)
    d[:, group * WINDOW:group * WINDOW + group * t_new] = np.where(okn, dn, -1)
    return d.astype(np.int32)


def _group_ones():
    r = lax.broadcasted_iota(jnp.int32, (LANES, LANES), 0)
    c = lax.broadcasted_iota(jnp.int32, (LANES, LANES), 1)
    low_c = jnp.where(c < HEAD_DIM, 1.0, 0.0)
    return jnp.where(r < HEAD_DIM, low_c, 1.0 - low_c).astype(BF16)


def _head_norm(x, gmat, ln2):
    sq = x * x
    hi = sq.astype(BF16)
    lo = (sq - hi.astype(F32)).astype(BF16)
    ms = (_dot(hi, gmat) + _dot(lo, gmat)) * (1.0 / HEAD_DIM)
    return x * lax.rsqrt(ms + EPS) * ln2


def _split_heads(x):
    lane = lax.broadcasted_iota(jnp.int32, (x.shape[0], LANES), 1)
    low = lane < HEAD_DIM
    lo_parts, hi_parts = [], []
    for c in range(KV_WIDTH_A // LANES):
        xc = x[:, c * LANES:(c + 1) * LANES]
        xr = pltpu.roll(xc, HEAD_DIM, 1)
        lo_parts += [jnp.where(low, xc, 0.0).astype(BF16), jnp.where(low, xr, 0.0).astype(BF16)]
        hi_parts += [jnp.where(low, 0.0, xr).astype(BF16), jnp.where(low, 0.0, xc).astype(BF16)]
    return lo_parts, hi_parts


def _attend(q_cols, k_parts, v_parts, bias_ref, sink_ref):
    mq = q_cols[0].shape[0]
    pairs_per_kv = N_HEADS_A // N_KV_A // 2
    assert pairs_per_kv == 2
    first_rows = lax.broadcasted_iota(jnp.int32, (pairs_per_kv * mq, 1), 0) < mq
    outs = [None] * (N_HEADS_A // 2)
    for kh in range(N_KV_A):
        pcs = [kh * pairs_per_kv + p for p in range(pairs_per_kv)]
        q_stack = jnp.concatenate([q_cols[pc] for pc in pcs], axis=0)
        for half in range(2):
            heads = [2 * pc + half for pc in pcs]
            bias = jnp.concatenate([bias_ref[h] for h in heads], axis=0)
            sink = jnp.where(first_rows, sink_ref[heads[0]], sink_ref[heads[1]]) * LOG2E
            s = _dot_nt(q_stack, k_parts[half][kh]) + bias
            m = jnp.maximum(jnp.max(s, axis=-1, keepdims=True), sink)
            e = jnp.exp2(s - m)
            den = jnp.sum(e, axis=-1, keepdims=True) + jnp.exp2(sink - m)
            o = _dot(e.astype(BF16), v_parts[half][kh]) * (1.0 / den)
            for p, pc in enumerate(pcs):
                o_p = o[p * mq:(p + 1) * mq, :]
                outs[pc] = o_p if outs[pc] is None else outs[pc] + o_p
    return outs


def _attn_prompt_kernel(sink_ref, q_ref, ko_ref, vo_ref, lnq_ref, lnk_ref, bias_ref,
                        o_ref, kn_ref, kprev_scr, vprev_scr):
    n = pl.program_id(1)
    gmat = _group_ones()
    lnq = lnq_ref[...]
    lnk = lnk_ref[...]
    scale = HEAD_DIM ** -0.5 * LOG2E

    @pl.when(n == 0)
    def _():
        kprev_scr[...] = jnp.zeros_like(kprev_scr)
        vprev_scr[...] = jnp.zeros_like(vprev_scr)

    kn_own = jnp.concatenate(
        [_head_norm(ko_ref[:, c * LANES:(c + 1) * LANES], gmat, lnk) for c in range(KV_WIDTH_A // LANES)], axis=1)
    kn_ref[...] = kn_own
    k_own = _split_heads(kn_own)
    v_own = _split_heads(vo_ref[...])
    k_parts = [[jnp.concatenate([kprev_scr[half, kh], k_own[half][kh]], axis=0) for kh in range(N_KV_A)]
               for half in range(2)]
    v_parts = [[jnp.concatenate([vprev_scr[half, kh], v_own[half][kh]], axis=0) for kh in range(N_KV_A)]
               for half in range(2)]
    q_cols = [(_head_norm(q_ref[:, c * LANES:(c + 1) * LANES], gmat, lnq) * scale).astype(BF16)
              for c in range(WIDTH_A // LANES)]
    outs = _attend(q_cols, k_parts, v_parts, bias_ref, sink_ref)
    for c, o in enumerate(outs):
        o_ref[:, c * LANES:(c + 1) * LANES] = o.astype(o_ref.dtype)
    for half in range(2):
        for kh in range(N_KV_A):
            kprev_scr[half, kh] = k_own[half][kh]
            vprev_scr[half, kh] = v_own[half][kh]


def _gates_attn_kernel(blocks_per_seq, sink_ref, h_ref, w_ref, q_ref, k_ref, v_ref, lnq_ref, lnk_ref, bias_ref,
                       g_ref, o_ref, kn_ref, wb_scr, kprev_scr, vprev_scr, o_scr, kn_scr):
    j = pl.program_id(0)
    i = pl.program_id(1)
    step = j * pl.num_programs(1) + i

    @pl.when(i == 0)
    def _():
        wb_scr[...] = w_ref[...].astype(BF16)

    @pl.when(step == 0)
    def _():
        kprev_scr[...] = jnp.zeros_like(kprev_scr)
        vprev_scr[...] = jnp.zeros_like(vprev_scr)

    gmat = _group_ones()
    lnq = lnq_ref[...]
    lnk = lnk_ref[...]
    scale = HEAD_DIM ** -0.5 * LOG2E
    for t in range(ATTN_BLOCKS_PER_STEP):
        rows = slice(t * WINDOW, (t + 1) * WINDOW)
        block = step * ATTN_BLOCKS_PER_STEP + t
        table = bias_ref.at[jnp.where(block % blocks_per_seq == 0, 0, 1)]
        kn_own = jnp.concatenate(
            [_head_norm(k_ref[rows, c * LANES:(c + 1) * LANES], gmat, lnk) for c in range(KV_WIDTH_A // LANES)],
            axis=1)
        kn_scr[rows, :] = kn_own
        k_own = _split_heads(kn_own)
        v_own = _split_heads(v_ref[rows, :])
        k_parts = [[jnp.concatenate([kprev_scr[half, kh], k_own[half][kh]], axis=0) for kh in range(N_KV_A)]
                   for half in range(2)]
        v_parts = [[jnp.concatenate([vprev_scr[half, kh], v_own[half][kh]], axis=0) for kh in range(N_KV_A)]
                   for half in range(2)]
        q_cols = [(_head_norm(q_ref[rows, c * LANES:(c + 1) * LANES], gmat, lnq) * scale).astype(BF16)
                  for c in range(WIDTH_A // LANES)]
        outs = _attend(q_cols, k_parts, v_parts, table, sink_ref)
        for c, o in enumerate(outs):
            o_scr[rows, c * LANES:(c + 1) * LANES] = o.astype(o_scr.dtype)
        for half in range(2):
            for kh in range(N_KV_A):
                kprev_scr[half, kh] = k_own[half][kh]
                vprev_scr[half, kh] = v_own[half][kh]

    g_ref[...] = jax.nn.sigmoid(_dot_nt(h_ref[...], wb_scr[...])).astype(g_ref.dtype)
    o_ref[...] = o_scr[...]
    kn_ref[...] = kn_scr[...]


def _gates_attn(h, w_in_t, proj, sinks_l, lnq2, lnk2, bias_p, l, tm, seq):
    r = h.shape[0]
    tn = TN_GATES
    nj, nt = GATE_COLS // tn, r // tm
    rows = ATTN_BLOCKS_PER_STEP * WINDOW
    assert nj * nt * rows == r
    kcol = OFF_KA // KV_WIDTH_A
    vcol = OFF_VA // KV_WIDTH_A
    step = lambda j, i: j * nt + i
    return pl.pallas_call(
        functools.partial(_gates_attn_kernel, seq // WINDOW),
        grid=(nj, nt),
        in_specs=[
            pl.BlockSpec(memory_space=pltpu.SMEM),
            pl.BlockSpec((tm, D_MODEL), lambda j, i: (i, 0)),
            pl.BlockSpec((pl.Squeezed(), pl.Element(tn), pl.Element(D_MODEL)),
                         lambda j, i: (l, pl.multiple_of(OFF_GA + j * tn, GATE_RANK), 0)),
            pl.BlockSpec((rows, WIDTH_A), lambda j, i: (step(j, i), 0)),
            pl.BlockSpec((rows, KV_WIDTH_A), lambda j, i: (step(j, i), kcol)),
            pl.BlockSpec((rows, KV_WIDTH_A), lambda j, i: (step(j, i), vcol)),
            pl.BlockSpec((1, LANES), lambda j, i: (0, 0)),
            pl.BlockSpec((1, LANES), lambda j, i: (0, 0)),
            pl.BlockSpec((2, N_HEADS_A, WINDOW, 2 * WINDOW), lambda j, i: (0, 0, 0, 0)),
        ],
        out_specs=[
            pl.BlockSpec((tm, tn), lambda j, i: (i, j)),
            pl.BlockSpec((rows, WIDTH_A), lambda j, i: (step(j, i), 0)),
            pl.BlockSpec((rows, KV_WIDTH_A), lambda j, i: (step(j, i), 0)),
        ],
        out_shape=[
            jax.ShapeDtypeStruct((r, GATE_COLS), BF16),
            jax.ShapeDtypeStruct((r, WIDTH_A), BF16),
            jax.ShapeDtypeStruct((r, KV_WIDTH_A), F32),
        ],
        scratch_shapes=[
            pltpu.VMEM((tn, D_MODEL), BF16),
            pltpu.VMEM((2, N_KV_A, WINDOW, LANES), BF16),
            pltpu.VMEM((2, N_KV_A, WINDOW, LANES), BF16),
            pltpu.VMEM((rows, WIDTH_A), BF16),
            pltpu.VMEM((rows, KV_WIDTH_A), F32),
        ],
        compiler_params=_cparams(("arbitrary", "arbitrary")),
        name="gates_attn",
    )(sinks_l, h, w_in_t, proj, proj, proj, lnq2, lnk2, bias_p)


def _attn_prompt(proj, sinks_l, lnq2, lnk2, bias_p, batch, seq):
    nb = seq // WINDOW
    r = batch * seq
    kcol = OFF_KA // KV_WIDTH_A
    vcol = OFF_VA // KV_WIDTH_A
    own = lambda b, n: (b * nb + n, 0)
    return pl.pallas_call(
        _attn_prompt_kernel,
        grid=(batch, nb),
        in_specs=[
            pl.BlockSpec(memory_space=pltpu.SMEM),
            pl.BlockSpec((WINDOW, WIDTH_A), own),
            pl.BlockSpec((WINDOW, KV_WIDTH_A), lambda b, n: (b * nb + n, kcol)),
            pl.BlockSpec((WINDOW, KV_WIDTH_A), lambda b, n: (b * nb + n, vcol)),
            pl.BlockSpec((1, LANES), lambda b, n: (0, 0)),
            pl.BlockSpec((1, LANES), lambda b, n: (0, 0)),
            pl.BlockSpec((None, N_HEADS_A, WINDOW, 2 * WINDOW), lambda b, n: (jnp.minimum(n, 1), 0, 0, 0)),
        ],
        out_specs=[
            pl.BlockSpec((WINDOW, WIDTH_A), own),
            pl.BlockSpec((WINDOW, KV_WIDTH_A), own),
        ],
        out_shape=[
            jax.ShapeDtypeStruct((r, WIDTH_A), BF16),
            jax.ShapeDtypeStruct((r, KV_WIDTH_A), F32),
        ],
        scratch_shapes=[pltpu.VMEM((2, N_KV_A, WINDOW, LANES), BF16), pltpu.VMEM((2, N_KV_A, WINDOW, LANES), BF16)],
        compiler_params=_cparams(("parallel", "arbitrary")),
        name="attn_prompt",
    )(sinks_l, proj, proj, proj, lnq2, lnk2, bias_p)


def _attn_sample_kernel(sink_ref, q_ref, kn_in_ref, vn_ref, wk_ref, wv_ref, lnq_ref, lnk_ref, bias_ref,
                        o_ref, kn_ref):
    gmat = _group_ones()
    lnq = lnq_ref[...]
    lnk = lnk_ref[...]
    scale = HEAD_DIM ** -0.5 * LOG2E
    rows = q_ref.shape[0]
    g = wk_ref.shape[0]
    kn_new = jnp.concatenate(
        [_head_norm(kn_in_ref[:, c * LANES:(c + 1) * LANES], gmat, lnk) for c in range(KV_WIDTH_A // LANES)], axis=1)
    kn_ref[...] = kn_new
    pad = jnp.zeros((LANES - rows, KV_WIDTH_A), F32)
    k = jnp.concatenate([wk_ref[...].reshape(g * WINDOW, KV_WIDTH_A), kn_new, pad], axis=0)
    v = jnp.concatenate([wv_ref[...].reshape(g * WINDOW, KV_WIDTH_A), vn_ref[...], pad], axis=0)
    q_cols = [(_head_norm(q_ref[:, c * LANES:(c + 1) * LANES], gmat, lnq) * scale).astype(BF16)
              for c in range(WIDTH_A // LANES)]
    outs = _attend(q_cols, _split_heads(k), _split_heads(v), bias_ref, sink_ref)
    for c, o in enumerate(outs):
        o_ref[:, c * LANES:(c + 1) * LANES] = o.astype(o_ref.dtype)


def _attn_sample(proj, win_k, win_v, sinks_l, lnq2, lnk2, bias_s, batch, t_new):
    g = SAMPLE_GROUP_A
    rows = g * t_new
    nk = g * WINDOW + LANES
    kcol = OFF_KA // KV_WIDTH_A
    vcol = OFF_VA // KV_WIDTH_A
    return pl.pallas_call(
        _attn_sample_kernel,
        grid=(batch // g,),
        in_specs=[
            pl.BlockSpec(memory_space=pltpu.SMEM),
            pl.BlockSpec((rows, WIDTH_A), lambda i: (i, 0)),
            pl.BlockSpec((rows, KV_WIDTH_A), lambda i: (i, kcol)),
            pl.BlockSpec((rows, KV_WIDTH_A), lambda i: (i, vcol)),
            pl.BlockSpec((g, WINDOW, KV_WIDTH_A), lambda i: (i, 0, 0)),
            pl.BlockSpec((g, WINDOW, KV_WIDTH_A), lambda i: (i, 0, 0)),
            pl.BlockSpec((1, LANES), lambda i: (0, 0)),
            pl.BlockSpec((1, LANES), lambda i: (0, 0)),
            pl.BlockSpec((N_HEADS_A, rows, nk), lambda i: (0, 0, 0)),
        ],
        out_specs=[
            pl.BlockSpec((rows, WIDTH_A), lambda i: (i, 0)),
            pl.BlockSpec((rows, KV_WIDTH_A), lambda i: (i, 0)),
        ],
        out_shape=[
            jax.ShapeDtypeStruct((batch * t_new, WIDTH_A), BF16),
            jax.ShapeDtypeStruct((batch * t_new, KV_WIDTH_A), F32),
        ],
        compiler_params=_cparams(("parallel",)),
        name="attn_sample",
    )(sinks_l, proj, proj, proj, win_k, win_v, lnq2, lnk2, bias_s)


def _log_decay(gk, w2_ref, bgk_ref):
    x = _dot(gk.astype(BF16), w2_ref[...]) + bgk_ref[...]
    log_sig = jnp.minimum(x, 0.0) - jnp.log(1.0 + jnp.exp(-jnp.abs(x)))
    return log_sig * (1.0 / GATE_TAU)


def _column(row):
    r = lax.broadcasted_iota(jnp.int32, (LANES, LANES), 0)
    c = lax.broadcasted_iota(jnp.int32, (LANES, LANES), 1)
    return jnp.sum(jnp.where(r == c, row, 0.0), axis=1, keepdims=True)


def _gla_out(o, ln_o, rgate):
    return (_rms(o, ln_o) * (rgate * jax.nn.sigmoid(rgate))).astype(BF16)


def _gla_prompt_kernel(q_ref, k_ref, va_ref, vb_ref, ra_ref, rb_ref, gk_ref, w2_ref, bgk_ref, lno_ref,
                       o_ref, s_out_ref, s_scr, b_scr):
    t = pl.program_id(1)
    c_len = GLA_CHUNK_P
    tb = q_ref.shape[0]
    n_chunks = tb // c_len
    half = c_len // 2
    scale = DK_B ** -0.5

    @pl.when(t == 0)
    def _():
        s_scr[...] = jnp.zeros_like(s_scr)

    shift = c_len.bit_length() - 1
    rb = lax.broadcasted_iota(jnp.int32, (tb, tb), 0)
    cb = lax.broadcasted_iota(jnp.int32, (tb, tb), 1)
    same_chunk = lax.shift_right_logical(rb, shift) == lax.shift_right_logical(cb, shift)
    tri_blocks = jnp.where(same_chunk & (rb >= cb), 1.0, 0.0).astype(BF16)
    b_scr[...] = _dot_exact_lhs01(tri_blocks, _log_decay(gk_ref[...], w2_ref, bgk_ref))

    ri = lax.broadcasted_iota(jnp.int32, (c_len, c_len), 0)
    ci = lax.broadcasted_iota(jnp.int32, (c_len, c_len), 1)
    quarter = half // 2
    same_half = jnp.where(ci < half, jnp.where(ri < half, 1, 0), jnp.where(ri < half, 0, 1))
    own_causal = (ri >= ci) & (same_half == 1)
    upper_row = lax.broadcasted_iota(jnp.int32, (c_len, 1), 0) >= half
    ln_o = lno_ref[...]
    v_refs = (va_ref, vb_ref)
    r_refs = (ra_ref, rb_ref)

    def chunk(c, carry):
        rows = pl.ds(pl.multiple_of(c * c_len, c_len), c_len)
        for h in range(N_HEADS_B):
            ks = slice(h * DK_B, (h + 1) * DK_B)
            vs = slice((h % 2) * DV_B, (h % 2 + 1) * DV_B)
            bh = b_scr[rows, ks]
            b_last = bh[c_len - 1:c_len, :]
            b_half = bh[half - 1:half, :]
            q = q_ref[rows, ks] * scale
            k = k_ref[rows, ks]
            v = v_refs[h // 2][rows, vs].astype(BF16)
            q_inter = (q * jnp.exp(bh)).astype(BF16)
            r_own = jnp.where(upper_row, bh[half + quarter - 1:half + quarter, :], bh[quarter - 1:quarter, :])
            a_own = _dot_nt((q * jnp.exp(bh - r_own)).astype(BF16), (k * jnp.exp(r_own - bh)).astype(BF16))
            q_x = jnp.where(upper_row, q * jnp.exp(bh - b_half), 0.0).astype(BF16)
            k_x = jnp.where(upper_row, 0.0, k * jnp.exp(b_half - bh)).astype(BF16)
            a = (jnp.where(own_causal, a_own, 0.0) + _dot_nt(q_x, k_x)).astype(BF16)
            k_state = (k * jnp.exp(b_last - bh)).astype(BF16)
            s_old = s_scr[h]
            o = _dot(a, v) + _dot(q_inter, s_old.astype(BF16))
            s_scr[h] = _column(jnp.exp(b_last)) * s_old + _dot_tn(k_state, v)
            o_ref[rows, h * DV_B:(h + 1) * DV_B] = _gla_out(o, ln_o, r_refs[h // 2][rows, vs])
        return carry

    lax.fori_loop(0, n_chunks, chunk, 0, unroll=4)

    @pl.when(t == pl.num_programs(1) - 1)
    def _():
        s_out_ref[...] = s_scr[...]


def _gla_prompt(proj, gk, w2_l, bgk_l, lno_l, batch, seq):
    tb = GLA_TBLOCK
    nt = seq // tb
    w = 2 * DV_B
    row = lambda b, t: b * nt + t
    spec = lambda col: pl.BlockSpec((tb, w), lambda b, t: (row(b, t), col))
    return pl.pallas_call(
        _gla_prompt_kernel,
        grid=(batch, nt),
        in_specs=[
            spec(OFF_QB // w), spec(OFF_KB // w),
            spec(OFF_VB // w), spec(OFF_VB // w + 1),
            spec(OFF_RB // w), spec(OFF_RB // w + 1),
            pl.BlockSpec((tb, GATE_RANK), lambda b, t: (row(b, t), 0)),
            pl.BlockSpec((GATE_RANK, WIDTH_BK), lambda b, t: (0, 0)),
            pl.BlockSpec((1, WIDTH_BK), lambda b, t: (0, 0)),
            pl.BlockSpec((1, DV_B), lambda b, t: (0, 0)),
        ],
        out_specs=[
            pl.BlockSpec((tb, WIDTH_BV), lambda b, t: (row(b, t), 0)),
            pl.BlockSpec((None, N_HEADS_B, DK_B, DV_B), lambda b, t: (b, 0, 0, 0)),
        ],
        out_shape=[
            jax.ShapeDtypeStruct((batch * seq, WIDTH_BV), BF16),
            jax.ShapeDtypeStruct((batch, N_HEADS_B, DK_B, DV_B), F32),
        ],
        scratch_shapes=[pltpu.VMEM((N_HEADS_B, DK_B, DV_B), F32), pltpu.VMEM((tb, WIDTH_BK), F32)],
        compiler_params=_cparams(("parallel", "arbitrary")),
        name="gla_prompt",
    )(proj, proj, proj, proj, proj, proj, gk, w2_l, bgk_l, lno_l)


def _gla_sample_kernel(t_new, q_ref, k_ref, va_ref, vb_ref, ra_ref, rb_ref, gk_ref, w2_ref, bgk_ref, lno_ref,
                       s_ref, o_ref, s_out_ref):
    rows = q_ref.shape[0]
    n_seq = rows // t_new
    scale = DK_B ** -0.5
    ri = lax.broadcasted_iota(jnp.int32, (rows, rows), 0)
    ci = lax.broadcasted_iota(jnp.int32, (rows, rows), 1)
    same = None
    for sq in range(n_seq):
        lo, hi = sq * t_new, (sq + 1) * t_new
        blk = (ri >= lo) & (ri < hi) & (ci >= lo) & (ci < hi)
        same = blk if same is None else (same | blk)
    causal = same & (ri >= ci)
    tri = jnp.where(causal, 1.0, 0.0).astype(BF16)
    rcol = lax.broadcasted_iota(jnp.int32, (rows, 1), 0)
    ln_o = lno_ref[...]
    v_refs = (va_ref, vb_ref)
    r_refs = (ra_ref, rb_ref)

    g = _log_decay(gk_ref[...], w2_ref, bgk_ref)
    b = _dot_exact_lhs01(tri, g)
    for h in range(N_HEADS_B):
        ks = slice(h * DK_B, (h + 1) * DK_B)
        vs = slice((h % 2) * DV_B, (h % 2 + 1) * DV_B)
        bh = b[:, ks]
        gh = g[:, ks]
        q = q_ref[:, ks] * scale
        k = k_ref[:, ks]
        v = v_refs[h // 2][:, vs].astype(BF16)
        q_inter = q * jnp.exp(bh)
        k_t = (k * jnp.exp(-bh)).astype(BF16)
        a = jnp.where(causal, _dot_nt(q_inter.astype(BF16), k_t), 0.0).astype(BF16)
        o = _dot(a, v)
        for sq in range(n_seq):
            mine = (rcol >= sq * t_new) & (rcol < (sq + 1) * t_new)
            b_last = jnp.sum(jnp.where(mine, gh, 0.0), axis=0, keepdims=True)
            k_state = jnp.where(mine, k * jnp.exp(b_last - bh), 0.0).astype(BF16)
            q_mine = jnp.where(mine, q_inter, 0.0).astype(BF16)
            s_old = s_ref[sq, h]
            o = o + _dot(q_mine, s_old.astype(BF16))
            s_out_ref[sq, h] = _column(jnp.exp(b_last)) * s_old + _dot_tn(k_state, v)
        o_ref[:, h * DV_B:(h + 1) * DV_B] = _gla_out(o, ln_o, r_refs[h // 2][:, vs])


def _gla_sample(proj, gk, w2_l, bgk_l, lno_l, state, l, batch, t_new):
    g = SAMPLE_GROUP_B
    rows = g * t_new
    w = 2 * DV_B
    spec = lambda col: pl.BlockSpec((rows, w), lambda i: (i, col))
    return pl.pallas_call(
        functools.partial(_gla_sample_kernel, t_new),
        grid=(batch // g,),
        in_specs=[
            spec(OFF_QB // w), spec(OFF_KB // w),
            spec(OFF_VB // w), spec(OFF_VB // w + 1),
            spec(OFF_RB // w), spec(OFF_RB // w + 1),
            pl.BlockSpec((rows, GATE_RANK), lambda i: (i, 0)),
            pl.BlockSpec((GATE_RANK, WIDTH_BK), lambda i: (0, 0)),
            pl.BlockSpec((1, WIDTH_BK), lambda i: (0, 0)),
            pl.BlockSpec((1, DV_B), lambda i: (0, 0)),
            pl.BlockSpec((None, g, N_HEADS_B, DK_B, DV_B), lambda i: (l, i, 0, 0, 0)),
        ],
        out_specs=[
            pl.BlockSpec((rows, WIDTH_BV), lambda i: (i, 0)),
            pl.BlockSpec((g, N_HEADS_B, DK_B, DV_B), lambda i: (i, 0, 0, 0)),
        ],
        out_shape=[
            jax.ShapeDtypeStruct((batch * t_new, WIDTH_BV), BF16),
            jax.ShapeDtypeStruct((batch, N_HEADS_B, DK_B, DV_B), F32),
        ],
        compiler_params=_cparams(("parallel",)),
        name="gla_sample",
    )(proj, proj, proj, proj, proj, proj, gk, w2_l, bgk_l, lno_l, state)


def _merge_kernel(oa_ref, ob_ref, sa_ref, sb_ref, x_ref, woa_ref, wob_ref, wout_ref, ln_ref, x_out_ref, h_ref):
    a = _dot(oa_ref[...], woa_ref[...])
    b = _dot(ob_ref[...], wob_ref[...])
    mix = (sa_ref[...].astype(F32) * a + sb_ref[...].astype(F32) * b).astype(BF16)
    x_new = x_ref[...] + _dot(mix, wout_ref[...])
    x_out_ref[...] = x_new
    h_ref[...] = _rms(x_new, ln_ref[...]).astype(BF16)


def _merge(oa, ob, gates, x, w_oa, w_ob, w_out, ln2, l, tm):
    r = x.shape[0]
    resident = dict(pipeline_mode=pl.Buffered(1))
    row = lambda i: (i, 0)
    return pl.pallas_call(
        _merge_kernel,
        grid=(r // tm,),
        in_specs=[
            pl.BlockSpec((tm, WIDTH_A), row),
            pl.BlockSpec((tm, WIDTH_BV), row),
            pl.BlockSpec((tm, D_MODEL), lambda i: (i, 0)),
            pl.BlockSpec((tm, D_MODEL), lambda i: (i, 1)),
            pl.BlockSpec((tm, D_MODEL), row),
            pl.BlockSpec((None, WIDTH_A, D_MODEL), lambda i: (l, 0, 0), **resident),
            pl.BlockSpec((None, WIDTH_BV, D_MODEL), lambda i: (l, 0, 0), **resident),
            pl.BlockSpec((None, D_MODEL, D_MODEL), lambda i: (l, 0, 0), **resident),
            _ln_spec(l, 1),
        ],
        out_specs=[pl.BlockSpec((tm, D_MODEL), row), pl.BlockSpec((tm, D_MODEL), row)],
        out_shape=[jax.ShapeDtypeStruct((r, D_MODEL), F32), jax.ShapeDtypeStruct((r, D_MODEL), BF16)],
        compiler_params=_cparams(("parallel",)),
        name="merge",
    )(oa, ob, gates, gates, x, w_oa, w_ob, w_out, ln2)


def _gelu_tanh(x):
    return x * (0.5 * (1.0 + jnp.tanh(math.sqrt(2.0 / math.pi) * (x + 0.044715 * (x * x * x)))))


def _conv3(u, p1, p2, w_ref, b_ref):
    return w_ref[0:1, :] * p2 + w_ref[1:2, :] * p1 + w_ref[2:3, :] * u + b_ref[...]


def _history_selectors(n_seq, t_new):
    hist = CONV_W - 1
    e1 = np.zeros((n_seq * t_new, n_seq * hist), np.float32)
    e2 = np.zeros((n_seq * t_new, n_seq * hist), np.float32)
    for b in range(n_seq):
        for t in range(min(t_new, hist)):
            if t < 1:
                e1[b * t_new + t, b * hist + hist - 1 + t] = 1.0
            e2[b * t_new + t, b * hist + t] = 1.0
    return jnp.asarray(e1, BF16), jnp.asarray(e2, BF16)


def _ffn_up_prompt_kernel(tiles_per_seq, h_ref, wv_ref, wg_ref, cv_ref, cg_ref, bv_ref, bg_ref,
                          act_ref, tv_ref, tg_ref, wvb_scr, wgb_scr, carry_v, carry_g):
    i = pl.program_id(1)

    @pl.when(i == 0)
    def _():
        wvb_scr[...] = wv_ref[...].astype(BF16)
        wgb_scr[...] = wg_ref[...].astype(BF16)

    sub = FFN_SUB_ROWS
    first = (i % tiles_per_seq) == 0
    row8 = lax.broadcasted_iota(jnp.int32, (8, 1), 0)

    def branch(h, wb_scr, prev, c_ref, b_ref):
        u = _dot(h, wb_scr[...])
        p1 = pltpu.roll(u, 1, 0)
        p2 = pltpu.roll(u, 2, 0)
        body = _conv3(u, p1, p2, c_ref, b_ref)
        p1_top = jnp.where(row8 < 1, pltpu.roll(prev, 1, 0), p1[0:8, :])
        p2_top = jnp.where(row8 < 2, pltpu.roll(prev, 2, 0), p2[0:8, :])
        top = _conv3(u[0:8, :], p1_top, p2_top, c_ref, b_ref)
        return jnp.concatenate([top, body[8:, :]], axis=0), u[sub - 8:sub, :]

    prev_g = jnp.where(first, 0.0, carry_g[...])
    prev_v = jnp.where(first, 0.0, carry_v[...])
    for rb in range(h_ref.shape[0] // sub):
        rows = slice(rb * sub, (rb + 1) * sub)
        h = h_ref[rows, :]
        gate, prev_g = branch(h, wgb_scr, prev_g, cg_ref, bg_ref)
        gate = _gelu_tanh(gate)
        val, prev_v = branch(h, wvb_scr, prev_v, cv_ref, bv_ref)
        act_ref[rows, :] = (gate * val).astype(act_ref.dtype)
    carry_g[...] = prev_g
    carry_v[...] = prev_v
    tg_ref[...] = prev_g
    tv_ref[...] = prev_v


def _ffn_up_sample_kernel(pos_ref, e1_ref, e2_ref, h_ref, wv_ref, wg_ref, cv_ref, cg_ref, bv_ref, bg_ref,
                          sv_ref, sg_ref, act_ref, uv_ref, ug_ref):
    h = h_ref[...]
    pos = pos_ref[...]
    e1 = e1_ref[...]
    e2 = e2_ref[...]

    def branch(w_ref, c_ref, b_ref, s_ref, u_ref):
        u = _dot(h, w_ref[...].astype(BF16))
        u_ref[...] = u
        cached = s_ref[...]
        p1 = jnp.where(pos >= 1, pltpu.roll(u, 1, 0), _dot_exact_lhs01(e1, cached))
        p2 = jnp.where(pos >= 2, pltpu.roll(u, 2, 0), _dot_exact_lhs01(e2, cached))
        return _conv3(u, p1, p2, c_ref, b_ref)

    val = branch(wv_ref, cv_ref, bv_ref, sv_ref, uv_ref)
    gate = branch(wg_ref, cg_ref, bg_ref, sg_ref, ug_ref)
    act_ref[...] = (_gelu_tanh(gate) * val).astype(act_ref.dtype)


def _ffn_up_prompt(h, w_up, w_conv, b_conv, l, tm, tiles_per_seq):
    r = h.shape[0]
    tf = TF_UP
    nt = r // tm
    nj = D_FF // tf
    w_spec = lambda off: pl.BlockSpec((None, D_MODEL, tf), lambda j, i: (l, 0, j + off))
    c_spec = lambda off: pl.BlockSpec((None, CONV_W, tf), lambda j, i: (l, 0, j + off))
    b_spec = lambda off: pl.BlockSpec((None, 1, tf), lambda j, i: (l, 0, j + off))
    tail_spec = pl.BlockSpec((None, 8, tf), lambda j, i: (i, 0, j))
    return pl.pallas_call(
        functools.partial(_ffn_up_prompt_kernel, tiles_per_seq),
        grid=(nj, nt),
        in_specs=[
            pl.BlockSpec((tm, D_MODEL), lambda j, i: (i, 0)),
            w_spec(0), w_spec(nj), c_spec(0), c_spec(nj), b_spec(0), b_spec(nj),
        ],
        out_specs=[pl.BlockSpec((None, tm, tf), lambda j, i: (j, i, 0)), tail_spec, tail_spec],
        out_shape=[
            jax.ShapeDtypeStruct((nj, r, tf), BF16),
            jax.ShapeDtypeStruct((nt, 8, D_FF), F32),
            jax.ShapeDtypeStruct((nt, 8, D_FF), F32),
        ],
        scratch_shapes=[
            pltpu.VMEM((D_MODEL, tf), BF16),
            pltpu.VMEM((D_MODEL, tf), BF16),
            pltpu.VMEM((8, tf), F32),
            pltpu.VMEM((8, tf), F32),
        ],
        compiler_params=_cparams(("arbitrary", "arbitrary")),
        name="ffn_up_prompt",
    )(h, w_up, w_up, w_conv, w_conv, b_conv, b_conv)


def _ffn_up_sample(h, w_up, w_conv, b_conv, conv_state, l, t_new):
    r = h.shape[0]
    tf = TF_UP
    nj = D_FF // tf
    ns = conv_state.shape[1]
    w_spec = lambda off: pl.BlockSpec((None, D_MODEL, tf), lambda j: (l, 0, j + off))
    c_spec = lambda off: pl.BlockSpec((None, CONV_W, tf), lambda j: (l, 0, j + off))
    b_spec = lambda off: pl.BlockSpec((None, 1, tf), lambda j: (l, 0, j + off))
    s_spec = lambda off: pl.BlockSpec((None, ns, tf), lambda j: (l, 0, j + off))
    u_spec = pl.BlockSpec((r, tf), lambda j: (0, j))
    pos = jnp.asarray((np.arange(r) % t_new).astype(np.int32).reshape(r, 1))
    e1, e2 = _history_selectors(r // t_new, t_new)
    return pl.pallas_call(
        _ffn_up_sample_kernel,
        grid=(nj,),
        in_specs=[
            pl.BlockSpec((r, 1), lambda j: (0, 0)),
            pl.BlockSpec((r, ns), lambda j: (0, 0)),
            pl.BlockSpec((r, ns), lambda j: (0, 0)),
            pl.BlockSpec((r, D_MODEL), lambda j: (0, 0)),
            w_spec(0), w_spec(nj), c_spec(0), c_spec(nj), b_spec(0), b_spec(nj),
            s_spec(0), s_spec(nj),
        ],
        out_specs=[pl.BlockSpec((None, r, tf), lambda j: (j, 0, 0)), u_spec, u_spec],
        out_shape=[
            jax.ShapeDtypeStruct((nj, r, tf), BF16),
            jax.ShapeDtypeStruct((r, D_FF), F32),
            jax.ShapeDtypeStruct((r, D_FF), F32),
        ],
        compiler_params=_cparams(("arbitrary",)),
        name="ffn_up_sample",
    )(pos, e1, e2, h, w_up, w_up, w_conv, w_conv, b_conv, b_conv, conv_state, conv_state)


def _down_accumulate(act_ref, w_ref, x_ref, x_out_ref):
    @pl.when(pl.program_id(1) == 0)
    def _():
        x_out_ref[...] = x_ref[...]

    act = act_ref[...]
    step = 512
    for n in range(D_MODEL // step):
        cols = slice(n * step, (n + 1) * step)
        x_out_ref[:, cols] += _dot(act, w_ref[:, cols])


def _down_last_kernel(act_ref, w_ref, x_ref, x_out_ref):
    _down_accumulate(act_ref, w_ref, x_ref, x_out_ref)


def _down_norm_kernel(act_ref, w_ref, x_ref, ln_ref, wgk_ref, x_out_ref, h_ref, gk_ref):
    _down_accumulate(act_ref, w_ref, x_ref, x_out_ref)

    @pl.when(pl.program_id(1) == pl.num_programs(1) - 1)
    def _():
        h = _rms(x_out_ref[...], ln_ref[...]).astype(BF16)
        h_ref[...] = h
        gk_ref[...] = _dot_nt(h, wgk_ref[...].astype(BF16))


def _down(act, w_down, x, ln1, w_in_t, l, tm, with_norm):
    nk, r, tk = act.shape
    row = lambda i, k: (i, 0)
    in_specs = [
        pl.BlockSpec((None, tm, tk), lambda i, k: (k, i, 0)),
        pl.BlockSpec((None, tk, D_MODEL), lambda i, k: (l, k, 0)),
        pl.BlockSpec((tm, D_MODEL), row),
    ]
    out_specs = [pl.BlockSpec((tm, D_MODEL), row)]
    out_shape = [jax.ShapeDtypeStruct((r, D_MODEL), F32)]
    args = [act, w_down, x]
    if with_norm:
        in_specs += [_ln_spec(l + 1, 2), _wgk_spec(l + 1, 2)]
        out_specs += [pl.BlockSpec((tm, D_MODEL), row), pl.BlockSpec((tm, GATE_RANK), row)]
        out_shape += [jax.ShapeDtypeStruct((r, D_MODEL), BF16), jax.ShapeDtypeStruct((r, GATE_RANK), F32)]
        args += [ln1, w_in_t]
    out = pl.pallas_call(
        _down_norm_kernel if with_norm else _down_last_kernel,
        grid=(r // tm, nk),
        in_specs=in_specs,
        out_specs=out_specs,
        out_shape=out_shape,
        compiler_params=_cparams(("parallel", "arbitrary")),
        name="down_norm" if with_norm else "down_last",
    )(*args)
    return out if with_norm else (out[0], None, None)


def kernel(x_prompt, x_sample, cache_win_k, cache_win_v, state_gla, state_conv, rel_bias,
           ln1, w_in, ln_q, ln_k, sinks, w_gk2, b_gk, ln_o, w_oa, w_ob, w_out,
           ln2, w_up, w_conv, b_conv, w_down):
    bp, seq, _ = x_prompt.shape
    bs, t_new, _ = x_sample.shape
    tm_p = TM_PROMPT
    tm_s = bs * t_new
    tiles_per_seq = seq // tm_p

    ln1_3 = ln1.reshape(DEPTH, 1, D_MODEL)
    ln2_3 = ln2.reshape(DEPTH, 1, D_MODEL)
    w_in_t = jnp.swapaxes(w_in, 1, 2)
    w_oa_b, w_ob_b, w_out_b, w_down_b = (w.astype(BF16) for w in (w_oa, w_ob, w_out, w_down))
    w2_b = w_gk2.astype(BF16)
    b_conv3 = b_conv.reshape(DEPTH, 1, 2 * D_FF)
    lnq2 = jnp.tile(ln_q, (1, 2)).reshape(DEPTH, 1, LANES)
    lnk2 = jnp.tile(ln_k, (1, 2)).reshape(DEPTH, 1, LANES)

    bias_p = jnp.stack([_bias_table(rel_bias, jnp.asarray(_dmap_prompt(first))) for first in (True, False)])
    bias_s = _bias_table(rel_bias, jnp.asarray(_dmap_sample(t_new, SAMPLE_GROUP_A)))

    xp = x_prompt.reshape(bp * seq, D_MODEL)
    xs = x_sample.reshape(bs * t_new, D_MODEL)
    win_k = cache_win_k.reshape(DEPTH, bs, WINDOW, KV_WIDTH_A)
    win_v = cache_win_v.reshape(DEPTH, bs, WINDOW, KV_WIDTH_A)
    conv_state = state_conv.reshape(DEPTH, bs * (CONV_W - 1), 2 * D_FF)

    def dense_in(h, l, tm):
        proj = _proj(h, w_in_t, l, tm, TN_MAIN, 0, MAIN_COLS, False, F32, "proj_main")
        gates = _proj(h, w_in_t, l, tm, TN_GATES, OFF_GA, GATE_COLS, True, BF16, "proj_gates")
        return proj, gates

    hp, gkp = _norm(xp, ln1_3, w_in_t, 0, 512)
    hs, gks = _norm(xs, ln1_3, w_in_t, 0, tm_s)

    kp, vp, gp, cp = [], [], [], []
    ksm, vsm, gsm, csm = [], [], [], []
    for l in range(DEPTH):
        lq, lk, sk = lnq2[l], lnk2[l], sinks[l]
        w2_l, bgk_l, lno_l = w2_b[l], b_gk[l].reshape(1, WIDTH_BK), ln_o[l].reshape(1, DV_B)
        more = l + 1 < DEPTH

        proj = _proj(hp, w_in_t, l, tm_p, TN_MAIN, 0, MAIN_COLS, False, F32, "proj_main")
        gates, oa, kn = _gates_attn(hp, w_in_t, proj, sk, lq, lk, bias_p, l, tm_p, seq)
        ob, s_new = _gla_prompt(proj, gkp, w2_l, bgk_l, lno_l, bp, seq)
        xp, h2 = _merge(oa, ob, gates, xp, w_oa_b, w_ob_b, w_out_b, ln2_3, l, TM_MERGE)
        act, tail_v, tail_g = _ffn_up_prompt(h2, w_up, w_conv, b_conv3, l, TM_FFN_UP, seq // TM_FFN_UP)
        xp, hp, gkp = _down(act, w_down_b, xp, ln1_3, w_in_t, l, tm_p, more)
        kp.append(kn.reshape(bp, seq, N_KV_A, HEAD_DIM)[:, -WINDOW:])
        vp.append(proj[:, OFF_VA:OFF_VA + KV_WIDTH_A].reshape(bp, seq, N_KV_A, HEAD_DIM)[:, -WINDOW:])
        gp.append(s_new)
        tails = jnp.concatenate([tail_v, tail_g], axis=-1).reshape(bp, seq // TM_FFN_UP, 8, 2 * D_FF)
        cp.append(tails[:, -1, 8 - (CONV_W - 1):])

        proj, gates = dense_in(hs, l, tm_s)
        oa, kn = _attn_sample(proj, win_k[l], win_v[l], sk, lq, lk, bias_s, bs, t_new)
        ob, s_new = _gla_sample(proj, gks, w2_l, bgk_l, lno_l, state_gla, l, bs, t_new)
        xs, h2 = _merge(oa, ob, gates, xs, w_oa_b, w_ob_b, w_out_b, ln2_3, l, tm_s)
        act, u_v, u_g = _ffn_up_sample(h2, w_up, w_conv, b_conv3, conv_state, l, t_new)
        xs, hs, gks = _down(act, w_down_b, xs, ln1_3, w_in_t, l, tm_s, more)
        k_new = kn.reshape(bs, t_new, N_KV_A, HEAD_DIM)
        v_new = proj[:, OFF_VA:OFF_VA + KV_WIDTH_A].reshape(bs, t_new, N_KV_A, HEAD_DIM)
        ksm.append(jnp.concatenate([cache_win_k[l][:, t_new:], k_new], axis=1))
        vsm.append(jnp.concatenate([cache_win_v[l][:, t_new:], v_new], axis=1))
        gsm.append(s_new)
        u = jnp.concatenate([u_v, u_g], axis=-1).reshape(bs, t_new, 2 * D_FF)
        csm.append(u[:, t_new - (CONV_W - 1):])

    return (xp.reshape(bp, seq, D_MODEL), xs.reshape(bs, t_new, D_MODEL),
            jnp.stack(kp), jnp.stack(vp), jnp.stack(gp), jnp.stack(cp),
            jnp.stack(ksm), jnp.stack(vsm), jnp.stack(gsm), jnp.stack(csm))
```

```python
import functools
import math

import numpy as np
import jax
import jax.numpy as jnp
from jax import lax
from jax.experimental import pallas as pl
from jax.experimental.pallas import tpu as pltpu

F32 = jnp.float32
BF16 = jnp.bfloat16

D_MODEL = 2048
DEPTH = 4
HEAD_DIM = 64
N_HEADS_A = 16
N_KV_A = 4
WIDTH_A = N_HEADS_A * HEAD_DIM
KV_WIDTH_A = N_KV_A * HEAD_DIM
WINDOW = 128
N_BUCKETS = 32
MAX_DISTANCE = 128
N_HEADS_B = 4
DK_B = 128
DV_B = 256
WIDTH_BK = N_HEADS_B * DK_B
WIDTH_BV = N_HEADS_B * DV_B
GATE_RANK = 16
GATE_TAU = 16.0
D_FF = 5632
CONV_W = 3
EPS = 1e-6

OFF_QA = 0
OFF_KA = OFF_QA + WIDTH_A
OFF_VA = OFF_KA + KV_WIDTH_A
OFF_QB = OFF_VA + KV_WIDTH_A
OFF_KB = OFF_QB + WIDTH_BK
OFF_VB = OFF_KB + WIDTH_BK
OFF_RB = OFF_VB + WIDTH_BV
OFF_GK = OFF_RB + WIDTH_BV
OFF_GA = OFF_GK + GATE_RANK
MAIN_COLS = OFF_GK
GATE_COLS = 2 * D_MODEL

LANES = 128
LOG2E = math.log2(math.e)
NEG_BIG = -1e30
VMEM_LIMIT = 56 * 1024 * 1024

TM_PROMPT = 1024
TM_FFN_UP = 1024
TM_MERGE = 256
TN_MAIN = 1536
TN_GATES = 1024
ATTN_BLOCKS_PER_STEP = 2
TF_UP = 512
GLA_CHUNK_P = 128
GLA_TBLOCK = 512
SAMPLE_GROUP_A = 8
SAMPLE_GROUP_B = 4


def _cparams(sem):
    return pltpu.CompilerParams(dimension_semantics=sem, vmem_limit_bytes=VMEM_LIMIT)


def _dot(a, b):
    return jnp.dot(a, b, preferred_element_type=F32)


def _dot_nt(a, b):
    return lax.dot_general(a, b, (((1,), (1,)), ((), ())), preferred_element_type=F32)


def _dot_tn(a, b):
    return lax.dot_general(a, b, (((0,), (0,)), ((), ())), preferred_element_type=F32)


def _split3(x):
    hi = x.astype(BF16)
    r = x - hi.astype(F32)
    mid = r.astype(BF16)
    lo = (r - mid.astype(F32)).astype(BF16)
    return hi, mid, lo


def _dot_exact_lhs01(m01, x):
    hi, mid, lo = _split3(x)
    return _dot(m01, hi) + _dot(m01, mid) + _dot(m01, lo)


def _rms(x, g):
    ms = jnp.mean(x * x, axis=-1, keepdims=True)
    return x * lax.rsqrt(ms + EPS) * g


def _norm_kernel(x_ref, ln_ref, wgk_ref, h_ref, gk_ref):
    h = _rms(x_ref[...], ln_ref[...]).astype(BF16)
    h_ref[...] = h
    gk_ref[...] = _dot_nt(h, wgk_ref[...].astype(BF16))


def _ln_spec(l, nargs):
    if nargs == 1:
        return pl.BlockSpec((None, 1, D_MODEL), lambda i: (l, 0, 0))
    return pl.BlockSpec((None, 1, D_MODEL), lambda i, j: (l, 0, 0))


def _wgk_spec(l, nargs):
    blk = OFF_GK // GATE_RANK
    if nargs == 1:
        return pl.BlockSpec((None, GATE_RANK, D_MODEL), lambda i: (l, blk, 0))
    return pl.BlockSpec((None, GATE_RANK, D_MODEL), lambda i, j: (l, blk, 0))


def _norm(x, ln1, w_in_t, l, tm):
    r = x.shape[0]
    return pl.pallas_call(
        _norm_kernel,
        grid=(r // tm,),
        in_specs=[pl.BlockSpec((tm, D_MODEL), lambda i: (i, 0)), _ln_spec(l, 1), _wgk_spec(l, 1)],
        out_specs=[pl.BlockSpec((tm, D_MODEL), lambda i: (i, 0)), pl.BlockSpec((tm, GATE_RANK), lambda i: (i, 0))],
        out_shape=[jax.ShapeDtypeStruct((r, D_MODEL), BF16), jax.ShapeDtypeStruct((r, GATE_RANK), F32)],
        compiler_params=_cparams(("parallel",)),
        name="norm",
    )(x, ln1, w_in_t)


def _proj_kernel(sigmoid, h_ref, w_ref, o_ref, wb_scr):
    @pl.when(pl.program_id(1) == 0)
    def _():
        wb_scr[...] = w_ref[...].astype(BF16)

    y = _dot_nt(h_ref[...], wb_scr[...])
    if sigmoid:
        y = jax.nn.sigmoid(y)
    o_ref[...] = y.astype(o_ref.dtype)


def _proj(h, w_in_t, l, tm, tn, row_off, n_cols, sigmoid, out_dtype, name):
    r = h.shape[0]
    if row_off == 0:
        w_spec = pl.BlockSpec((None, tn, D_MODEL), lambda j, i: (l, j, 0))
    else:
        w_spec = pl.BlockSpec((pl.Squeezed(), pl.Element(tn), pl.Element(D_MODEL)),
                              lambda j, i: (l, pl.multiple_of(row_off + j * tn, GATE_RANK), 0))
    return pl.pallas_call(
        functools.partial(_proj_kernel, sigmoid),
        grid=(n_cols // tn, r // tm),
        in_specs=[pl.BlockSpec((tm, D_MODEL), lambda j, i: (i, 0)), w_spec],
        out_specs=pl.BlockSpec((tm, tn), lambda j, i: (i, j)),
        out_shape=jax.ShapeDtypeStruct((r, n_cols), out_dtype),
        scratch_shapes=[pltpu.VMEM((tn, D_MODEL), BF16)],
        compiler_params=_cparams(("arbitrary", "arbitrary")),
        name=name,
    )(h, w_in_t)


def _bucket_ranges():
    d = np.arange(WINDOW)
    max_exact = N_BUCKETS // 2
    df = np.maximum(d, 1).astype(np.float32)
    large = max_exact + (np.log(df / np.float32(max_exact)) / np.float32(math.log(MAX_DISTANCE / max_exact))
                         * np.float32(N_BUCKETS - max_exact)).astype(np.int32)
    large = np.minimum(large, N_BUCKETS - 1)
    bucket = np.where(d < max_exact, d, large)
    ranges = []
    for b in range(N_BUCKETS):
        idx = np.nonzero(bucket == b)[0]
        if idx.size:
            assert idx[-1] - idx[0] + 1 == idx.size
            ranges.append((b, int(idx[0]), int(idx[-1])))
    return ranges


def _bias_kernel(rel_ref, d_ref, o_ref):
    h = pl.program_id(0)
    d = d_ref[...]
    val = jnp.full(d.shape, NEG_BIG, F32)
    for b, lo, hi in _bucket_ranges():
        val = jnp.where((d >= lo) & (d <= hi), rel_ref[b, h] * LOG2E, val)
    o_ref[...] = val


def _bias_table(rel_bias, dmap):
    mq, nk = dmap.shape
    return pl.pallas_call(
        _bias_kernel,
        grid=(N_HEADS_A,),
        in_specs=[
            pl.BlockSpec(memory_space=pltpu.SMEM),
            pl.BlockSpec((mq, nk), lambda h: (0, 0)),
        ],
        out_specs=pl.BlockSpec((None, mq, nk), lambda h: (h, 0, 0)),
        out_shape=jax.ShapeDtypeStruct((N_HEADS_A, mq, nk), F32),
        compiler_params=_cparams(("parallel",)),
        name="bias_table",
    )(rel_bias, dmap)


def _dmap_prompt(first_block):
    i = np.arange(WINDOW)[:, None]
    j = np.arange(2 * WINDOW)[None, :]
    d = i + WINDOW - j
    ok = (d >= 0) & (d < WINDOW)
    if first_block:
        ok = ok & (j >= WINDOW)
    return np.where(ok, d, -1).astype(np.int32)


def _dmap_sample(t_new, group):
    nk = group * WINDOW + LANES
    rows = np.arange(group * t_new)
    rb, rt = rows // t_new, rows % t_new
    d = np.full((group * t_new, nk), -1, np.int64)
    cols = np.arange(group * WINDOW)
    cb, cs = cols // WINDOW, cols % WINDOW
    dw = rt[:, None] + WINDOW - cs[None, :]
    ok = (rb[:, None] == cb[None, :]) & (dw >= 0) & (dw < WINDOW)
    d[:, :group * WINDOW] = np.where(ok, dw, -1)
    ncols = np.arange(group * t_new)
    nb, nu = ncols // t_new, ncols % t_new
    dn = rt[:, None] - nu[None, :]
    okn = (rb[:, None] == nb[None, :]) & (dn >= 0)
    d[:, group * WINDOW:group * WINDOW + group * t_new] = np.where(okn, dn, -1)
    return d.astype(np.int32)


def _group_ones():
    r = lax.broadcasted_iota(jnp.int32, (LANES, LANES), 0)
    c = lax.broadcasted_iota(jnp.int32, (LANES, LANES), 1)
    low_c = jnp.where(c < HEAD_DIM, 1.0, 0.0)
    return jnp.where(r < HEAD_DIM, low_c, 1.0 - low_c).astype(BF16)


def _head_norm(x, gmat, ln2):
    sq = x * x
    hi = sq.astype(BF16)
    lo = (sq - hi.astype(F32)).astype(BF16)
    ms = (_dot(hi, gmat) + _dot(lo, gmat)) * (1.0 / HEAD_DIM)
    return x * lax.rsqrt(ms + EPS) * ln2


def _split_heads(x):
    lane = lax.broadcasted_iota(jnp.int32, (x.shape[0], LANES), 1)
    low = lane < HEAD_DIM
    lo_parts, hi_parts = [], []
    for c in range(KV_WIDTH_A // LANES):
        xc = x[:, c * LANES:(c + 1) * LANES]
        xr = pltpu.roll(xc, HEAD_DIM, 1)
        lo_parts += [jnp.where(low, xc, 0.0).astype(BF16), jnp.where(low, xr, 0.0).astype(BF16)]
        hi_parts += [jnp.where(low, 0.0, xr).astype(BF16), jnp.where(low, 0.0, xc).astype(BF16)]
    return lo_parts, hi_parts


def _attend(q_cols, k_parts, v_parts, bias_ref, sink_ref):
    mq = q_cols[0].shape[0]
    pairs_per_kv = N_HEADS_A // N_KV_A // 2
    assert pairs_per_kv == 2
    first_rows = lax.broadcasted_iota(jnp.int32, (pairs_per_kv * mq, 1), 0) < mq
    outs = [None] * (N_HEADS_A // 2)
    for kh in range(N_KV_A):
        pcs = [kh * pairs_per_kv + p for p in range(pairs_per_kv)]
        q_stack = jnp.concatenate([q_cols[pc] for pc in pcs], axis=0)
        for half in range(2):
            heads = [2 * pc + half for pc in pcs]
            bias = jnp.concatenate([bias_ref[h] for h in heads], axis=0)
            sink = jnp.where(first_rows, sink_ref[heads[0]], sink_ref[heads[1]]) * LOG2E
            s = _dot_nt(q_stack, k_parts[half][kh]) + bias
            m = jnp.maximum(jnp.max(s, axis=-1, keepdims=True), sink)
            e = jnp.exp2(s - m)
            den = jnp.sum(e, axis=-1, keepdims=True) + jnp.exp2(sink - m)
            o = _dot(e.astype(BF16), v_parts[half][kh]) * (1.0 / den)
            for p, pc in enumerate(pcs):
                o_p = o[p * mq:(p + 1) * mq, :]
                outs[pc] = o_p if outs[pc] is None else outs[pc] + o_p
    return outs


def _gates_attn_kernel(blocks_per_seq, sink_ref, h_ref, w_ref, q_ref, k_ref, v_ref, lnq_ref, lnk_ref, bias_ref,
                       g_ref, o_ref, kn_ref, wb_scr, kprev_scr, vprev_scr, o_scr, kn_scr):
    j = pl.program_id(0)
    i = pl.program_id(1)
    step = j * pl.num_programs(1) + i

    @pl.when(i == 0)
    def _():
        wb_scr[...] = w_ref[...].astype(BF16)

    @pl.when(step == 0)
    def _():
        kprev_scr[...] = jnp.zeros_like(kprev_scr)
        vprev_scr[...] = jnp.zeros_like(vprev_scr)

    gmat = _group_ones()
    lnq = lnq_ref[...]
    lnk = lnk_ref[...]
    scale = HEAD_DIM ** -0.5 * LOG2E
    for t in range(ATTN_BLOCKS_PER_STEP):
        rows = slice(t * WINDOW, (t + 1) * WINDOW)
        block = step * ATTN_BLOCKS_PER_STEP + t
        table = bias_ref.at[jnp.where(block % blocks_per_seq == 0, 0, 1)]
        kn_own = jnp.concatenate(
            [_head_norm(k_ref[rows, c * LANES:(c + 1) * LANES], gmat, lnk) for c in range(KV_WIDTH_A // LANES)],
            axis=1)
        kn_scr[rows, :] = kn_own
        k_own = _split_heads(kn_own)
        v_own = _split_heads(v_ref[rows, :])
        k_parts = [[jnp.concatenate([kprev_scr[half, kh], k_own[half][kh]], axis=0) for kh in range(N_KV_A)]
                   for half in range(2)]
        v_parts = [[jnp.concatenate([vprev_scr[half, kh], v_own[half][kh]], axis=0) for kh in range(N_KV_A)]
                   for half in range(2)]
        q_cols = [(_head_norm(q_ref[rows, c * LANES:(c + 1) * LANES], gmat, lnq) * scale).astype(BF16)
                  for c in range(WIDTH_A // LANES)]
        outs = _attend(q_cols, k_parts, v_parts, table, sink_ref)
        for c, o in enumerate(outs):
            o_scr[rows, c * LANES:(c + 1) * LANES] = o.astype(o_scr.dtype)
        for half in range(2):
            for kh in range(N_KV_A):
                kprev_scr[half, kh] = k_own[half][kh]
                vprev_scr[half, kh] = v_own[half][kh]

    g_ref[...] = jax.nn.sigmoid(_dot_nt(h_ref[...], wb_scr[...])).astype(g_ref.dtype)
    o_ref[...] = o_scr[...]
    kn_ref[...] = kn_scr[...]


def _gates_attn(h, w_in_t, proj, sinks_l, lnq2, lnk2, bias_p, l, tm, seq):
    r = h.shape[0]
    tn = TN_GATES
    nj, nt = GATE_COLS // tn, r // tm
    rows = ATTN_BLOCKS_PER_STEP * WINDOW
    assert nj * nt * rows == r
    kcol = OFF_KA // KV_WIDTH_A
    vcol = OFF_VA // KV_WIDTH_A
    step = lambda j, i: j * nt + i
    return pl.pallas_call(
        functools.partial(_gates_attn_kernel, seq // WINDOW),
        grid=(nj, nt),
        in_specs=[
            pl.BlockSpec(memory_space=pltpu.SMEM),
            pl.BlockSpec((tm, D_MODEL), lambda j, i: (i, 0)),
            pl.BlockSpec((pl.Squeezed(), pl.Element(tn), pl.Element(D_MODEL)),
                         lambda j, i: (l, pl.multiple_of(OFF_GA + j * tn, GATE_RANK), 0)),
            pl.BlockSpec((rows, WIDTH_A), lambda j, i: (step(j, i), 0)),
            pl.BlockSpec((rows, KV_WIDTH_A), lambda j, i: (step(j, i), kcol)),
            pl.BlockSpec((rows, KV_WIDTH_A), lambda j, i: (step(j, i), vcol)),
            pl.BlockSpec((1, LANES), lambda j, i: (0, 0)),
            pl.BlockSpec((1, LANES), lambda j, i: (0, 0)),
            pl.BlockSpec((2, N_HEADS_A, WINDOW, 2 * WINDOW), lambda j, i: (0, 0, 0, 0)),
        ],
        out_specs=[
            pl.BlockSpec((tm, tn), lambda j, i: (i, j)),
            pl.BlockSpec((rows, WIDTH_A), lambda j, i: (step(j, i), 0)),
            pl.BlockSpec((rows, KV_WIDTH_A), lambda j, i: (step(j, i), 0)),
        ],
        out_shape=[
            jax.ShapeDtypeStruct((r, GATE_COLS), BF16),
            jax.ShapeDtypeStruct((r, WIDTH_A), BF16),
            jax.ShapeDtypeStruct((r, KV_WIDTH_A), F32),
        ],
        scratch_shapes=[
            pltpu.VMEM((tn, D_MODEL), BF16),
            pltpu.VMEM((2, N_KV_A, WINDOW, LANES), BF16),
            pltpu.VMEM((2, N_KV_A, WINDOW, LANES), BF16),
            pltpu.VMEM((rows, WIDTH_A), BF16),
            pltpu.VMEM((rows, KV_WIDTH_A), F32),
        ],
        compiler_params=_cparams(("arbitrary", "arbitrary")),
        name="gates_attn",
    )(sinks_l, h, w_in_t, proj, proj, proj, lnq2, lnk2, bias_p)


def _attn_sample_kernel(sink_ref, q_ref, kn_in_ref, vn_ref, wk_ref, wv_ref, lnq_ref, lnk_ref, bias_ref,
                        o_ref, kn_ref):
    gmat = _group_ones()
    lnq = lnq_ref[...]
    lnk = lnk_ref[...]
    scale = HEAD_DIM ** -0.5 * LOG2E
    rows = q_ref.shape[0]
    g = wk_ref.shape[0]
    kn_new = jnp.concatenate(
        [_head_norm(kn_in_ref[:, c * LANES:(c + 1) * LANES], gmat, lnk) for c in range(KV_WIDTH_A // LANES)], axis=1)
    kn_ref[...] = kn_new
    pad = jnp.zeros((LANES - rows, KV_WIDTH_A), F32)
    k = jnp.concatenate([wk_ref[...].reshape(g * WINDOW, KV_WIDTH_A), kn_new, pad], axis=0)
    v = jnp.concatenate([wv_ref[...].reshape(g * WINDOW, KV_WIDTH_A), vn_ref[...], pad], axis=0)
    q_cols = [(_head_norm(q_ref[:, c * LANES:(c + 1) * LANES], gmat, lnq) * scale).astype(BF16)
              for c in range(WIDTH_A // LANES)]
    outs = _attend(q_cols, _split_heads(k), _split_heads(v), bias_ref, sink_ref)
    for c, o in enumerate(outs):
        o_ref[:, c * LANES:(c + 1) * LANES] = o.astype(o_ref.dtype)


def _attn_sample(proj, win_k, win_v, sinks_l, lnq2, lnk2, bias_s, batch, t_new):
    g = SAMPLE_GROUP_A
    rows = g * t_new
    nk = g * WINDOW + LANES
    kcol = OFF_KA // KV_WIDTH_A
    vcol = OFF_VA // KV_WIDTH_A
    return pl.pallas_call(
        _attn_sample_kernel,
        grid=(batch // g,),
        in_specs=[
            pl.BlockSpec(memory_space=pltpu.SMEM),
            pl.BlockSpec((rows, WIDTH_A), lambda i: (i, 0)),
            pl.BlockSpec((rows, KV_WIDTH_A), lambda i: (i, kcol)),
            pl.BlockSpec((rows, KV_WIDTH_A), lambda i: (i, vcol)),
            pl.BlockSpec((g, WINDOW, KV_WIDTH_A), lambda i: (i, 0, 0)),
            pl.BlockSpec((g, WINDOW, KV_WIDTH_A), lambda i: (i, 0, 0)),
            pl.BlockSpec((1, LANES), lambda i: (0, 0)),
            pl.BlockSpec((1, LANES), lambda i: (0, 0)),
            pl.BlockSpec((N_HEADS_A, rows, nk), lambda i: (0, 0, 0)),
        ],
        out_specs=[
            pl.BlockSpec((rows, WIDTH_A), lambda i: (i, 0)),
            pl.BlockSpec((rows, KV_WIDTH_A), lambda i: (i, 0)),
        ],
        out_shape=[
            jax.ShapeDtypeStruct((batch * t_new, WIDTH_A), BF16),
            jax.ShapeDtypeStruct((batch * t_new, KV_WIDTH_A), F32),
        ],
        compiler_params=_cparams(("parallel",)),
        name="attn_sample",
    )(sinks_l, proj, proj, proj, win_k, win_v, lnq2, lnk2, bias_s)


def _log_decay(gk, w2_ref, bgk_ref):
    x = _dot(gk.astype(BF16), w2_ref[...]) + bgk_ref[...]
    log_sig = jnp.minimum(x, 0.0) - jnp.log(1.0 + jnp.exp(-jnp.abs(x)))
    return log_sig * (1.0 / GATE_TAU)


def _column(row):
    r = lax.broadcasted_iota(jnp.int32, (LANES, LANES), 0)
    c = lax.broadcasted_iota(jnp.int32, (LANES, LANES), 1)
    return jnp.sum(jnp.where(r == c, row, 0.0), axis=1, keepdims=True)


def _gla_out(o, ln_o, rgate):
    return (_rms(o, ln_o) * (rgate * jax.nn.sigmoid(rgate))).astype(BF16)


def _gla_prompt_kernel(q_ref, k_ref, va_ref, vb_ref, ra_ref, rb_ref, gk_ref, w2_ref, bgk_ref, lno_ref,
                       o_ref, s_out_ref, s_scr, b_scr):
    t = pl.program_id(1)
    c_len = GLA_CHUNK_P
    tb = q_ref.shape[0]
    n_chunks = tb // c_len
    half = c_len // 2
    scale = DK_B ** -0.5

    @pl.when(t == 0)
    def _():
        s_scr[...] = jnp.zeros_like(s_scr)

    shift = c_len.bit_length() - 1
    rb = lax.broadcasted_iota(jnp.int32, (tb, tb), 0)
    cb = lax.broadcasted_iota(jnp.int32, (tb, tb), 1)
    same_chunk = lax.shift_right_logical(rb, shift) == lax.shift_right_logical(cb, shift)
    tri_blocks = jnp.where(same_chunk & (rb >= cb), 1.0, 0.0).astype(BF16)
    b_scr[...] = _dot_exact_lhs01(tri_blocks, _log_decay(gk_ref[...], w2_ref, bgk_ref))

    ri = lax.broadcasted_iota(jnp.int32, (c_len, c_len), 0)
    ci = lax.broadcasted_iota(jnp.int32, (c_len, c_len), 1)
    quarter = half // 2
    same_half = jnp.where(ci < half, jnp.where(ri < half, 1, 0), jnp.where(ri < half, 0, 1))
    own_causal = (ri >= ci) & (same_half == 1)
    upper_row = lax.broadcasted_iota(jnp.int32, (c_len, 1), 0) >= half
    ln_o = lno_ref[...]
    v_refs = (va_ref, vb_ref)
    r_refs = (ra_ref, rb_ref)

    def chunk(c, carry):
        rows = pl.ds(pl.multiple_of(c * c_len, c_len), c_len)
        for h in range(N_HEADS_B):
            ks = slice(h * DK_B, (h + 1) * DK_B)
            vs = slice((h % 2) * DV_B, (h % 2 + 1) * DV_B)
            bh = b_scr[rows, ks]
            b_last = bh[c_len - 1:c_len, :]
            b_half = bh[half - 1:half, :]
            q = q_ref[rows, ks] * scale
            k = k_ref[rows, ks]
            v = v_refs[h // 2][rows, vs].astype(BF16)
            q_inter = (q * jnp.exp(bh)).astype(BF16)
            r_own = jnp.where(upper_row, bh[half + quarter - 1:half + quarter, :], bh[quarter - 1:quarter, :])
            a_own = _dot_nt((q * jnp.exp(bh - r_own)).astype(BF16), (k * jnp.exp(r_own - bh)).astype(BF16))
            q_x = jnp.where(upper_row, q * jnp.exp(bh - b_half), 0.0).astype(BF16)
            k_x = jnp.where(upper_row, 0.0, k * jnp.exp(b_half - bh)).astype(BF16)
            a = (jnp.where(own_causal, a_own, 0.0) + _dot_nt(q_x, k_x)).astype(BF16)
            k_state = (k * jnp.exp(b_last - bh)).astype(BF16)
            s_old = s_scr[h]
            o = _dot(a, v) + _dot(q_inter, s_old.astype(BF16))
            s_scr[h] = _column(jnp.exp(b_last)) * s_old + _dot_tn(k_state, v)
            o_ref[rows, h * DV_B:(h + 1) * DV_B] = _gla_out(o, ln_o, r_refs[h // 2][rows, vs])
        return carry

    lax.fori_loop(0, n_chunks, chunk, 0, unroll=4)

    @pl.when(t == pl.num_programs(1) - 1)
    def _():
        s_out_ref[...] = s_scr[...]


def _gla_prompt(proj, gk, w2_l, bgk_l, lno_l, batch, seq):
    tb = GLA_TBLOCK
    nt = seq // tb
    w = 2 * DV_B
    row = lambda b, t: b * nt + t
    spec = lambda col: pl.BlockSpec((tb, w), lambda b, t: (row(b, t), col))
    return pl.pallas_call(
        _gla_prompt_kernel,
        grid=(batch, nt),
        in_specs=[
            spec(OFF_QB // w), spec(OFF_KB // w),
            spec(OFF_VB // w), spec(OFF_VB // w + 1),
            spec(OFF_RB // w), spec(OFF_RB // w + 1),
            pl.BlockSpec((tb, GATE_RANK), lambda b, t: (row(b, t), 0)),
            pl.BlockSpec((GATE_RANK, WIDTH_BK), lambda b, t: (0, 0)),
            pl.BlockSpec((1, WIDTH_BK), lambda b, t: (0, 0)),
            pl.BlockSpec((1, DV_B), lambda b, t: (0, 0)),
        ],
        out_specs=[
            pl.BlockSpec((tb, WIDTH_BV), lambda b, t: (row(b, t), 0)),
            pl.BlockSpec((None, N_HEADS_B, DK_B, DV_B), lambda b, t: (b, 0, 0, 0)),
        ],
        out_shape=[
            jax.ShapeDtypeStruct((batch * seq, WIDTH_BV), BF16),
            jax.ShapeDtypeStruct((batch, N_HEADS_B, DK_B, DV_B), F32),
        ],
        scratch_shapes=[pltpu.VMEM((N_HEADS_B, DK_B, DV_B), F32), pltpu.VMEM((tb, WIDTH_BK), F32)],
        compiler_params=_cparams(("parallel", "arbitrary")),
        name="gla_prompt",
    )(proj, proj, proj, proj, proj, proj, gk, w2_l, bgk_l, lno_l)


def _gla_sample_kernel(t_new, q_ref, k_ref, va_ref, vb_ref, ra_ref, rb_ref, gk_ref, w2_ref, bgk_ref, lno_ref,
                       s_ref, o_ref, s_out_ref):
    rows = q_ref.shape[0]
    n_seq = rows // t_new
    scale = DK_B ** -0.5
    ri = lax.broadcasted_iota(jnp.int32, (rows, rows), 0)
    ci = lax.broadcasted_iota(jnp.int32, (rows, rows), 1)
    same = None
    for sq in range(n_seq):
        lo, hi = sq * t_new, (sq + 1) * t_new
        blk = (ri >= lo) & (ri < hi) & (ci >= lo) & (ci < hi)
        same = blk if same is None else (same | blk)
    causal = same & (ri >= ci)
    tri = jnp.where(causal, 1.0, 0.0).astype(BF16)
    rcol = lax.broadcasted_iota(jnp.int32, (rows, 1), 0)
    ln_o = lno_ref[...]
    v_refs = (va_ref, vb_ref)
    r_refs = (ra_ref, rb_ref)

    g = _log_decay(gk_ref[...], w2_ref, bgk_ref)
    b = _dot_exact_lhs01(tri, g)
    for h in range(N_HEADS_B):
        ks = slice(h * DK_B, (h + 1) * DK_B)
        vs = slice((h % 2) * DV_B, (h % 2 + 1) * DV_B)
        bh = b[:, ks]
        gh = g[:, ks]
        q = q_ref[:, ks] * scale
        k = k_ref[:, ks]
        v = v_refs[h // 2][:, vs].astype(BF16)
        q_inter = q * jnp.exp(bh)
        k_t = (k * jnp.exp(-bh)).astype(BF16)
        a = jnp.where(causal, _dot_nt(q_inter.astype(BF16), k_t), 0.0).astype(BF16)
        o = _dot(a, v)
        for sq in range(n_seq):
            mine = (rcol >= sq * t_new) & (rcol < (sq + 1) * t_new)
            b_last = jnp.sum(jnp.where(mine, gh, 0.0), axis=0, keepdims=True)
            k_state = jnp.where(mine, k * jnp.exp(b_last - bh), 0.0).astype(BF16)
            q_mine = jnp.where(mine, q_inter, 0.0).astype(BF16)
            s_old = s_ref[sq, h]
            o = o + _dot(q_mine, s_old.astype(BF16))
            s_out_ref[sq, h] = _column(jnp.exp(b_last)) * s_old + _dot_tn(k_state, v)
        o_ref[:, h * DV_B:(h + 1) * DV_B] = _gla_out(o, ln_o, r_refs[h // 2][:, vs])


def _gla_sample(proj, gk, w2_l, bgk_l, lno_l, state, l, batch, t_new):
    g = SAMPLE_GROUP_B
    rows = g * t_new
    w = 2 * DV_B
    spec = lambda col: pl.BlockSpec((rows, w), lambda i: (i, col))
    return pl.pallas_call(
        functools.partial(_gla_sample_kernel, t_new),
        grid=(batch // g,),
        in_specs=[
            spec(OFF_QB // w), spec(OFF_KB // w),
            spec(OFF_VB // w), spec(OFF_VB // w + 1),
            spec(OFF_RB // w), spec(OFF_RB // w + 1),
            pl.BlockSpec((rows, GATE_RANK), lambda i: (i, 0)),
            pl.BlockSpec((GATE_RANK, WIDTH_BK), lambda i: (0, 0)),
            pl.BlockSpec((1, WIDTH_BK), lambda i: (0, 0)),
            pl.BlockSpec((1, DV_B), lambda i: (0, 0)),
            pl.BlockSpec((None, g, N_HEADS_B, DK_B, DV_B), lambda i: (l, i, 0, 0, 0)),
        ],
        out_specs=[
            pl.BlockSpec((rows, WIDTH_BV), lambda i: (i, 0)),
            pl.BlockSpec((g, N_HEADS_B, DK_B, DV_B), lambda i: (i, 0, 0, 0)),
        ],
        out_shape=[
            jax.ShapeDtypeStruct((batch * t_new, WIDTH_BV), BF16),
            jax.ShapeDtypeStruct((batch, N_HEADS_B, DK_B, DV_B), F32),
        ],
        compiler_params=_cparams(("parallel",)),
        name="gla_sample",
    )(proj, proj, proj, proj, proj, proj, gk, w2_l, bgk_l, lno_l, state)


def _merge_kernel(oa_ref, ob_ref, sa_ref, sb_ref, x_ref, woa_ref, wob_ref, wout_ref, ln_ref, x_out_ref, h_ref):
    a = _dot(oa_ref[...], woa_ref[...])
    b = _dot(ob_ref[...], wob_ref[...])
    mix = (sa_ref[...].astype(F32) * a + sb_ref[...].astype(F32) * b).astype(BF16)
    x_new = x_ref[...] + _dot(mix, wout_ref[...])
    x_out_ref[...] = x_new
    h_ref[...] = _rms(x_new, ln_ref[...]).astype(BF16)


def _merge(oa, ob, gates, x, w_oa, w_ob, w_out, ln2, l, tm):
    r = x.shape[0]
    resident = dict(pipeline_mode=pl.Buffered(1))
    row = lambda i: (i, 0)
    return pl.pallas_call(
        _merge_kernel,
        grid=(r // tm,),
        in_specs=[
            pl.BlockSpec((tm, WIDTH_A), row),
            pl.BlockSpec((tm, WIDTH_BV), row),
            pl.BlockSpec((tm, D_MODEL), lambda i: (i, 0)),
            pl.BlockSpec((tm, D_MODEL), lambda i: (i, 1)),
            pl.BlockSpec((tm, D_MODEL), row),
            pl.BlockSpec((None, WIDTH_A, D_MODEL), lambda i: (l, 0, 0), **resident),
            pl.BlockSpec((None, WIDTH_BV, D_MODEL), lambda i: (l, 0, 0), **resident),
            pl.BlockSpec((None, D_MODEL, D_MODEL), lambda i: (l, 0, 0), **resident),
            _ln_spec(l, 1),
        ],
        out_specs=[pl.BlockSpec((tm, D_MODEL), row), pl.BlockSpec((tm, D_MODEL), row)],
        out_shape=[jax.ShapeDtypeStruct((r, D_MODEL), F32), jax.ShapeDtypeStruct((r, D_MODEL), BF16)],
        compiler_params=_cparams(("parallel",)),
        name="merge",
    )(oa, ob, gates, gates, x, w_oa, w_ob, w_out, ln2)


def _gelu_tanh(x):
    return x * (0.5 * (1.0 + jnp.tanh(math.sqrt(2.0 / math.pi) * (x + 0.044715 * (x * x * x)))))


def _conv3(u, p1, p2, w_ref, b_ref):
    return w_ref[0:1, :] * p2 + w_ref[1:2, :] * p1 + w_ref[2:3, :] * u + b_ref[...]


def _history_selectors(n_seq, t_new):
    hist = CONV_W - 1
    e1 = np.zeros((n_seq * t_new, n_seq * hist), np.float32)
    e2 = np.zeros((n_seq * t_new, n_seq * hist), np.float32)
    for b in range(n_seq):
        for t in range(min(t_new, hist)):
            if t < 1:
                e1[b * t_new + t, b * hist + hist - 1 + t] = 1.0
            e2[b * t_new + t, b * hist + t] = 1.0
    return jnp.asarray(e1, BF16), jnp.asarray(e2, BF16)


def _ffn_up_prompt_kernel(tiles_per_seq, h_ref, wv_ref, wg_ref, cv_ref, cg_ref, bv_ref, bg_ref,
                          act_ref, tv_ref, tg_ref, wvb_scr, wgb_scr, carry_v, carry_g):
    i = pl.program_id(1)

    @pl.when(i == 0)
    def _():
        wvb_scr[...] = wv_ref[...].astype(BF16)
        wgb_scr[...] = wg_ref[...].astype(BF16)

    h = h_ref[...]
    tm = h.shape[0]
    first = (i % tiles_per_seq) == 0
    row8 = lax.broadcasted_iota(jnp.int32, (8, 1), 0)

    def branch(wb_scr, carry, c_ref, b_ref, tail_ref):
        u = _dot(h, wb_scr[...])
        prev = jnp.where(first, 0.0, carry[...])
        tail = u[tm - 8:tm, :]
        carry[...] = tail
        tail_ref[...] = tail
        p1 = pltpu.roll(u, 1, 0)
        p2 = pltpu.roll(u, 2, 0)
        body = _conv3(u, p1, p2, c_ref, b_ref)
        p1_top = jnp.where(row8 < 1, pltpu.roll(prev, 1, 0), p1[0:8, :])
        p2_top = jnp.where(row8 < 2, pltpu.roll(prev, 2, 0), p2[0:8, :])
        top = _conv3(u[0:8, :], p1_top, p2_top, c_ref, b_ref)
        return jnp.concatenate([top, body[8:, :]], axis=0)

    gate = _gelu_tanh(branch(wgb_scr, carry_g, cg_ref, bg_ref, tg_ref))
    val = branch(wvb_scr, carry_v, cv_ref, bv_ref, tv_ref)
    act_ref[...] = (gate * val).astype(act_ref.dtype)


def _ffn_up_sample_kernel(pos_ref, e1_ref, e2_ref, h_ref, wv_ref, wg_ref, cv_ref, cg_ref, bv_ref, bg_ref,
                          sv_ref, sg_ref, act_ref, uv_ref, ug_ref):
    h = h_ref[...]
    pos = pos_ref[...]
    e1 = e1_ref[...]
    e2 = e2_ref[...]

    def branch(w_ref, c_ref, b_ref, s_ref, u_ref):
        u = _dot(h, w_ref[...].astype(BF16))
        u_ref[...] = u
        cached = s_ref[...]
        p1 = jnp.where(pos >= 1, pltpu.roll(u, 1, 0), _dot_exact_lhs01(e1, cached))
        p2 = jnp.where(pos >= 2, pltpu.roll(u, 2, 0), _dot_exact_lhs01(e2, cached))
        return _conv3(u, p1, p2, c_ref, b_ref)

    val = branch(wv_ref, cv_ref, bv_ref, sv_ref, uv_ref)
    gate = branch(wg_ref, cg_ref, bg_ref, sg_ref, ug_ref)
    act_ref[...] = (_gelu_tanh(gate) * val).astype(act_ref.dtype)


def _ffn_up_kernel(tiles_per_seq, pos_ref, e1_ref, e2_ref, h_ref, hs_ref, wv_ref, wg_ref, cv_ref, cg_ref,
                   bv_ref, bg_ref, sv_ref, sg_ref,
                   act_ref, tv_ref, tg_ref, acts_ref, uv_ref, ug_ref, wvb_scr, wgb_scr, carry_v, carry_g):
    _ffn_up_prompt_kernel(tiles_per_seq, h_ref, wv_ref, wg_ref, cv_ref, cg_ref, bv_ref, bg_ref,
                          act_ref, tv_ref, tg_ref, wvb_scr, wgb_scr, carry_v, carry_g)

    @pl.when(pl.program_id(1) == pl.num_programs(1) - 1)
    def _():
        _ffn_up_sample_kernel(pos_ref, e1_ref, e2_ref, hs_ref, wvb_scr, wgb_scr, cv_ref, cg_ref, bv_ref, bg_ref,
                              sv_ref, sg_ref, acts_ref, uv_ref, ug_ref)


def _ffn_up(h, hs, w_up, w_conv, b_conv, conv_state, l, tm, tiles_per_seq, t_new):
    r = h.shape[0]
    rs = hs.shape[0]
    tf = TF_UP
    nt = r // tm
    nj = D_FF // tf
    ns = conv_state.shape[1]
    w_spec = lambda off: pl.BlockSpec((None, D_MODEL, tf), lambda j, i: (l, 0, j + off))
    c_spec = lambda off: pl.BlockSpec((None, CONV_W, tf), lambda j, i: (l, 0, j + off))
    b_spec = lambda off: pl.BlockSpec((None, 1, tf), lambda j, i: (l, 0, j + off))
    s_spec = lambda off: pl.BlockSpec((None, ns, tf), lambda j, i: (l, 0, j + off))
    tail_spec = pl.BlockSpec((None, 8, tf), lambda j, i: (i, 0, j))
    u_spec = pl.BlockSpec((rs, tf), lambda j, i: (0, j))
    fixed = lambda shape: pl.BlockSpec(shape, lambda j, i: (0, 0))
    pos = jnp.asarray((np.arange(rs) % t_new).astype(np.int32).reshape(rs, 1))
    e1, e2 = _history_selectors(rs // t_new, t_new)
    return pl.pallas_call(
        functools.partial(_ffn_up_kernel, tiles_per_seq),
        grid=(nj, nt),
        in_specs=[
            fixed((rs, 1)), fixed((rs, ns)), fixed((rs, ns)),
            pl.BlockSpec((tm, D_MODEL), lambda j, i: (i, 0)),
            fixed((rs, D_MODEL)),
            w_spec(0), w_spec(nj), c_spec(0), c_spec(nj), b_spec(0), b_spec(nj), s_spec(0), s_spec(nj),
        ],
        out_specs=[
            pl.BlockSpec((None, tm, tf), lambda j, i: (j, i, 0)), tail_spec, tail_spec,
            pl.BlockSpec((None, rs, tf), lambda j, i: (j, 0, 0)), u_spec, u_spec,
        ],
        out_shape=[
            jax.ShapeDtypeStruct((nj, r, tf), BF16),
            jax.ShapeDtypeStruct((nt, 8, D_FF), F32),
            jax.ShapeDtypeStruct((nt, 8, D_FF), F32),
            jax.ShapeDtypeStruct((nj, rs, tf), BF16),
            jax.ShapeDtypeStruct((rs, D_FF), F32),
            jax.ShapeDtypeStruct((rs, D_FF), F32),
        ],
        scratch_shapes=[
            pltpu.VMEM((D_MODEL, tf), BF16),
            pltpu.VMEM((D_MODEL, tf), BF16),
            pltpu.VMEM((8, tf), F32),
            pltpu.VMEM((8, tf), F32),
        ],
        compiler_params=_cparams(("arbitrary", "arbitrary")),
        name="ffn_up",
    )(pos, e1, e2, h, hs, w_up, w_up, w_conv, w_conv, b_conv, b_conv, conv_state, conv_state)


def _down_accumulate(act_ref, w_ref, x_ref, x_out_ref):
    @pl.when(pl.program_id(1) == 0)
    def _():
        x_out_ref[...] = x_ref[...]

    act = act_ref[...]
    step = 512
    for n in range(D_MODEL // step):
        cols = slice(n * step, (n + 1) * step)
        x_out_ref[:, cols] += _dot(act, w_ref[:, cols])


def _down_last_kernel(act_ref, w_ref, x_ref, x_out_ref):
    _down_accumulate(act_ref, w_ref, x_ref, x_out_ref)


def _down_norm_kernel(act_ref, w_ref, x_ref, ln_ref, wgk_ref, x_out_ref, h_ref, gk_ref):
    _down_accumulate(act_ref, w_ref, x_ref, x_out_ref)

    @pl.when(pl.program_id(1) == pl.num_programs(1) - 1)
    def _():
        h = _rms(x_out_ref[...], ln_ref[...]).astype(BF16)
        h_ref[...] = h
        gk_ref[...] = _dot_nt(h, wgk_ref[...].astype(BF16))


def _down(act, w_down, x, ln1, w_in_t, l, tm, with_norm):
    nk, r, tk = act.shape
    row = lambda i, k: (i, 0)
    in_specs = [
        pl.BlockSpec((None, tm, tk), lambda i, k: (k, i, 0)),
        pl.BlockSpec((None, tk, D_MODEL), lambda i, k: (l, k, 0)),
        pl.BlockSpec((tm, D_MODEL), row),
    ]
    out_specs = [pl.BlockSpec((tm, D_MODEL), row)]
    out_shape = [jax.ShapeDtypeStruct((r, D_MODEL), F32)]
    args = [act, w_down, x]
    if with_norm:
        in_specs += [_ln_spec(l + 1, 2), _wgk_spec(l + 1, 2)]
        out_specs += [pl.BlockSpec((tm, D_MODEL), row), pl.BlockSpec((tm, GATE_RANK), row)]
        out_shape += [jax.ShapeDtypeStruct((r, D_MODEL), BF16), jax.ShapeDtypeStruct((r, GATE_RANK), F32)]
        args += [ln1, w_in_t]
    out = pl.pallas_call(
        _down_norm_kernel if with_norm else _down_last_kernel,
        grid=(r // tm, nk),
        in_specs=in_specs,
        out_specs=out_specs,
        out_shape=out_shape,
        compiler_params=_cparams(("parallel", "arbitrary")),
        name="down_norm" if with_norm else "down_last",
    )(*args)
    return out if with_norm else (out[0], None, None)


def kernel(x_prompt, x_sample, cache_win_k, cache_win_v, state_gla, state_conv, rel_bias,
           ln1, w_in, ln_q, ln_k, sinks, w_gk2, b_gk, ln_o, w_oa, w_ob, w_out,
           ln2, w_up, w_conv, b_conv, w_down):
    bp, seq, _ = x_prompt.shape
    bs, t_new, _ = x_sample.shape
    tm_p = TM_PROMPT
    tm_s = bs * t_new
    tiles_per_seq = seq // tm_p

    ln1_3 = ln1.reshape(DEPTH, 1, D_MODEL)
    ln2_3 = ln2.reshape(DEPTH, 1, D_MODEL)
    w_in_t = jnp.swapaxes(w_in, 1, 2)
    w_oa_b, w_ob_b, w_out_b, w_down_b = (w.astype(BF16) for w in (w_oa, w_ob, w_out, w_down))
    w2_b = w_gk2.astype(BF16)
    b_conv3 = b_conv.reshape(DEPTH, 1, 2 * D_FF)
    lnq2 = jnp.tile(ln_q, (1, 2)).reshape(DEPTH, 1, LANES)
    lnk2 = jnp.tile(ln_k, (1, 2)).reshape(DEPTH, 1, LANES)

    bias_p = jnp.stack([_bias_table(rel_bias, jnp.asarray(_dmap_prompt(first))) for first in (True, False)])
    bias_s = _bias_table(rel_bias, jnp.asarray(_dmap_sample(t_new, SAMPLE_GROUP_A)))

    xp = x_prompt.reshape(bp * seq, D_MODEL)
    xs = x_sample.reshape(bs * t_new, D_MODEL)
    win_k = cache_win_k.reshape(DEPTH, bs, WINDOW, KV_WIDTH_A)
    win_v = cache_win_v.reshape(DEPTH, bs, WINDOW, KV_WIDTH_A)
    conv_state = state_conv.reshape(DEPTH, bs * (CONV_W - 1), 2 * D_FF)

    def dense_in(h, l, tm):
        proj = _proj(h, w_in_t, l, tm, TN_MAIN, 0, MAIN_COLS, False, F32, "proj_main")
        gates = _proj(h, w_in_t, l, tm, TN_GATES, OFF_GA, GATE_COLS, True, BF16, "proj_gates")
        return proj, gates

    hp, gkp = _norm(xp, ln1_3, w_in_t, 0, 512)
    hs, gks = _norm(xs, ln1_3, w_in_t, 0, tm_s)

    kp, vp, gp, cp = [], [], [], []
    ksm, vsm, gsm, csm = [], [], [], []
    for l in range(DEPTH):
        lq, lk, sk = lnq2[l], lnk2[l], sinks[l]
        w2_l, bgk_l, lno_l = w2_b[l], b_gk[l].reshape(1, WIDTH_BK), ln_o[l].reshape(1, DV_B)
        more = l + 1 < DEPTH

        proj = _proj(hp, w_in_t, l, tm_p, TN_MAIN, 0, MAIN_COLS, False, F32, "proj_main")
        gates, oa, kn = _gates_attn(hp, w_in_t, proj, sk, lq, lk, bias_p, l, tm_p, seq)
        ob, s_new = _gla_prompt(proj, gkp, w2_l, bgk_l, lno_l, bp, seq)
        xp, h2 = _merge(oa, ob, gates, xp, w_oa_b, w_ob_b, w_out_b, ln2_3, l, TM_MERGE)
        kp.append(kn.reshape(bp, seq, N_KV_A, HEAD_DIM)[:, -WINDOW:])
        vp.append(proj[:, OFF_VA:OFF_VA + KV_WIDTH_A].reshape(bp, seq, N_KV_A, HEAD_DIM)[:, -WINDOW:])
        gp.append(s_new)

        proj, gates = dense_in(hs, l, tm_s)
        oa, kn = _attn_sample(proj, win_k[l], win_v[l], sk, lq, lk, bias_s, bs, t_new)
        ob, s_new = _gla_sample(proj, gks, w2_l, bgk_l, lno_l, state_gla, l, bs, t_new)
        xs, h2_s = _merge(oa, ob, gates, xs, w_oa_b, w_ob_b, w_out_b, ln2_3, l, tm_s)

        act, tail_v, tail_g, act_s, u_v, u_g = _ffn_up(h2, h2_s, w_up, w_conv, b_conv3, conv_state, l, TM_FFN_UP,
                                                       seq // TM_FFN_UP, t_new)
        xp, hp, gkp = _down(act, w_down_b, xp, ln1_3, w_in_t, l, tm_p, more)
        xs, hs, gks = _down(act_s, w_down_b, xs, ln1_3, w_in_t, l, tm_s, more)
        tails = jnp.concatenate([tail_v, tail_g], axis=-1).reshape(bp, seq // TM_FFN_UP, 8, 2 * D_FF)
        cp.append(tails[:, -1, 8 - (CONV_W - 1):])
        k_new = kn.reshape(bs, t_new, N_KV_A, HEAD_DIM)
        v_new = proj[:, OFF_VA:OFF_VA + KV_WIDTH_A].reshape(bs, t_new, N_KV_A, HEAD_DIM)
        ksm.append(jnp.concatenate([cache_win_k[l][:, t_new:], k_new], axis=1))
        vsm.append(jnp.concatenate([cache_win_v[l][:, t_new:], v_new], axis=1))
        gsm.append(s_new)
        u = jnp.concatenate([u_v, u_g], axis=-1).reshape(bs, t_new, 2 * D_FF)
        csm.append(u[:, t_new - (CONV_W - 1):])

    return (xp.reshape(bp, seq, D_MODEL), xs.reshape(bs, t_new, D_MODEL),
            jnp.stack(kp), jnp.stack(vp), jnp.stack(gp), jnp.stack(cp),
            jnp.stack(ksm), jnp.stack(vsm), jnp.stack(gsm), jnp.stack(csm))
```

```python
import functools
import math

import numpy as np
import jax
import jax.numpy as jnp
from jax import lax
from jax.experimental import pallas as pl
from jax.experimental.pallas import tpu as pltpu

F32 = jnp.float32
BF16 = jnp.bfloat16

D_MODEL = 2048
DEPTH = 4
HEAD_DIM = 64
N_HEADS_A = 16
N_KV_A = 4
WIDTH_A = N_HEADS_A * HEAD_DIM
KV_WIDTH_A = N_KV_A * HEAD_DIM
WINDOW = 128
N_BUCKETS = 32
MAX_DISTANCE = 128
N_HEADS_B = 4
DK_B = 128
DV_B = 256
WIDTH_BK = N_HEADS_B * DK_B
WIDTH_BV = N_HEADS_B * DV_B
GATE_RANK = 16
GATE_TAU = 16.0
D_FF = 5632
CONV_W = 3
EPS = 1e-6

OFF_QA = 0
OFF_KA = OFF_QA + WIDTH_A
OFF_VA = OFF_KA + KV_WIDTH_A
OFF_QB = OFF_VA + KV_WIDTH_A
OFF_KB = OFF_QB + WIDTH_BK
OFF_VB = OFF_KB + WIDTH_BK
OFF_RB = OFF_VB + WIDTH_BV
OFF_GK = OFF_RB + WIDTH_BV
OFF_GA = OFF_GK + GATE_RANK
MAIN_COLS = OFF_GK
GATE_COLS = 2 * D_MODEL

LANES = 128
LOG2E = math.log2(math.e)
NEG_BIG = -1e30
VMEM_LIMIT = 56 * 1024 * 1024

TM_PROMPT = 1024
TM_FFN_UP = 1024
TM_MERGE = 256
TN_MAIN = 1536
TN_GATES = 1024
ATTN_BLOCKS_PER_STEP = 2
TF_UP = 512
GLA_CHUNK_P = 128
GLA_TBLOCK = 512
SAMPLE_GROUP_A = 8
SAMPLE_GROUP_B = 4


def _cparams(sem):
    return pltpu.CompilerParams(dimension_semantics=sem, vmem_limit_bytes=VMEM_LIMIT)


def _dot(a, b):
    return jnp.dot(a, b, preferred_element_type=F32)


def _dot_nt(a, b):
    return lax.dot_general(a, b, (((1,), (1,)), ((), ())), preferred_element_type=F32)


def _dot_tn(a, b):
    return lax.dot_general(a, b, (((0,), (0,)), ((), ())), preferred_element_type=F32)


def _split3(x):
    hi = x.astype(BF16)
    r = x - hi.astype(F32)
    mid = r.astype(BF16)
    lo = (r - mid.astype(F32)).astype(BF16)
    return hi, mid, lo


def _dot_exact_lhs01(m01, x):
    hi, mid, lo = _split3(x)
    return _dot(m01, hi) + _dot(m01, mid) + _dot(m01, lo)


def _rms(x, g):
    ms = jnp.mean(x * x, axis=-1, keepdims=True)
    return x * lax.rsqrt(ms + EPS) * g


def _norm_kernel(x_ref, ln_ref, wgk_ref, h_ref, gk_ref):
    h = _rms(x_ref[...], ln_ref[...]).astype(BF16)
    h_ref[...] = h
    gk_ref[...] = _dot_nt(h, wgk_ref[...].astype(BF16))


def _ln_spec(l, nargs):
    if nargs == 1:
        return pl.BlockSpec((None, 1, D_MODEL), lambda i: (l, 0, 0))
    return pl.BlockSpec((None, 1, D_MODEL), lambda i, j: (l, 0, 0))


def _wgk_spec(l, nargs):
    blk = OFF_GK // GATE_RANK
    if nargs == 1:
        return pl.BlockSpec((None, GATE_RANK, D_MODEL), lambda i: (l, blk, 0))
    return pl.BlockSpec((None, GATE_RANK, D_MODEL), lambda i, j: (l, blk, 0))


def _norm(x, ln1, w_in_t, l, tm):
    r = x.shape[0]
    return pl.pallas_call(
        _norm_kernel,
        grid=(r // tm,),
        in_specs=[pl.BlockSpec((tm, D_MODEL), lambda i: (i, 0)), _ln_spec(l, 1), _wgk_spec(l, 1)],
        out_specs=[pl.BlockSpec((tm, D_MODEL), lambda i: (i, 0)), pl.BlockSpec((tm, GATE_RANK), lambda i: (i, 0))],
        out_shape=[jax.ShapeDtypeStruct((r, D_MODEL), BF16), jax.ShapeDtypeStruct((r, GATE_RANK), F32)],
        compiler_params=_cparams(("parallel",)),
        name="norm",
    )(x, ln1, w_in_t)


def _proj_kernel(h_ref, w_ref, o_ref, wb_scr):
    @pl.when(pl.program_id(1) == 0)
    def _():
        wb_scr[...] = w_ref[...].astype(BF16)

    o_ref[...] = _dot_nt(h_ref[...], wb_scr[...])


def _proj_main(h, w_in_t, l, tm):
    r = h.shape[0]
    tn = TN_MAIN
    return pl.pallas_call(
        _proj_kernel,
        grid=(MAIN_COLS // tn, r // tm),
        in_specs=[
            pl.BlockSpec((tm, D_MODEL), lambda j, i: (i, 0)),
            pl.BlockSpec((None, tn, D_MODEL), lambda j, i: (l, j, 0)),
        ],
        out_specs=pl.BlockSpec((tm, tn), lambda j, i: (i, j)),
        out_shape=jax.ShapeDtypeStruct((r, MAIN_COLS), F32),
        scratch_shapes=[pltpu.VMEM((tn, D_MODEL), BF16)],
        compiler_params=_cparams(("arbitrary", "arbitrary")),
        name="proj_main",
    )(h, w_in_t)


def _bucket_ranges():
    d = np.arange(WINDOW)
    max_exact = N_BUCKETS // 2
    df = np.maximum(d, 1).astype(np.float32)
    large = max_exact + (np.log(df / np.float32(max_exact)) / np.float32(math.log(MAX_DISTANCE / max_exact))
                         * np.float32(N_BUCKETS - max_exact)).astype(np.int32)
    large = np.minimum(large, N_BUCKETS - 1)
    bucket = np.where(d < max_exact, d, large)
    ranges = []
    for b in range(N_BUCKETS):
        idx = np.nonzero(bucket == b)[0]
        if idx.size:
            assert idx[-1] - idx[0] + 1 == idx.size
            ranges.append((b, int(idx[0]), int(idx[-1])))
    return ranges


def _bias_kernel(rel_ref, d_ref, o_ref):
    h = pl.program_id(0)
    d = d_ref[...]
    val = jnp.full(d.shape, NEG_BIG, F32)
    for b, lo, hi in _bucket_ranges():
        val = jnp.where((d >= lo) & (d <= hi), rel_ref[b, h] * LOG2E, val)
    o_ref[...] = val


def _bias_table(rel_bias, dmap):
    mq, nk = dmap.shape
    return pl.pallas_call(
        _bias_kernel,
        grid=(N_HEADS_A,),
        in_specs=[
            pl.BlockSpec(memory_space=pltpu.SMEM),
            pl.BlockSpec((mq, nk), lambda h: (0, 0)),
        ],
        out_specs=pl.BlockSpec((None, mq, nk), lambda h: (h, 0, 0)),
        out_shape=jax.ShapeDtypeStruct((N_HEADS_A, mq, nk), F32),
        compiler_params=_cparams(("parallel",)),
        name="bias_table",
    )(rel_bias, dmap)


def _dmap_prompt(first_block):
    i = np.arange(WINDOW)[:, None]
    j = np.arange(2 * WINDOW)[None, :]
    d = i + WINDOW - j
    ok = (d >= 0) & (d < WINDOW)
    if first_block:
        ok = ok & (j >= WINDOW)
    return np.where(ok, d, -1).astype(np.int32)


def _dmap_sample(t_new, group):
    nk = group * WINDOW + LANES
    rows = np.arange(group * t_new)
    rb, rt = rows // t_new, rows % t_new
    d = np.full((group * t_new, nk), -1, np.int64)
    cols = np.arange(group * WINDOW)
    cb, cs = cols // WINDOW, cols % WINDOW
    dw = rt[:, None] + WINDOW - cs[None, :]
    ok = (rb[:, None] == cb[None, :]) & (dw >= 0) & (dw < WINDOW)
    d[:, :group * WINDOW] = np.where(ok, dw, -1)
    ncols = np.arange(group * t_new)
    nb, nu = ncols // t_new, ncols % t_new
    dn = rt[:, None] - nu[None, :]
    okn = (rb[:, None] == nb[None, :]) & (dn >= 0)
    d[:, group * WINDOW:group * WINDOW + group * t_new] = np.where(okn, dn, -1)
    return d.astype(np.int32)


def _group_ones():
    r = lax.broadcasted_iota(jnp.int32, (LANES, LANES), 0)
    c = lax.broadcasted_iota(jnp.int32, (LANES, LANES), 1)
    low_c = jnp.where(c < HEAD_DIM, 1.0, 0.0)
    return jnp.where(r < HEAD_DIM, low_c, 1.0 - low_c).astype(BF16)


def _head_norm(x, gmat, ln2):
    sq = x * x
    hi = sq.astype(BF16)
    lo = (sq - hi.astype(F32)).astype(BF16)
    ms = (_dot(hi, gmat) + _dot(lo, gmat)) * (1.0 / HEAD_DIM)
    return x * lax.rsqrt(ms + EPS) * ln2


def _split_heads(x):
    lane = lax.broadcasted_iota(jnp.int32, (x.shape[0], LANES), 1)
    low = lane < HEAD_DIM
    lo_parts, hi_parts = [], []
    for c in range(KV_WIDTH_A // LANES):
        xc = x[:, c * LANES:(c + 1) * LANES]
        xr = pltpu.roll(xc, HEAD_DIM, 1)
        lo_parts += [jnp.where(low, xc, 0.0).astype(BF16), jnp.where(low, xr, 0.0).astype(BF16)]
        hi_parts += [jnp.where(low, 0.0, xr).astype(BF16), jnp.where(low, 0.0, xc).astype(BF16)]
    return lo_parts, hi_parts


def _attend(q_cols, k_parts, v_parts, bias_ref, sink_ref):
    mq = q_cols[0].shape[0]
    pairs_per_kv = N_HEADS_A // N_KV_A // 2
    assert pairs_per_kv == 2
    first_rows = lax.broadcasted_iota(jnp.int32, (pairs_per_kv * mq, 1), 0) < mq
    outs = [None] * (N_HEADS_A // 2)
    for kh in range(N_KV_A):
        pcs = [kh * pairs_per_kv + p for p in range(pairs_per_kv)]
        q_stack = jnp.concatenate([q_cols[pc] for pc in pcs], axis=0)
        for half in range(2):
            heads = [2 * pc + half for pc in pcs]
            bias = jnp.concatenate([bias_ref[h] for h in heads], axis=0)
            sink = jnp.where(first_rows, sink_ref[heads[0]], sink_ref[heads[1]]) * LOG2E
            s = _dot_nt(q_stack, k_parts[half][kh]) + bias
            m = jnp.maximum(jnp.max(s, axis=-1, keepdims=True), sink)
            e = jnp.exp2(s - m)
            den = jnp.sum(e, axis=-1, keepdims=True) + jnp.exp2(sink - m)
            o = _dot(e.astype(BF16), v_parts[half][kh]) * (1.0 / den)
            for p, pc in enumerate(pcs):
                o_p = o[p * mq:(p + 1) * mq, :]
                outs[pc] = o_p if outs[pc] is None else outs[pc] + o_p
    return outs


def _gates_attn_kernel(blocks_per_seq, sink_ref, h_ref, hs_ref, w_ref, q_ref, k_ref, v_ref, lnq_ref, lnk_ref,
                       bias_ref, g_ref, gs_ref, o_ref, kn_ref, wb_scr, kprev_scr, vprev_scr, o_scr, kn_scr):
    j = pl.program_id(0)
    i = pl.program_id(1)
    step = j * pl.num_programs(1) + i

    @pl.when(i == 0)
    def _():
        wb_scr[...] = w_ref[...].astype(BF16)

    @pl.when(i == pl.num_programs(1) - 1)
    def _():
        gs_ref[...] = jax.nn.sigmoid(_dot_nt(hs_ref[...], wb_scr[...])).astype(gs_ref.dtype)

    @pl.when(step == 0)
    def _():
        kprev_scr[...] = jnp.zeros_like(kprev_scr)
        vprev_scr[...] = jnp.zeros_like(vprev_scr)

    gmat = _group_ones()
    lnq = lnq_ref[...]
    lnk = lnk_ref[...]
    scale = HEAD_DIM ** -0.5 * LOG2E
    for t in range(ATTN_BLOCKS_PER_STEP):
        rows = slice(t * WINDOW, (t + 1) * WINDOW)
        block = step * ATTN_BLOCKS_PER_STEP + t
        table = bias_ref.at[jnp.where(block % blocks_per_seq == 0, 0, 1)]
        kn_own = jnp.concatenate(
            [_head_norm(k_ref[rows, c * LANES:(c + 1) * LANES], gmat, lnk) for c in range(KV_WIDTH_A // LANES)],
            axis=1)
        kn_scr[rows, :] = kn_own
        k_own = _split_heads(kn_own)
        v_own = _split_heads(v_ref[rows, :])
        k_parts = [[jnp.concatenate([kprev_scr[half, kh], k_own[half][kh]], axis=0) for kh in range(N_KV_A)]
                   for half in range(2)]
        v_parts = [[jnp.concatenate([vprev_scr[half, kh], v_own[half][kh]], axis=0) for kh in range(N_KV_A)]
                   for half in range(2)]
        q_cols = [(_head_norm(q_ref[rows, c * LANES:(c + 1) * LANES], gmat, lnq) * scale).astype(BF16)
                  for c in range(WIDTH_A // LANES)]
        outs = _attend(q_cols, k_parts, v_parts, table, sink_ref)
        for c, o in enumerate(outs):
            o_scr[rows, c * LANES:(c + 1) * LANES] = o.astype(o_scr.dtype)
        for half in range(2):
            for kh in range(N_KV_A):
                kprev_scr[half, kh] = k_own[half][kh]
                vprev_scr[half, kh] = v_own[half][kh]

    g_ref[...] = jax.nn.sigmoid(_dot_nt(h_ref[...], wb_scr[...])).astype(g_ref.dtype)
    o_ref[...] = o_scr[...]
    kn_ref[...] = kn_scr[...]


def _gates_attn(h, hs, w_in_t, proj, sinks_l, lnq2, lnk2, bias_p, l, tm, seq):
    r = h.shape[0]
    rs = hs.shape[0]
    tn = TN_GATES
    nj, nt = GATE_COLS // tn, r // tm
    rows = ATTN_BLOCKS_PER_STEP * WINDOW
    assert nj * nt * rows == r
    kcol = OFF_KA // KV_WIDTH_A
    vcol = OFF_VA // KV_WIDTH_A
    step = lambda j, i: j * nt + i
    return pl.pallas_call(
        functools.partial(_gates_attn_kernel, seq // WINDOW),
        grid=(nj, nt),
        in_specs=[
            pl.BlockSpec(memory_space=pltpu.SMEM),
            pl.BlockSpec((tm, D_MODEL), lambda j, i: (i, 0)),
            pl.BlockSpec((rs, D_MODEL), lambda j, i: (0, 0)),
            pl.BlockSpec((pl.Squeezed(), pl.Element(tn), pl.Element(D_MODEL)),
                         lambda j, i: (l, pl.multiple_of(OFF_GA + j * tn, GATE_RANK), 0)),
            pl.BlockSpec((rows, WIDTH_A), lambda j, i: (step(j, i), 0)),
            pl.BlockSpec((rows, KV_WIDTH_A), lambda j, i: (step(j, i), kcol)),
            pl.BlockSpec((rows, KV_WIDTH_A), lambda j, i: (step(j, i), vcol)),
            pl.BlockSpec((1, LANES), lambda j, i: (0, 0)),
            pl.BlockSpec((1, LANES), lambda j, i: (0, 0)),
            pl.BlockSpec((2, N_HEADS_A, WINDOW, 2 * WINDOW), lambda j, i: (0, 0, 0, 0)),
        ],
        out_specs=[
            pl.BlockSpec((tm, tn), lambda j, i: (i, j)),
            pl.BlockSpec((rs, tn), lambda j, i: (0, j)),
            pl.BlockSpec((rows, WIDTH_A), lambda j, i: (step(j, i), 0)),
            pl.BlockSpec((rows, KV_WIDTH_A), lambda j, i: (step(j, i), 0)),
        ],
        out_shape=[
            jax.ShapeDtypeStruct((r, GATE_COLS), BF16),
            jax.ShapeDtypeStruct((rs, GATE_COLS), BF16),
            jax.ShapeDtypeStruct((r, WIDTH_A), BF16),
            jax.ShapeDtypeStruct((r, KV_WIDTH_A), F32),
        ],
        scratch_shapes=[
            pltpu.VMEM((tn, D_MODEL), BF16),
            pltpu.VMEM((2, N_KV_A, WINDOW, LANES), BF16),
            pltpu.VMEM((2, N_KV_A, WINDOW, LANES), BF16),
            pltpu.VMEM((rows, WIDTH_A), BF16),
            pltpu.VMEM((rows, KV_WIDTH_A), F32),
        ],
        compiler_params=_cparams(("arbitrary", "arbitrary")),
        name="gates_attn",
    )(sinks_l, h, hs, w_in_t, proj, proj, proj, lnq2, lnk2, bias_p)


def _attn_sample_kernel(sink_ref, q_ref, kn_in_ref, vn_ref, wk_ref, wv_ref, lnq_ref, lnk_ref, bias_ref,
                        o_ref, kn_ref):
    gmat = _group_ones()
    lnq = lnq_ref[...]
    lnk = lnk_ref[...]
    scale = HEAD_DIM ** -0.5 * LOG2E
    rows = q_ref.shape[0]
    g = wk_ref.shape[0]
    kn_new = jnp.concatenate(
        [_head_norm(kn_in_ref[:, c * LANES:(c + 1) * LANES], gmat, lnk) for c in range(KV_WIDTH_A // LANES)], axis=1)
    kn_ref[...] = kn_new
    pad = jnp.zeros((LANES - rows, KV_WIDTH_A), F32)

    def window_rows(ref):
        heads_per_col = LANES // HEAD_DIM
        cols = []
        for c in range(KV_WIDTH_A // LANES):
            blocks = [jnp.concatenate([ref[b, c * heads_per_col + j] for j in range(heads_per_col)], axis=0).T
                      for b in range(g)]
            cols.append(jnp.concatenate(blocks, axis=0))
        return jnp.concatenate(cols, axis=1)

    k = jnp.concatenate([window_rows(wk_ref), kn_new, pad], axis=0)
    v = jnp.concatenate([window_rows(wv_ref), vn_ref[...], pad], axis=0)
    q_cols = [(_head_norm(q_ref[:, c * LANES:(c + 1) * LANES], gmat, lnq) * scale).astype(BF16)
              for c in range(WIDTH_A // LANES)]
    outs = _attend(q_cols, _split_heads(k), _split_heads(v), bias_ref, sink_ref)
    for c, o in enumerate(outs):
        o_ref[:, c * LANES:(c + 1) * LANES] = o.astype(o_ref.dtype)


def _attn_sample(proj, win_k, win_v, sinks_l, lnq2, lnk2, bias_s, l, batch, t_new):
    g = SAMPLE_GROUP_A
    rows = g * t_new
    nk = g * WINDOW + LANES
    kcol = OFF_KA // KV_WIDTH_A
    vcol = OFF_VA // KV_WIDTH_A
    return pl.pallas_call(
        _attn_sample_kernel,
        grid=(batch // g,),
        in_specs=[
            pl.BlockSpec(memory_space=pltpu.SMEM),
            pl.BlockSpec((rows, WIDTH_A), lambda i: (i, 0)),
            pl.BlockSpec((rows, KV_WIDTH_A), lambda i: (i, kcol)),
            pl.BlockSpec((rows, KV_WIDTH_A), lambda i: (i, vcol)),
            pl.BlockSpec((None, g, N_KV_A, HEAD_DIM, WINDOW), lambda i: (l, i, 0, 0, 0)),
            pl.BlockSpec((None, g, N_KV_A, HEAD_DIM, WINDOW), lambda i: (l, i, 0, 0, 0)),
            pl.BlockSpec((1, LANES), lambda i: (0, 0)),
            pl.BlockSpec((1, LANES), lambda i: (0, 0)),
            pl.BlockSpec((N_HEADS_A, rows, nk), lambda i: (0, 0, 0)),
        ],
        out_specs=[
            pl.BlockSpec((rows, WIDTH_A), lambda i: (i, 0)),
            pl.BlockSpec((rows, KV_WIDTH_A), lambda i: (i, 0)),
        ],
        out_shape=[
            jax.ShapeDtypeStruct((batch * t_new, WIDTH_A), BF16),
            jax.ShapeDtypeStruct((batch * t_new, KV_WIDTH_A), F32),
        ],
        compiler_params=_cparams(("parallel",)),
        name="attn_sample",
    )(sinks_l, proj, proj, proj, win_k, win_v, lnq2, lnk2, bias_s)


def _log_decay(gk, w2_ref, bgk_ref):
    x = _dot(gk.astype(BF16), w2_ref[...]) + bgk_ref[...]
    log_sig = jnp.minimum(x, 0.0) - jnp.log(1.0 + jnp.exp(-jnp.abs(x)))
    return log_sig * (1.0 / GATE_TAU)


def _column(row):
    r = lax.broadcasted_iota(jnp.int32, (LANES, LANES), 0)
    c = lax.broadcasted_iota(jnp.int32, (LANES, LANES), 1)
    return jnp.sum(jnp.where(r == c, row, 0.0), axis=1, keepdims=True)


def _gla_out(o, ln_o, rgate):
    return (_rms(o, ln_o) * (rgate * jax.nn.sigmoid(rgate))).astype(BF16)


def _gla_prompt_kernel(q_ref, k_ref, va_ref, vb_ref, ra_ref, rb_ref, gk_ref, w2_ref, bgk_ref, lno_ref,
                       o_ref, s_out_ref, s_scr, b_scr):
    t = pl.program_id(1)
    c_len = GLA_CHUNK_P
    tb = q_ref.shape[0]
    n_chunks = tb // c_len
    half = c_len // 2
    scale = DK_B ** -0.5

    @pl.when(t == 0)
    def _():
        s_scr[...] = jnp.zeros_like(s_scr)

    shift = c_len.bit_length() - 1
    rb = lax.broadcasted_iota(jnp.int32, (tb, tb), 0)
    cb = lax.broadcasted_iota(jnp.int32, (tb, tb), 1)
    same_chunk = lax.shift_right_logical(rb, shift) == lax.shift_right_logical(cb, shift)
    tri_blocks = jnp.where(same_chunk & (rb >= cb), 1.0, 0.0).astype(BF16)
    b_scr[...] = _dot_exact_lhs01(tri_blocks, _log_decay(gk_ref[...], w2_ref, bgk_ref))

    ri = lax.broadcasted_iota(jnp.int32, (c_len, c_len), 0)
    ci = lax.broadcasted_iota(jnp.int32, (c_len, c_len), 1)
    quarter = half // 2
    same_half = jnp.where(ci < half, jnp.where(ri < half, 1, 0), jnp.where(ri < half, 0, 1))
    own_causal = (ri >= ci) & (same_half == 1)
    upper_row = lax.broadcasted_iota(jnp.int32, (c_len, 1), 0) >= half
    ln_o = lno_ref[...]
    v_refs = (va_ref, vb_ref)
    r_refs = (ra_ref, rb_ref)

    def chunk(c, carry):
        rows = pl.ds(pl.multiple_of(c * c_len, c_len), c_len)
        for h in range(N_HEADS_B):
            ks = slice(h * DK_B, (h + 1) * DK_B)
            vs = slice((h % 2) * DV_B, (h % 2 + 1) * DV_B)
            bh = b_scr[rows, ks]
            b_last = bh[c_len - 1:c_len, :]
            b_half = bh[half - 1:half, :]
            q = q_ref[rows, ks] * scale
            k = k_ref[rows, ks]
            v = v_refs[h // 2][rows, vs].astype(BF16)
            q_inter = (q * jnp.exp(bh)).astype(BF16)
            r_own = jnp.where(upper_row, bh[half + quarter - 1:half + quarter, :], bh[quarter - 1:quarter, :])
            a_own = _dot_nt((q * jnp.exp(bh - r_own)).astype(BF16), (k * jnp.exp(r_own - bh)).astype(BF16))
            q_x = jnp.where(upper_row, q * jnp.exp(bh - b_half), 0.0).astype(BF16)
            k_x = jnp.where(upper_row, 0.0, k * jnp.exp(b_half - bh)).astype(BF16)
            a = (jnp.where(own_causal, a_own, 0.0) + _dot_nt(q_x, k_x)).astype(BF16)
            k_state = (k * jnp.exp(b_last - bh)).astype(BF16)
            s_old = s_scr[h]
            o = _dot(a, v) + _dot(q_inter, s_old.astype(BF16))
            s_scr[h] = _column(jnp.exp(b_last)) * s_old + _dot_tn(k_state, v)
            o_ref[rows, h * DV_B:(h + 1) * DV_B] = _gla_out(o, ln_o, r_refs[h // 2][rows, vs])
        return carry

    lax.fori_loop(0, n_chunks, chunk, 0, unroll=4)

    @pl.when(t == pl.num_programs(1) - 1)
    def _():
        s_out_ref[...] = s_scr[...]


def _gla_prompt(proj, gk, w2_l, bgk_l, lno_l, batch, seq):
    tb = GLA_TBLOCK
    nt = seq // tb
    w = 2 * DV_B
    row = lambda b, t: b * nt + t
    spec = lambda col: pl.BlockSpec((tb, w), lambda b, t: (row(b, t), col))
    return pl.pallas_call(
        _gla_prompt_kernel,
        grid=(batch, nt),
        in_specs=[
            spec(OFF_QB // w), spec(OFF_KB // w),
            spec(OFF_VB // w), spec(OFF_VB // w + 1),
            spec(OFF_RB // w), spec(OFF_RB // w + 1),
            pl.BlockSpec((tb, GATE_RANK), lambda b, t: (row(b, t), 0)),
            pl.BlockSpec((GATE_RANK, WIDTH_BK), lambda b, t: (0, 0)),
            pl.BlockSpec((1, WIDTH_BK), lambda b, t: (0, 0)),
            pl.BlockSpec((1, DV_B), lambda b, t: (0, 0)),
        ],
        out_specs=[
            pl.BlockSpec((tb, WIDTH_BV), lambda b, t: (row(b, t), 0)),
            pl.BlockSpec((None, N_HEADS_B, DK_B, DV_B), lambda b, t: (b, 0, 0, 0)),
        ],
        out_shape=[
            jax.ShapeDtypeStruct((batch * seq, WIDTH_BV), BF16),
            jax.ShapeDtypeStruct((batch, N_HEADS_B, DK_B, DV_B), F32),
        ],
        scratch_shapes=[pltpu.VMEM((N_HEADS_B, DK_B, DV_B), F32), pltpu.VMEM((tb, WIDTH_BK), F32)],
        compiler_params=_cparams(("parallel", "arbitrary")),
        name="gla_prompt",
    )(proj, proj, proj, proj, proj, proj, gk, w2_l, bgk_l, lno_l)


def _gla_sample_kernel(t_new, q_ref, k_ref, va_ref, vb_ref, ra_ref, rb_ref, gk_ref, w2_ref, bgk_ref, lno_ref,
                       s_ref, o_ref, s_out_ref):
    rows = q_ref.shape[0]
    n_seq = rows // t_new
    scale = DK_B ** -0.5
    ri = lax.broadcasted_iota(jnp.int32, (rows, rows), 0)
    ci = lax.broadcasted_iota(jnp.int32, (rows, rows), 1)
    same = None
    for sq in range(n_seq):
        lo, hi = sq * t_new, (sq + 1) * t_new
        blk = (ri >= lo) & (ri < hi) & (ci >= lo) & (ci < hi)
        same = blk if same is None else (same | blk)
    causal = same & (ri >= ci)
    tri = jnp.where(causal, 1.0, 0.0).astype(BF16)
    rcol = lax.broadcasted_iota(jnp.int32, (rows, 1), 0)
    ln_o = lno_ref[...]
    v_refs = (va_ref, vb_ref)
    r_refs = (ra_ref, rb_ref)

    g = _log_decay(gk_ref[...], w2_ref, bgk_ref)
    b = _dot_exact_lhs01(tri, g)
    for h in range(N_HEADS_B):
        ks = slice(h * DK_B, (h + 1) * DK_B)
        vs = slice((h % 2) * DV_B, (h % 2 + 1) * DV_B)
        bh = b[:, ks]
        gh = g[:, ks]
        q = q_ref[:, ks] * scale
        k = k_ref[:, ks]
        v = v_refs[h // 2][:, vs].astype(BF16)
        q_inter = q * jnp.exp(bh)
        k_t = (k * jnp.exp(-bh)).astype(BF16)
        a = jnp.where(causal, _dot_nt(q_inter.astype(BF16), k_t), 0.0).astype(BF16)
        o = _dot(a, v)
        for sq in range(n_seq):
            mine = (rcol >= sq * t_new) & (rcol < (sq + 1) * t_new)
            b_last = jnp.sum(jnp.where(mine, gh, 0.0), axis=0, keepdims=True)
            k_state = jnp.where(mine, k * jnp.exp(b_last - bh), 0.0).astype(BF16)
            q_mine = jnp.where(mine, q_inter, 0.0).astype(BF16)
            s_old = s_ref[sq, h]
            o = o + _dot(q_mine, s_old.astype(BF16))
            s_out_ref[sq, h] = _column(jnp.exp(b_last)) * s_old + _dot_tn(k_state, v)
        o_ref[:, h * DV_B:(h + 1) * DV_B] = _gla_out(o, ln_o, r_refs[h // 2][:, vs])


def _gla_sample(proj, gk, w2_l, bgk_l, lno_l, state, l, batch, t_new):
    g = SAMPLE_GROUP_B
    rows = g * t_new
    w = 2 * DV_B
    spec = lambda col: pl.BlockSpec((rows, w), lambda i: (i, col))
    return pl.pallas_call(
        functools.partial(_gla_sample_kernel, t_new),
        grid=(batch // g,),
        in_specs=[
            spec(OFF_QB // w), spec(OFF_KB // w),
            spec(OFF_VB // w), spec(OFF_VB // w + 1),
            spec(OFF_RB // w), spec(OFF_RB // w + 1),
            pl.BlockSpec((rows, GATE_RANK), lambda i: (i, 0)),
            pl.BlockSpec((GATE_RANK, WIDTH_BK), lambda i: (0, 0)),
            pl.BlockSpec((1, WIDTH_BK), lambda i: (0, 0)),
            pl.BlockSpec((1, DV_B), lambda i: (0, 0)),
            pl.BlockSpec((None, g, N_HEADS_B, DK_B, DV_B), lambda i: (l, i, 0, 0, 0)),
        ],
        out_specs=[
            pl.BlockSpec((rows, WIDTH_BV), lambda i: (i, 0)),
            pl.BlockSpec((g, N_HEADS_B, DK_B, DV_B), lambda i: (i, 0, 0, 0)),
        ],
        out_shape=[
            jax.ShapeDtypeStruct((batch * t_new, WIDTH_BV), BF16),
            jax.ShapeDtypeStruct((batch, N_HEADS_B, DK_B, DV_B), F32),
        ],
        compiler_params=_cparams(("parallel",)),
        name="gla_sample",
    )(proj, proj, proj, proj, proj, proj, gk, w2_l, bgk_l, lno_l, state)


def _merge_rows(oa_ref, ob_ref, sa_ref, sb_ref, x_ref, woa_ref, wob_ref, wout_ref, ln_ref, x_out_ref, h_ref):
    a = _dot(oa_ref[...], woa_ref[...])
    b = _dot(ob_ref[...], wob_ref[...])
    mix = (sa_ref[...].astype(F32) * a + sb_ref[...].astype(F32) * b).astype(BF16)
    x_new = x_ref[...] + _dot(mix, wout_ref[...])
    x_out_ref[...] = x_new
    h_ref[...] = _rms(x_new, ln_ref[...]).astype(BF16)


def _merge_kernel(oa_ref, ob_ref, sa_ref, sb_ref, x_ref, oa_s_ref, ob_s_ref, sa_s_ref, sb_s_ref, x_s_ref,
                  woa_ref, wob_ref, wout_ref, ln_ref, x_out_ref, h_ref, x_s_out_ref, h_s_ref):
    _merge_rows(oa_ref, ob_ref, sa_ref, sb_ref, x_ref, woa_ref, wob_ref, wout_ref, ln_ref, x_out_ref, h_ref)

    @pl.when(pl.program_id(0) == pl.num_programs(0) - 1)
    def _():
        _merge_rows(oa_s_ref, ob_s_ref, sa_s_ref, sb_s_ref, x_s_ref, woa_ref, wob_ref, wout_ref, ln_ref,
                    x_s_out_ref, h_s_ref)


def _merge(prompt, sample, w_oa, w_ob, w_out, ln2, l, tm):
    r = prompt[3].shape[0]
    rs = sample[3].shape[0]
    resident = dict(pipeline_mode=pl.Buffered(1))
    row = lambda i: (i, 0)
    fixed = lambda i: (0, 0)
    group_specs = lambda rows, at, at1: [
        pl.BlockSpec((rows, WIDTH_A), at),
        pl.BlockSpec((rows, WIDTH_BV), at),
        pl.BlockSpec((rows, D_MODEL), at),
        pl.BlockSpec((rows, D_MODEL), at1),
        pl.BlockSpec((rows, D_MODEL), at),
    ]
    operands = lambda g: (g[0], g[1], g[2], g[2], g[3])
    return pl.pallas_call(
        _merge_kernel,
        grid=(r // tm,),
        in_specs=group_specs(tm, row, lambda i: (i, 1)) + group_specs(rs, fixed, lambda i: (0, 1)) + [
            pl.BlockSpec((None, WIDTH_A, D_MODEL), lambda i: (l, 0, 0), **resident),
            pl.BlockSpec((None, WIDTH_BV, D_MODEL), lambda i: (l, 0, 0), **resident),
            pl.BlockSpec((None, D_MODEL, D_MODEL), lambda i: (l, 0, 0), **resident),
            _ln_spec(l, 1),
        ],
        out_specs=[pl.BlockSpec((tm, D_MODEL), row), pl.BlockSpec((tm, D_MODEL), row),
                   pl.BlockSpec((rs, D_MODEL), fixed), pl.BlockSpec((rs, D_MODEL), fixed)],
        out_shape=[jax.ShapeDtypeStruct((r, D_MODEL), F32), jax.ShapeDtypeStruct((r, D_MODEL), BF16),
                   jax.ShapeDtypeStruct((rs, D_MODEL), F32), jax.ShapeDtypeStruct((rs, D_MODEL), BF16)],
        compiler_params=_cparams(("arbitrary",)),
        name="merge",
    )(*operands(prompt), *operands(sample), w_oa, w_ob, w_out, ln2)


def _gelu_tanh(x):
    return x * (0.5 * (1.0 + jnp.tanh(math.sqrt(2.0 / math.pi) * (x + 0.044715 * (x * x * x)))))


def _conv3(u, p1, p2, w_ref, b_ref):
    return w_ref[0:1, :] * p2 + w_ref[1:2, :] * p1 + w_ref[2:3, :] * u + b_ref[...]


def _history_selectors(n_seq, t_new):
    hist = CONV_W - 1
    e1 = np.zeros((n_seq * t_new, n_seq * hist), np.float32)
    e2 = np.zeros((n_seq * t_new, n_seq * hist), np.float32)
    for b in range(n_seq):
        for t in range(min(t_new, hist)):
            if t < 1:
                e1[b * t_new + t, b * hist + hist - 1 + t] = 1.0
            e2[b * t_new + t, b * hist + t] = 1.0
    return jnp.asarray(e1, BF16), jnp.asarray(e2, BF16)


def _ffn_up_prompt_kernel(tiles_per_seq, h_ref, wv_ref, wg_ref, cv_ref, cg_ref, bv_ref, bg_ref,
                          act_ref, tv_ref, tg_ref, wvb_scr, wgb_scr, carry_v, carry_g):
    i = pl.program_id(1)

    @pl.when(i == 0)
    def _():
        wvb_scr[...] = wv_ref[...].astype(BF16)
        wgb_scr[...] = wg_ref[...].astype(BF16)

    h = h_ref[...]
    tm = h.shape[0]
    first = (i % tiles_per_seq) == 0
    row8 = lax.broadcasted_iota(jnp.int32, (8, 1), 0)

    def branch(wb_scr, carry, c_ref, b_ref, tail_ref):
        u = _dot(h, wb_scr[...])
        prev = jnp.where(first, 0.0, carry[...])
        tail = u[tm - 8:tm, :]
        carry[...] = tail
        tail_ref[...] = tail
        p1 = pltpu.roll(u, 1, 0)
        p2 = pltpu.roll(u, 2, 0)
        body = _conv3(u, p1, p2, c_ref, b_ref)
        p1_top = jnp.where(row8 < 1, pltpu.roll(prev, 1, 0), p1[0:8, :])
        p2_top = jnp.where(row8 < 2, pltpu.roll(prev, 2, 0), p2[0:8, :])
        top = _conv3(u[0:8, :], p1_top, p2_top, c_ref, b_ref)
        return jnp.concatenate([top, body[8:, :]], axis=0)

    gate = _gelu_tanh(branch(wgb_scr, carry_g, cg_ref, bg_ref, tg_ref))
    val = branch(wvb_scr, carry_v, cv_ref, bv_ref, tv_ref)
    act_ref[...] = (gate * val).astype(act_ref.dtype)


def _ffn_up_sample_kernel(pos_ref, e1_ref, e2_ref, h_ref, wv_ref, wg_ref, cv_ref, cg_ref, bv_ref, bg_ref,
                          sv_ref, sg_ref, act_ref, uv_ref, ug_ref):
    h = h_ref[...]
    pos = pos_ref[...]
    e1 = e1_ref[...]
    e2 = e2_ref[...]

    def branch(w_ref, c_ref, b_ref, s_ref, u_ref):
        u = _dot(h, w_ref[...].astype(BF16))
        u_ref[...] = u
        cached = s_ref[...]
        p1 = jnp.where(pos >= 1, pltpu.roll(u, 1, 0), _dot_exact_lhs01(e1, cached))
        p2 = jnp.where(pos >= 2, pltpu.roll(u, 2, 0), _dot_exact_lhs01(e2, cached))
        return _conv3(u, p1, p2, c_ref, b_ref)

    val = branch(wv_ref, cv_ref, bv_ref, sv_ref, uv_ref)
    gate = branch(wg_ref, cg_ref, bg_ref, sg_ref, ug_ref)
    act_ref[...] = (_gelu_tanh(gate) * val).astype(act_ref.dtype)


def _ffn_up_kernel(tiles_per_seq, pos_ref, e1_ref, e2_ref, h_ref, hs_ref, wv_ref, wg_ref, cv_ref, cg_ref,
                   bv_ref, bg_ref, sv_ref, sg_ref,
                   act_ref, tv_ref, tg_ref, acts_ref, uv_ref, ug_ref, wvb_scr, wgb_scr, carry_v, carry_g):
    _ffn_up_prompt_kernel(tiles_per_seq, h_ref, wv_ref, wg_ref, cv_ref, cg_ref, bv_ref, bg_ref,
                          act_ref, tv_ref, tg_ref, wvb_scr, wgb_scr, carry_v, carry_g)

    @pl.when(pl.program_id(1) == pl.num_programs(1) - 1)
    def _():
        _ffn_up_sample_kernel(pos_ref, e1_ref, e2_ref, hs_ref, wvb_scr, wgb_scr, cv_ref, cg_ref, bv_ref, bg_ref,
                              sv_ref, sg_ref, acts_ref, uv_ref, ug_ref)


def _ffn_up(h, hs, w_up, w_conv, b_conv, conv_state, l, tm, tiles_per_seq, t_new):
    r = h.shape[0]
    rs = hs.shape[0]
    tf = TF_UP
    nt = r // tm
    nj = D_FF // tf
    ns = conv_state.shape[1]
    w_spec = lambda off: pl.BlockSpec((None, D_MODEL, tf), lambda j, i: (l, 0, j + off))
    c_spec = lambda off: pl.BlockSpec((None, CONV_W, tf), lambda j, i: (l, 0, j + off))
    b_spec = lambda off: pl.BlockSpec((None, 1, tf), lambda j, i: (l, 0, j + off))
    s_spec = lambda off: pl.BlockSpec((None, ns, tf), lambda j, i: (l, 0, j + off))
    tail_spec = pl.BlockSpec((None, 8, tf), lambda j, i: (i, 0, j))
    u_spec = pl.BlockSpec((rs, tf), lambda j, i: (0, j))
    fixed = lambda shape: pl.BlockSpec(shape, lambda j, i: (0, 0))
    pos = jnp.asarray((np.arange(rs) % t_new).astype(np.int32).reshape(rs, 1))
    e1, e2 = _history_selectors(rs // t_new, t_new)
    return pl.pallas_call(
        functools.partial(_ffn_up_kernel, tiles_per_seq),
        grid=(nj, nt),
        in_specs=[
            fixed((rs, 1)), fixed((rs, ns)), fixed((rs, ns)),
            pl.BlockSpec((tm, D_MODEL), lambda j, i: (i, 0)),
            fixed((rs, D_MODEL)),
            w_spec(0), w_spec(nj), c_spec(0), c_spec(nj), b_spec(0), b_spec(nj), s_spec(0), s_spec(nj),
        ],
        out_specs=[
            pl.BlockSpec((None, tm, tf), lambda j, i: (j, i, 0)), tail_spec, tail_spec,
            pl.BlockSpec((None, rs, tf), lambda j, i: (j, 0, 0)), u_spec, u_spec,
        ],
        out_shape=[
            jax.ShapeDtypeStruct((nj, r, tf), BF16),
            jax.ShapeDtypeStruct((nt, 8, D_FF), F32),
            jax.ShapeDtypeStruct((nt, 8, D_FF), F32),
            jax.ShapeDtypeStruct((nj, rs, tf), BF16),
            jax.ShapeDtypeStruct((rs, D_FF), F32),
            jax.ShapeDtypeStruct((rs, D_FF), F32),
        ],
        scratch_shapes=[
            pltpu.VMEM((D_MODEL, tf), BF16),
            pltpu.VMEM((D_MODEL, tf), BF16),
            pltpu.VMEM((8, tf), F32),
            pltpu.VMEM((8, tf), F32),
        ],
        compiler_params=_cparams(("arbitrary", "arbitrary")),
        name="ffn_up",
    )(pos, e1, e2, h, hs, w_up, w_up, w_conv, w_conv, b_conv, b_conv, conv_state, conv_state)


def _down_accumulate(act_ref, w_ref, x_ref, x_out_ref):
    @pl.when(pl.program_id(1) == 0)
    def _():
        x_out_ref[...] = x_ref[...]

    act = act_ref[...]
    step = 512
    for n in range(D_MODEL // step):
        cols = slice(n * step, (n + 1) * step)
        x_out_ref[:, cols] += _dot(act, w_ref[:, cols])


def _down_last_kernel(act_ref, w_ref, x_ref, x_out_ref):
    _down_accumulate(act_ref, w_ref, x_ref, x_out_ref)


def _down_norm_kernel(act_ref, w_ref, x_ref, ln_ref, wgk_ref, x_out_ref, h_ref, gk_ref):
    _down_accumulate(act_ref, w_ref, x_ref, x_out_ref)

    @pl.when(pl.program_id(1) == pl.num_programs(1) - 1)
    def _():
        h = _rms(x_out_ref[...], ln_ref[...]).astype(BF16)
        h_ref[...] = h
        gk_ref[...] = _dot_nt(h, wgk_ref[...].astype(BF16))


def _down(act, w_down, x, ln1, w_in_t, l, tm, with_norm):
    nk, r, tk = act.shape
    row = lambda i, k: (i, 0)
    in_specs = [
        pl.BlockSpec((None, tm, tk), lambda i, k: (k, i, 0)),
        pl.BlockSpec((None, tk, D_MODEL), lambda i, k: (l, k, 0)),
        pl.BlockSpec((tm, D_MODEL), row),
    ]
    out_specs = [pl.BlockSpec((tm, D_MODEL), row)]
    out_shape = [jax.ShapeDtypeStruct((r, D_MODEL), F32)]
    args = [act, w_down, x]
    if with_norm:
        in_specs += [_ln_spec(l + 1, 2), _wgk_spec(l + 1, 2)]
        out_specs += [pl.BlockSpec((tm, D_MODEL), row), pl.BlockSpec((tm, GATE_RANK), row)]
        out_shape += [jax.ShapeDtypeStruct((r, D_MODEL), BF16), jax.ShapeDtypeStruct((r, GATE_RANK), F32)]
        args += [ln1, w_in_t]
    out = pl.pallas_call(
        _down_norm_kernel if with_norm else _down_last_kernel,
        grid=(r // tm, nk),
        in_specs=in_specs,
        out_specs=out_specs,
        out_shape=out_shape,
        compiler_params=_cparams(("parallel", "arbitrary")),
        name="down_norm" if with_norm else "down_last",
    )(*args)
    return out if with_norm else (out[0], None, None)


def kernel(x_prompt, x_sample, cache_win_k, cache_win_v, state_gla, state_conv, rel_bias,
           ln1, w_in, ln_q, ln_k, sinks, w_gk2, b_gk, ln_o, w_oa, w_ob, w_out,
           ln2, w_up, w_conv, b_conv, w_down):
    bp, seq, _ = x_prompt.shape
    bs, t_new, _ = x_sample.shape
    tm_p = TM_PROMPT
    tm_s = bs * t_new
    tiles_per_seq = seq // tm_p

    ln1_3 = ln1.reshape(DEPTH, 1, D_MODEL)
    ln2_3 = ln2.reshape(DEPTH, 1, D_MODEL)
    w_in_t = jnp.swapaxes(w_in, 1, 2)
    w_oa_b, w_ob_b, w_out_b, w_down_b = (w.astype(BF16) for w in (w_oa, w_ob, w_out, w_down))
    w2_b = w_gk2.astype(BF16)
    b_conv3 = b_conv.reshape(DEPTH, 1, 2 * D_FF)
    lnq2 = jnp.tile(ln_q, (1, 2)).reshape(DEPTH, 1, LANES)
    lnk2 = jnp.tile(ln_k, (1, 2)).reshape(DEPTH, 1, LANES)

    bias_p = jnp.stack([_bias_table(rel_bias, jnp.asarray(_dmap_prompt(first))) for first in (True, False)])
    bias_s = _bias_table(rel_bias, jnp.asarray(_dmap_sample(t_new, SAMPLE_GROUP_A)))

    xp = x_prompt.reshape(bp * seq, D_MODEL)
    xs = x_sample.reshape(bs * t_new, D_MODEL)
    win_k = jnp.transpose(cache_win_k, (0, 1, 3, 4, 2))
    win_v = jnp.transpose(cache_win_v, (0, 1, 3, 4, 2))
    conv_state = state_conv.reshape(DEPTH, bs * (CONV_W - 1), 2 * D_FF)

    hp, gkp = _norm(xp, ln1_3, w_in_t, 0, 512)
    hs, gks = _norm(xs, ln1_3, w_in_t, 0, tm_s)

    kp, vp, gp, cp = [], [], [], []
    ksm, vsm, gsm, csm = [], [], [], []
    for l in range(DEPTH):
        lq, lk, sk = lnq2[l], lnk2[l], sinks[l]
        w2_l, bgk_l, lno_l = w2_b[l], b_gk[l].reshape(1, WIDTH_BK), ln_o[l].reshape(1, DV_B)
        more = l + 1 < DEPTH

        proj = _proj_main(hp, w_in_t, l, tm_p)
        gates, gates_s, oa, kn = _gates_attn(hp, hs, w_in_t, proj, sk, lq, lk, bias_p, l, tm_p, seq)
        ob, s_new = _gla_prompt(proj, gkp, w2_l, bgk_l, lno_l, bp, seq)
        kp.append(kn.reshape(bp, seq, N_KV_A, HEAD_DIM)[:, -WINDOW:])
        vp.append(proj[:, OFF_VA:OFF_VA + KV_WIDTH_A].reshape(bp, seq, N_KV_A, HEAD_DIM)[:, -WINDOW:])
        gp.append(s_new)
        prompt = (oa, ob, gates, xp)

        proj = _proj_main(hs, w_in_t, l, tm_s)
        oa, kn = _attn_sample(proj, win_k, win_v, sk, lq, lk, bias_s, l, bs, t_new)
        ob, s_new = _gla_sample(proj, gks, w2_l, bgk_l, lno_l, state_gla, l, bs, t_new)

        xp, h2, xs, h2_s = _merge(prompt, (oa, ob, gates_s, xs), w_oa_b, w_ob_b, w_out_b, ln2_3, l, TM_MERGE)
        act, tail_v, tail_g, act_s, u_v, u_g = _ffn_up(h2, h2_s, w_up, w_conv, b_conv3, conv_state, l, TM_FFN_UP,
                                                       seq // TM_FFN_UP, t_new)
        xp, hp, gkp = _down(act, w_down_b, xp, ln1_3, w_in_t, l, tm_p, more)
        xs, hs, gks = _down(act_s, w_down_b, xs, ln1_3, w_in_t, l, tm_s, more)
        tails = jnp.concatenate([tail_v, tail_g], axis=-1).reshape(bp, seq // TM_FFN_UP, 8, 2 * D_FF)
        cp.append(tails[:, -1, 8 - (CONV_W - 1):])
        k_new = kn.reshape(bs, t_new, N_KV_A, HEAD_DIM)
        v_new = proj[:, OFF_VA:OFF_VA + KV_WIDTH_A].reshape(bs, t_new, N_KV_A, HEAD_DIM)
        ksm.append(jnp.concatenate([cache_win_k[l][:, t_new:], k_new], axis=1))
        vsm.append(jnp.concatenate([cache_win_v[l][:, t_new:], v_new], axis=1))
        gsm.append(s_new)
        u = jnp.concatenate([u_v, u_g], axis=-1).reshape(bs, t_new, 2 * D_FF)
        csm.append(u[:, t_new - (CONV_W - 1):])

    return (xp.reshape(bp, seq, D_MODEL), xs.reshape(bs, t_new, D_MODEL),
            jnp.stack(kp), jnp.stack(vp), jnp.stack(gp), jnp.stack(cp),
            jnp.stack(ksm), jnp.stack(vsm), jnp.stack(gsm), jnp.stack(csm))
```

```python
import functools
import math

import numpy as np
import jax
import jax.numpy as jnp
from jax import lax
from jax.experimental import pallas as pl
from jax.experimental.pallas import tpu as pltpu

F32 = jnp.float32
BF16 = jnp.bfloat16

D_MODEL = 2048
DEPTH = 4
HEAD_DIM = 64
N_HEADS_A = 16
N_KV_A = 4
WIDTH_A = N_HEADS_A * HEAD_DIM
KV_WIDTH_A = N_KV_A * HEAD_DIM
WINDOW = 128
N_BUCKETS = 32
MAX_DISTANCE = 128
N_HEADS_B = 4
DK_B = 128
DV_B = 256
WIDTH_BK = N_HEADS_B * DK_B
WIDTH_BV = N_HEADS_B * DV_B
GATE_RANK = 16
GATE_TAU = 16.0
D_FF = 5632
CONV_W = 3
EPS = 1e-6

OFF_QA = 0
OFF_KA = OFF_QA + WIDTH_A
OFF_VA = OFF_KA + KV_WIDTH_A
OFF_QB = OFF_VA + KV_WIDTH_A
OFF_KB = OFF_QB + WIDTH_BK
OFF_VB = OFF_KB + WIDTH_BK
OFF_RB = OFF_VB + WIDTH_BV
OFF_GK = OFF_RB + WIDTH_BV
OFF_GA = OFF_GK + GATE_RANK
MAIN_COLS = OFF_GK
GATE_COLS = 2 * D_MODEL

LANES = 128
LOG2E = math.log2(math.e)
NEG_BIG = -1e30
VMEM_LIMIT = 56 * 1024 * 1024

TM_PROMPT = 1024
TM_FFN_UP = 1024
TM_MERGE = 256
TN_MAIN = 1536
TN_GATES = 1024
ATTN_BLOCKS_PER_STEP = 2
TF_UP = 512
GLA_CHUNK_P = 128
GLA_TBLOCK = 512
SAMPLE_GROUP_A = 8
SAMPLE_GROUP_B = 4


def _cparams(sem):
    return pltpu.CompilerParams(dimension_semantics=sem, vmem_limit_bytes=VMEM_LIMIT)


def _dot(a, b):
    return jnp.dot(a, b, preferred_element_type=F32)


def _dot_nt(a, b):
    return lax.dot_general(a, b, (((1,), (1,)), ((), ())), preferred_element_type=F32)


def _dot_tn(a, b):
    return lax.dot_general(a, b, (((0,), (0,)), ((), ())), preferred_element_type=F32)


def _split3(x):
    hi = x.astype(BF16)
    r = x - hi.astype(F32)
    mid = r.astype(BF16)
    lo = (r - mid.astype(F32)).astype(BF16)
    return hi, mid, lo


def _dot_exact_lhs01(m01, x):
    hi, mid, lo = _split3(x)
    return _dot(m01, hi) + _dot(m01, mid) + _dot(m01, lo)


def _rms(x, g):
    ms = jnp.mean(x * x, axis=-1, keepdims=True)
    return x * lax.rsqrt(ms + EPS) * g


def _norm_kernel(x_ref, ln_ref, wgk_ref, h_ref, gk_ref):
    h = _rms(x_ref[...], ln_ref[...]).astype(BF16)
    h_ref[...] = h
    gk_ref[...] = _dot_nt(h, wgk_ref[...].astype(BF16))


def _ln_spec(l, nargs):
    if nargs == 1:
        return pl.BlockSpec((None, 1, D_MODEL), lambda i: (l, 0, 0))
    return pl.BlockSpec((None, 1, D_MODEL), lambda i, j: (l, 0, 0))


def _wgk_spec(l, nargs):
    blk = OFF_GK // GATE_RANK
    if nargs == 1:
        return pl.BlockSpec((None, GATE_RANK, D_MODEL), lambda i: (l, blk, 0))
    return pl.BlockSpec((None, GATE_RANK, D_MODEL), lambda i, j: (l, blk, 0))


def _norm(x, ln1, w_in_t, l, tm):
    r = x.shape[0]
    return pl.pallas_call(
        _norm_kernel,
        grid=(r // tm,),
        in_specs=[pl.BlockSpec((tm, D_MODEL), lambda i: (i, 0)), _ln_spec(l, 1), _wgk_spec(l, 1)],
        out_specs=[pl.BlockSpec((tm, D_MODEL), lambda i: (i, 0)), pl.BlockSpec((tm, GATE_RANK), lambda i: (i, 0))],
        out_shape=[jax.ShapeDtypeStruct((r, D_MODEL), BF16), jax.ShapeDtypeStruct((r, GATE_RANK), F32)],
        compiler_params=_cparams(("parallel",)),
        name="norm",
    )(x, ln1, w_in_t)


def _proj_kernel(h_ref, w_ref, o_ref, wb_scr):
    @pl.when(pl.program_id(1) == 0)
    def _():
        wb_scr[...] = w_ref[...].astype(BF16)

    o_ref[...] = _dot_nt(h_ref[...], wb_scr[...])


def _proj_main(h, w_in_t, l, tm):
    r = h.shape[0]
    tn = TN_MAIN
    return pl.pallas_call(
        _proj_kernel,
        grid=(MAIN_COLS // tn, r // tm),
        in_specs=[
            pl.BlockSpec((tm, D_MODEL), lambda j, i: (i, 0)),
            pl.BlockSpec((None, tn, D_MODEL), lambda j, i: (l, j, 0)),
        ],
        out_specs=pl.BlockSpec((tm, tn), lambda j, i: (i, j)),
        out_shape=jax.ShapeDtypeStruct((r, MAIN_COLS), F32),
        scratch_shapes=[pltpu.VMEM((tn, D_MODEL), BF16)],
        compiler_params=_cparams(("arbitrary", "arbitrary")),
        name="proj_main",
    )(h, w_in_t)


def _bucket_ranges():
    d = np.arange(WINDOW)
    max_exact = N_BUCKETS // 2
    df = np.maximum(d, 1).astype(np.float32)
    large = max_exact + (np.log(df / np.float32(max_exact)) / np.float32(math.log(MAX_DISTANCE / max_exact))
                         * np.float32(N_BUCKETS - max_exact)).astype(np.int32)
    large = np.minimum(large, N_BUCKETS - 1)
    bucket = np.where(d < max_exact, d, large)
    ranges = []
    for b in range(N_BUCKETS):
        idx = np.nonzero(bucket == b)[0]
        if idx.size:
            assert idx[-1] - idx[0] + 1 == idx.size
            ranges.append((b, int(idx[0]), int(idx[-1])))
    return ranges


def _bias_kernel(rel_ref, d_ref, o_ref):
    h = pl.program_id(0)
    d = d_ref[...]
    val = jnp.full(d.shape, NEG_BIG, F32)
    for b, lo, hi in _bucket_ranges():
        val = jnp.where((d >= lo) & (d <= hi), rel_ref[b, h] * LOG2E, val)
    o_ref[...] = val


def _bias_table(rel_bias, dmap):
    mq, nk = dmap.shape
    return pl.pallas_call(
        _bias_kernel,
        grid=(N_HEADS_A,),
        in_specs=[
            pl.BlockSpec(memory_space=pltpu.SMEM),
            pl.BlockSpec((mq, nk), lambda h: (0, 0)),
        ],
        out_specs=pl.BlockSpec((None, mq, nk), lambda h: (h, 0, 0)),
        out_shape=jax.ShapeDtypeStruct((N_HEADS_A, mq, nk), F32),
        compiler_params=_cparams(("parallel",)),
        name="bias_table",
    )(rel_bias, dmap)


def _dmap_prompt(first_block):
    i = np.arange(WINDOW)[:, None]
    j = np.arange(2 * WINDOW)[None, :]
    d = i + WINDOW - j
    ok = (d >= 0) & (d < WINDOW)
    if first_block:
        ok = ok & (j >= WINDOW)
    return np.where(ok, d, -1).astype(np.int32)


def _dmap_sample(t_new, group):
    nk = group * WINDOW + LANES
    rows = np.arange(group * t_new)
    rb, rt = rows // t_new, rows % t_new
    d = np.full((group * t_new, nk), -1, np.int64)
    cols = np.arange(group * WINDOW)
    cb, cs = cols // WINDOW, cols % WINDOW
    dw = rt[:, None] + WINDOW - cs[None, :]
    ok = (rb[:, None] == cb[None, :]) & (dw >= 0) & (dw < WINDOW)
    d[:, :group * WINDOW] = np.where(ok, dw, -1)
    ncols = np.arange(group * t_new)
    nb, nu = ncols // t_new, ncols % t_new
    dn = rt[:, None] - nu[None, :]
    okn = (rb[:, None] == nb[None, :]) & (dn >= 0)
    d[:, group * WINDOW:group * WINDOW + group * t_new] = np.where(okn, dn, -1)
    return d.astype(np.int32)


def _group_ones():
    r = lax.broadcasted_iota(jnp.int32, (LANES, LANES), 0)
    c = lax.broadcasted_iota(jnp.int32, (LANES, LANES), 1)
    low_c = jnp.where(c < HEAD_DIM, 1.0, 0.0)
    return jnp.where(r < HEAD_DIM, low_c, 1.0 - low_c).astype(BF16)


def _head_norm(x, gmat, ln2):
    sq = x * x
    hi = sq.astype(BF16)
    lo = (sq - hi.astype(F32)).astype(BF16)
    ms = (_dot(hi, gmat) + _dot(lo, gmat)) * (1.0 / HEAD_DIM)
    return x * lax.rsqrt(ms + EPS) * ln2


def _split_heads(x):
    lane = lax.broadcasted_iota(jnp.int32, (x.shape[0], LANES), 1)
    low = lane < HEAD_DIM
    lo_parts, hi_parts = [], []
    for c in range(KV_WIDTH_A // LANES):
        xc = x[:, c * LANES:(c + 1) * LANES]
        xr = pltpu.roll(xc, HEAD_DIM, 1)
        lo_parts += [jnp.where(low, xc, 0.0).astype(BF16), jnp.where(low, xr, 0.0).astype(BF16)]
        hi_parts += [jnp.where(low, 0.0, xr).astype(BF16), jnp.where(low, 0.0, xc).astype(BF16)]
    return lo_parts, hi_parts


def _attend(q_cols, k_parts, v_parts, bias_ref, sink_ref):
    mq = q_cols[0].shape[0]
    pairs_per_kv = N_HEADS_A // N_KV_A // 2
    assert pairs_per_kv == 2
    first_rows = lax.broadcasted_iota(jnp.int32, (pairs_per_kv * mq, 1), 0) < mq
    outs = [None] * (N_HEADS_A // 2)
    for kh in range(N_KV_A):
        pcs = [kh * pairs_per_kv + p for p in range(pairs_per_kv)]
        q_stack = jnp.concatenate([q_cols[pc] for pc in pcs], axis=0)
        for half in range(2):
            heads = [2 * pc + half for pc in pcs]
            bias = jnp.concatenate([bias_ref[h] for h in heads], axis=0)
            sink = jnp.where(first_rows, sink_ref[heads[0]], sink_ref[heads[1]]) * LOG2E
            s = _dot_nt(q_stack, k_parts[half][kh]) + bias
            m = jnp.maximum(jnp.max(s, axis=-1, keepdims=True), sink)
            e = jnp.exp2(s - m)
            den = jnp.sum(e, axis=-1, keepdims=True) + jnp.exp2(sink - m)
            o = _dot(e.astype(BF16), v_parts[half][kh]) * (1.0 / den)
            for p, pc in enumerate(pcs):
                o_p = o[p * mq:(p + 1) * mq, :]
                outs[pc] = o_p if outs[pc] is None else outs[pc] + o_p
    return outs


def _gates_attn_kernel(blocks_per_seq, sink_ref, h_ref, hs_ref, w_ref, q_ref, k_ref, v_ref, lnq_ref, lnk_ref,
                       bias_ref, g_ref, gs_ref, o_ref, kn_ref, wb_scr, kprev_scr, vprev_scr, o_scr, kn_scr):
    j = pl.program_id(0)
    i = pl.program_id(1)
    step = j * pl.num_programs(1) + i

    @pl.when(i == 0)
    def _():
        wb_scr[...] = w_ref[...].astype(BF16)

    @pl.when(i == pl.num_programs(1) - 1)
    def _():
        gs_ref[...] = jax.nn.sigmoid(_dot_nt(hs_ref[...], wb_scr[...])).astype(gs_ref.dtype)

    @pl.when(step == 0)
    def _():
        kprev_scr[...] = jnp.zeros_like(kprev_scr)
        vprev_scr[...] = jnp.zeros_like(vprev_scr)

    gmat = _group_ones()
    lnq = lnq_ref[...]
    lnk = lnk_ref[...]
    scale = HEAD_DIM ** -0.5 * LOG2E
    for t in range(ATTN_BLOCKS_PER_STEP):
        rows = slice(t * WINDOW, (t + 1) * WINDOW)
        block = step * ATTN_BLOCKS_PER_STEP + t
        table = bias_ref.at[jnp.where(block % blocks_per_seq == 0, 0, 1)]
        kn_own = jnp.concatenate(
            [_head_norm(k_ref[rows, c * LANES:(c + 1) * LANES], gmat, lnk) for c in range(KV_WIDTH_A // LANES)],
            axis=1)
        kn_scr[rows, :] = kn_own
        k_own = _split_heads(kn_own)
        v_own = _split_heads(v_ref[rows, :])
        k_parts = [[jnp.concatenate([kprev_scr[half, kh], k_own[half][kh]], axis=0) for kh in range(N_KV_A)]
                   for half in range(2)]
        v_parts = [[jnp.concatenate([vprev_scr[half, kh], v_own[half][kh]], axis=0) for kh in range(N_KV_A)]
                   for half in range(2)]
        q_cols = [(_head_norm(q_ref[rows, c * LANES:(c + 1) * LANES], gmat, lnq) * scale).astype(BF16)
                  for c in range(WIDTH_A // LANES)]
        outs = _attend(q_cols, k_parts, v_parts, table, sink_ref)
        for c, o in enumerate(outs):
            o_scr[rows, c * LANES:(c + 1) * LANES] = o.astype(o_scr.dtype)
        for half in range(2):
            for kh in range(N_KV_A):
                kprev_scr[half, kh] = k_own[half][kh]
                vprev_scr[half, kh] = v_own[half][kh]

    g_ref[...] = jax.nn.sigmoid(_dot_nt(h_ref[...], wb_scr[...])).astype(g_ref.dtype)
    o_ref[...] = o_scr[...]
    kn_ref[...] = kn_scr[...]


def _gates_attn(h, hs, w_in_t, proj, sinks_l, lnq2, lnk2, bias_p, l, tm, seq):
    r = h.shape[0]
    rs = hs.shape[0]
    tn = TN_GATES
    nj, nt = GATE_COLS // tn, r // tm
    rows = ATTN_BLOCKS_PER_STEP * WINDOW
    assert nj * nt * rows == r
    kcol = OFF_KA // KV_WIDTH_A
    vcol = OFF_VA // KV_WIDTH_A
    step = lambda j, i: j * nt + i
    return pl.pallas_call(
        functools.partial(_gates_attn_kernel, seq // WINDOW),
        grid=(nj, nt),
        in_specs=[
            pl.BlockSpec(memory_space=pltpu.SMEM),
            pl.BlockSpec((tm, D_MODEL), lambda j, i: (i, 0)),
            pl.BlockSpec((rs, D_MODEL), lambda j, i: (0, 0)),
            pl.BlockSpec((pl.Squeezed(), pl.Element(tn), pl.Element(D_MODEL)),
                         lambda j, i: (l, pl.multiple_of(OFF_GA + j * tn, GATE_RANK), 0)),
            pl.BlockSpec((rows, WIDTH_A), lambda j, i: (step(j, i), 0)),
            pl.BlockSpec((rows, KV_WIDTH_A), lambda j, i: (step(j, i), kcol)),
            pl.BlockSpec((rows, KV_WIDTH_A), lambda j, i: (step(j, i), vcol)),
            pl.BlockSpec((1, LANES), lambda j, i: (0, 0)),
            pl.BlockSpec((1, LANES), lambda j, i: (0, 0)),
            pl.BlockSpec((2, N_HEADS_A, WINDOW, 2 * WINDOW), lambda j, i: (0, 0, 0, 0)),
        ],
        out_specs=[
            pl.BlockSpec((tm, tn), lambda j, i: (i, j)),
            pl.BlockSpec((rs, tn), lambda j, i: (0, j)),
            pl.BlockSpec((rows, WIDTH_A), lambda j, i: (step(j, i), 0)),
            pl.BlockSpec((rows, KV_WIDTH_A), lambda j, i: (step(j, i), 0)),
        ],
        out_shape=[
            jax.ShapeDtypeStruct((r, GATE_COLS), BF16),
            jax.ShapeDtypeStruct((rs, GATE_COLS), BF16),
            jax.ShapeDtypeStruct((r, WIDTH_A), BF16),
            jax.ShapeDtypeStruct((r, KV_WIDTH_A), F32),
        ],
        scratch_shapes=[
            pltpu.VMEM((tn, D_MODEL), BF16),
            pltpu.VMEM((2, N_KV_A, WINDOW, LANES), BF16),
            pltpu.VMEM((2, N_KV_A, WINDOW, LANES), BF16),
            pltpu.VMEM((rows, WIDTH_A), BF16),
            pltpu.VMEM((rows, KV_WIDTH_A), F32),
        ],
        compiler_params=_cparams(("arbitrary", "arbitrary")),
        name="gates_attn",
    )(sinks_l, h, hs, w_in_t, proj, proj, proj, lnq2, lnk2, bias_p)


def _attn_sample_kernel(sink_ref, q_ref, kn_in_ref, vn_ref, wk_ref, wv_ref, lnq_ref, lnk_ref, bias_ref,
                        o_ref, kn_ref):
    gmat = _group_ones()
    lnq = lnq_ref[...]
    lnk = lnk_ref[...]
    scale = HEAD_DIM ** -0.5 * LOG2E
    rows = q_ref.shape[0]
    g = wk_ref.shape[0]
    kn_new = jnp.concatenate(
        [_head_norm(kn_in_ref[:, c * LANES:(c + 1) * LANES], gmat, lnk) for c in range(KV_WIDTH_A // LANES)], axis=1)
    kn_ref[...] = kn_new
    pad = jnp.zeros((LANES - rows, KV_WIDTH_A), F32)

    def window_rows(ref):
        heads_per_col = LANES // HEAD_DIM
        cols = []
        for c in range(KV_WIDTH_A // LANES):
            blocks = [jnp.concatenate([ref[b, c * heads_per_col + j] for j in range(heads_per_col)], axis=0).T
                      for b in range(g)]
            cols.append(jnp.concatenate(blocks, axis=0))
        return jnp.concatenate(cols, axis=1)

    k = jnp.concatenate([window_rows(wk_ref), kn_new, pad], axis=0)
    v = jnp.concatenate([window_rows(wv_ref), vn_ref[...], pad], axis=0)
    q_cols = [(_head_norm(q_ref[:, c * LANES:(c + 1) * LANES], gmat, lnq) * scale).astype(BF16)
              for c in range(WIDTH_A // LANES)]
    outs = _attend(q_cols, _split_heads(k), _split_heads(v), bias_ref, sink_ref)
    for c, o in enumerate(outs):
        o_ref[:, c * LANES:(c + 1) * LANES] = o.astype(o_ref.dtype)


def _attn_sample(proj, win_k, win_v, sinks_l, lnq2, lnk2, bias_s, l, batch, t_new):
    g = SAMPLE_GROUP_A
    rows = g * t_new
    nk = g * WINDOW + LANES
    kcol = OFF_KA // KV_WIDTH_A
    vcol = OFF_VA // KV_WIDTH_A
    return pl.pallas_call(
        _attn_sample_kernel,
        grid=(batch // g,),
        in_specs=[
            pl.BlockSpec(memory_space=pltpu.SMEM),
            pl.BlockSpec((rows, WIDTH_A), lambda i: (i, 0)),
            pl.BlockSpec((rows, KV_WIDTH_A), lambda i: (i, kcol)),
            pl.BlockSpec((rows, KV_WIDTH_A), lambda i: (i, vcol)),
            pl.BlockSpec((None, g, N_KV_A, HEAD_DIM, WINDOW), lambda i: (l, i, 0, 0, 0)),
            pl.BlockSpec((None, g, N_KV_A, HEAD_DIM, WINDOW), lambda i: (l, i, 0, 0, 0)),
            pl.BlockSpec((1, LANES), lambda i: (0, 0)),
            pl.BlockSpec((1, LANES), lambda i: (0, 0)),
            pl.BlockSpec((N_HEADS_A, rows, nk), lambda i: (0, 0, 0)),
        ],
        out_specs=[
            pl.BlockSpec((rows, WIDTH_A), lambda i: (i, 0)),
            pl.BlockSpec((rows, KV_WIDTH_A), lambda i: (i, 0)),
        ],
        out_shape=[
            jax.ShapeDtypeStruct((batch * t_new, WIDTH_A), BF16),
            jax.ShapeDtypeStruct((batch * t_new, KV_WIDTH_A), F32),
        ],
        compiler_params=_cparams(("parallel",)),
        name="attn_sample",
    )(sinks_l, proj, proj, proj, win_k, win_v, lnq2, lnk2, bias_s)


def _log_decay(gk, w2_ref, bgk_ref):
    x = _dot(gk.astype(BF16), w2_ref[...]) + bgk_ref[...]
    log_sig = jnp.minimum(x, 0.0) - jnp.log(1.0 + jnp.exp(-jnp.abs(x)))
    return log_sig * (1.0 / GATE_TAU)


def _column(row):
    r = lax.broadcasted_iota(jnp.int32, (LANES, LANES), 0)
    c = lax.broadcasted_iota(jnp.int32, (LANES, LANES), 1)
    return jnp.sum(jnp.where(r == c, row, 0.0), axis=1, keepdims=True)


def _gla_out(o, ln_o, rgate):
    return (_rms(o, ln_o) * (rgate * jax.nn.sigmoid(rgate))).astype(BF16)


def _gla_prompt_kernel(q_ref, k_ref, va_ref, vb_ref, ra_ref, rb_ref, gk_ref, w2_ref, bgk_ref, lno_ref,
                       o_ref, s_out_ref, s_scr, b_scr):
    t = pl.program_id(1)
    c_len = GLA_CHUNK_P
    tb = q_ref.shape[0]
    n_chunks = tb // c_len
    half = c_len // 2
    scale = DK_B ** -0.5

    @pl.when(t == 0)
    def _():
        s_scr[...] = jnp.zeros_like(s_scr)

    shift = c_len.bit_length() - 1
    rb = lax.broadcasted_iota(jnp.int32, (tb, tb), 0)
    cb = lax.broadcasted_iota(jnp.int32, (tb, tb), 1)
    same_chunk = lax.shift_right_logical(rb, shift) == lax.shift_right_logical(cb, shift)
    tri_blocks = jnp.where(same_chunk & (rb >= cb), 1.0, 0.0).astype(BF16)
    b_scr[...] = _dot_exact_lhs01(tri_blocks, _log_decay(gk_ref[...], w2_ref, bgk_ref))

    ri = lax.broadcasted_iota(jnp.int32, (c_len, c_len), 0)
    ci = lax.broadcasted_iota(jnp.int32, (c_len, c_len), 1)
    quarter = half // 2
    same_half = jnp.where(ci < half, jnp.where(ri < half, 1, 0), jnp.where(ri < half, 0, 1))
    own_causal = (ri >= ci) & (same_half == 1)
    upper_row = lax.broadcasted_iota(jnp.int32, (c_len, 1), 0) >= half
    ln_o = lno_ref[...]
    v_refs = (va_ref, vb_ref)
    r_refs = (ra_ref, rb_ref)

    def chunk(c, carry):
        rows = pl.ds(pl.multiple_of(c * c_len, c_len), c_len)
        for h in range(N_HEADS_B):
            ks = slice(h * DK_B, (h + 1) * DK_B)
            vs = slice((h % 2) * DV_B, (h % 2 + 1) * DV_B)
            bh = b_scr[rows, ks]
            b_last = bh[c_len - 1:c_len, :]
            b_half = bh[half - 1:half, :]
            q = q_ref[rows, ks] * scale
            k = k_ref[rows, ks]
            v = v_refs[h // 2][rows, vs].astype(BF16)
            q_inter = (q * jnp.exp(bh)).astype(BF16)
            r_own = jnp.where(upper_row, bh[half + quarter - 1:half + quarter, :], bh[quarter - 1:quarter, :])
            a_own = _dot_nt((q * jnp.exp(bh - r_own)).astype(BF16), (k * jnp.exp(r_own - bh)).astype(BF16))
            q_x = jnp.where(upper_row, q * jnp.exp(bh - b_half), 0.0).astype(BF16)
            k_x = jnp.where(upper_row, 0.0, k * jnp.exp(b_half - bh)).astype(BF16)
            a = (jnp.where(own_causal, a_own, 0.0) + _dot_nt(q_x, k_x)).astype(BF16)
            k_state = (k * jnp.exp(b_last - bh)).astype(BF16)
            s_old = s_scr[h]
            o = _dot(a, v) + _dot(q_inter, s_old.astype(BF16))
            s_scr[h] = _column(jnp.exp(b_last)) * s_old + _dot_tn(k_state, v)
            o_ref[rows, h * DV_B:(h + 1) * DV_B] = _gla_out(o, ln_o, r_refs[h // 2][rows, vs])
        return carry

    lax.fori_loop(0, n_chunks, chunk, 0, unroll=4)

    @pl.when(t == pl.num_programs(1) - 1)
    def _():
        s_out_ref[...] = s_scr[...]


def _gla_prompt(proj, gk, w2_l, bgk_l, lno_l, batch, seq):
    tb = GLA_TBLOCK
    nt = seq // tb
    w = 2 * DV_B
    row = lambda b, t: b * nt + t
    spec = lambda col: pl.BlockSpec((tb, w), lambda b, t: (row(b, t), col))
    return pl.pallas_call(
        _gla_prompt_kernel,
        grid=(batch, nt),
        in_specs=[
            spec(OFF_QB // w), spec(OFF_KB // w),
            spec(OFF_VB // w), spec(OFF_VB // w + 1),
            spec(OFF_RB // w), spec(OFF_RB // w + 1),
            pl.BlockSpec((tb, GATE_RANK), lambda b, t: (row(b, t), 0)),
            pl.BlockSpec((GATE_RANK, WIDTH_BK), lambda b, t: (0, 0)),
            pl.BlockSpec((1, WIDTH_BK), lambda b, t: (0, 0)),
            pl.BlockSpec((1, DV_B), lambda b, t: (0, 0)),
        ],
        out_specs=[
            pl.BlockSpec((tb, WIDTH_BV), lambda b, t: (row(b, t), 0)),
            pl.BlockSpec((None, N_HEADS_B, DK_B, DV_B), lambda b, t: (b, 0, 0, 0)),
        ],
        out_shape=[
            jax.ShapeDtypeStruct((batch * seq, WIDTH_BV), BF16),
            jax.ShapeDtypeStruct((batch, N_HEADS_B, DK_B, DV_B), F32),
        ],
        scratch_shapes=[pltpu.VMEM((N_HEADS_B, DK_B, DV_B), F32), pltpu.VMEM((tb, WIDTH_BK), F32)],
        compiler_params=_cparams(("parallel", "arbitrary")),
        name="gla_prompt",
    )(proj, proj, proj, proj, proj, proj, gk, w2_l, bgk_l, lno_l)


def _gla_sample_kernel(t_new, q_ref, k_ref, va_ref, vb_ref, ra_ref, rb_ref, gk_ref, w2_ref, bgk_ref, lno_ref,
                       s_ref, o_ref, s_out_ref):
    rows = q_ref.shape[0]
    n_seq = rows // t_new
    scale = DK_B ** -0.5
    ri = lax.broadcasted_iota(jnp.int32, (rows, rows), 0)
    ci = lax.broadcasted_iota(jnp.int32, (rows, rows), 1)
    same = None
    for sq in range(n_seq):
        lo, hi = sq * t_new, (sq + 1) * t_new
        blk = (ri >= lo) & (ri < hi) & (ci >= lo) & (ci < hi)
        same = blk if same is None else (same | blk)
    causal = same & (ri >= ci)
    tri = jnp.where(causal, 1.0, 0.0).astype(BF16)
    rcol = lax.broadcasted_iota(jnp.int32, (rows, 1), 0)
    ln_o = lno_ref[...]
    v_refs = (va_ref, vb_ref)
    r_refs = (ra_ref, rb_ref)

    g = _log_decay(gk_ref[...], w2_ref, bgk_ref)
    b = _dot_exact_lhs01(tri, g)
    for h in range(N_HEADS_B):
        ks = slice(h * DK_B, (h + 1) * DK_B)
        vs = slice((h % 2) * DV_B, (h % 2 + 1) * DV_B)
        bh = b[:, ks]
        gh = g[:, ks]
        q = q_ref[:, ks] * scale
        k = k_ref[:, ks]
        v = v_refs[h // 2][:, vs].astype(BF16)
        q_inter = q * jnp.exp(bh)
        k_t = (k * jnp.exp(-bh)).astype(BF16)
        a = jnp.where(causal, _dot_nt(q_inter.astype(BF16), k_t), 0.0).astype(BF16)
        o = _dot(a, v)
        for sq in range(n_seq):
            mine = (rcol >= sq * t_new) & (rcol < (sq + 1) * t_new)
            b_last = jnp.sum(jnp.where(mine, gh, 0.0), axis=0, keepdims=True)
            k_state = jnp.where(mine, k * jnp.exp(b_last - bh), 0.0).astype(BF16)
            q_mine = jnp.where(mine, q_inter, 0.0).astype(BF16)
            s_old = s_ref[sq, h]
            o = o + _dot(q_mine, s_old.astype(BF16))
            s_out_ref[sq, h] = _column(jnp.exp(b_last)) * s_old + _dot_tn(k_state, v)
        o_ref[:, h * DV_B:(h + 1) * DV_B] = _gla_out(o, ln_o, r_refs[h // 2][:, vs])


def _gla_sample(proj, gk, w2_l, bgk_l, lno_l, state, l, batch, t_new):
    g = SAMPLE_GROUP_B
    rows = g * t_new
    w = 2 * DV_B
    spec = lambda col: pl.BlockSpec((rows, w), lambda i: (i, col))
    return pl.pallas_call(
        functools.partial(_gla_sample_kernel, t_new),
        grid=(batch // g,),
        in_specs=[
            spec(OFF_QB // w), spec(OFF_KB // w),
            spec(OFF_VB // w), spec(OFF_VB // w + 1),
            spec(OFF_RB // w), spec(OFF_RB // w + 1),
            pl.BlockSpec((rows, GATE_RANK), lambda i: (i, 0)),
            pl.BlockSpec((GATE_RANK, WIDTH_BK), lambda i: (0, 0)),
            pl.BlockSpec((1, WIDTH_BK), lambda i: (0, 0)),
            pl.BlockSpec((1, DV_B), lambda i: (0, 0)),
            pl.BlockSpec((None, g, N_HEADS_B, DK_B, DV_B), lambda i: (l, i, 0, 0, 0)),
        ],
        out_specs=[
            pl.BlockSpec((rows, WIDTH_BV), lambda i: (i, 0)),
            pl.BlockSpec((g, N_HEADS_B, DK_B, DV_B), lambda i: (i, 0, 0, 0)),
        ],
        out_shape=[
            jax.ShapeDtypeStruct((batch * t_new, WIDTH_BV), BF16),
            jax.ShapeDtypeStruct((batch, N_HEADS_B, DK_B, DV_B), F32),
        ],
        compiler_params=_cparams(("parallel",)),
        name="gla_sample",
    )(proj, proj, proj, proj, proj, proj, gk, w2_l, bgk_l, lno_l, state)


def _merge_rows(oa_ref, ob_ref, sa_ref, sb_ref, x_ref, woa_ref, wob_ref, wout_ref, ln_ref, x_out_ref, h_ref):
    a = _dot(oa_ref[...], woa_ref[...])
    b = _dot(ob_ref[...], wob_ref[...])
    mix = (sa_ref[...].astype(F32) * a + sb_ref[...].astype(F32) * b).astype(BF16)
    x_new = x_ref[...] + _dot(mix, wout_ref[...])
    x_out_ref[...] = x_new
    h_ref[...] = _rms(x_new, ln_ref[...]).astype(BF16)


def _merge_kernel(oa_ref, ob_ref, sa_ref, sb_ref, x_ref, oa_s_ref, ob_s_ref, sa_s_ref, sb_s_ref, x_s_ref,
                  woa_ref, wob_ref, wout_ref, ln_ref, x_out_ref, h_ref, x_s_out_ref, h_s_ref):
    _merge_rows(oa_ref, ob_ref, sa_ref, sb_ref, x_ref, woa_ref, wob_ref, wout_ref, ln_ref, x_out_ref, h_ref)

    @pl.when(pl.program_id(0) == pl.num_programs(0) - 1)
    def _():
        _merge_rows(oa_s_ref, ob_s_ref, sa_s_ref, sb_s_ref, x_s_ref, woa_ref, wob_ref, wout_ref, ln_ref,
                    x_s_out_ref, h_s_ref)


def _merge(prompt, sample, w_oa, w_ob, w_out, ln2, l, tm):
    r = prompt[3].shape[0]
    rs = sample[3].shape[0]
    resident = dict(pipeline_mode=pl.Buffered(1))
    row = lambda i: (i, 0)
    fixed = lambda i: (0, 0)
    group_specs = lambda rows, at, at1: [
        pl.BlockSpec((rows, WIDTH_A), at),
        pl.BlockSpec((rows, WIDTH_BV), at),
        pl.BlockSpec((rows, D_MODEL), at),
        pl.BlockSpec((rows, D_MODEL), at1),
        pl.BlockSpec((rows, D_MODEL), at),
    ]
    operands = lambda g: (g[0], g[1], g[2], g[2], g[3])
    return pl.pallas_call(
        _merge_kernel,
        grid=(r // tm,),
        in_specs=group_specs(tm, row, lambda i: (i, 1)) + group_specs(rs, fixed, lambda i: (0, 1)) + [
            pl.BlockSpec((None, WIDTH_A, D_MODEL), lambda i: (l, 0, 0), **resident),
            pl.BlockSpec((None, WIDTH_BV, D_MODEL), lambda i: (l, 0, 0), **resident),
            pl.BlockSpec((None, D_MODEL, D_MODEL), lambda i: (l, 0, 0), **resident),
            _ln_spec(l, 1),
        ],
        out_specs=[pl.BlockSpec((tm, D_MODEL), row), pl.BlockSpec((tm, D_MODEL), row),
                   pl.BlockSpec((rs, D_MODEL), fixed), pl.BlockSpec((rs, D_MODEL), fixed)],
        out_shape=[jax.ShapeDtypeStruct((r, D_MODEL), F32), jax.ShapeDtypeStruct((r, D_MODEL), BF16),
                   jax.ShapeDtypeStruct((rs, D_MODEL), F32), jax.ShapeDtypeStruct((rs, D_MODEL), BF16)],
        compiler_params=_cparams(("arbitrary",)),
        name="merge",
    )(*operands(prompt), *operands(sample), w_oa, w_ob, w_out, ln2)


def _gelu_tanh(x):
    return x * (0.5 * (1.0 + jnp.tanh(math.sqrt(2.0 / math.pi) * (x + 0.044715 * (x * x * x)))))


def _conv3(u, p1, p2, w_ref, b_ref):
    return w_ref[0:1, :] * p2 + w_ref[1:2, :] * p1 + w_ref[2:3, :] * u + b_ref[...]


def _history_selectors(n_seq, t_new):
    hist = CONV_W - 1
    e1 = np.zeros((n_seq * t_new, n_seq * hist), np.float32)
    e2 = np.zeros((n_seq * t_new, n_seq * hist), np.float32)
    for b in range(n_seq):
        for t in range(min(t_new, hist)):
            if t < 1:
                e1[b * t_new + t, b * hist + hist - 1 + t] = 1.0
            e2[b * t_new + t, b * hist + t] = 1.0
    return jnp.asarray(e1, BF16), jnp.asarray(e2, BF16)


def _ffn_up_prompt_kernel(tiles_per_seq, h_ref, wv_ref, wg_ref, cv_ref, cg_ref, bv_ref, bg_ref,
                          act_ref, tv_ref, tg_ref, wvb_scr, wgb_scr, carry_v, carry_g):
    i = pl.program_id(1)

    @pl.when(i == 0)
    def _():
        wvb_scr[...] = wv_ref[...].astype(BF16)
        wgb_scr[...] = wg_ref[...].astype(BF16)

    h = h_ref[...]
    tm = h.shape[0]
    first = (i % tiles_per_seq) == 0
    row8 = lax.broadcasted_iota(jnp.int32, (8, 1), 0)

    def branch(wb_scr, carry, c_ref, b_ref, tail_ref):
        u = _dot(h, wb_scr[...])
        prev = jnp.where(first, 0.0, carry[...])
        tail = u[tm - 8:tm, :]
        carry[...] = tail
        tail_ref[...] = tail
        p1 = pltpu.roll(u, 1, 0)
        p2 = pltpu.roll(u, 2, 0)
        body = _conv3(u, p1, p2, c_ref, b_ref)
        p1_top = jnp.where(row8 < 1, pltpu.roll(prev, 1, 0), p1[0:8, :])
        p2_top = jnp.where(row8 < 2, pltpu.roll(prev, 2, 0), p2[0:8, :])
        top = _conv3(u[0:8, :], p1_top, p2_top, c_ref, b_ref)
        return jnp.concatenate([top, body[8:, :]], axis=0)

    gate = _gelu_tanh(branch(wgb_scr, carry_g, cg_ref, bg_ref, tg_ref))
    val = branch(wvb_scr, carry_v, cv_ref, bv_ref, tv_ref)
    act_ref[...] = (gate * val).astype(act_ref.dtype)


def _ffn_up_sample_kernel(pos_ref, e1_ref, e2_ref, h_ref, wv_ref, wg_ref, cv_ref, cg_ref, bv_ref, bg_ref,
                          sv_ref, sg_ref, act_ref, uv_ref, ug_ref):
    h = h_ref[...]
    pos = pos_ref[...]
    e1 = e1_ref[...]
    e2 = e2_ref[...]

    def branch(w_ref, c_ref, b_ref, s_ref, u_ref):
        u = _dot(h, w_ref[...].astype(BF16))
        u_ref[...] = u
        cached = s_ref[...]
        p1 = jnp.where(pos >= 1, pltpu.roll(u, 1, 0), _dot_exact_lhs01(e1, cached))
        p2 = jnp.where(pos >= 2, pltpu.roll(u, 2, 0), _dot_exact_lhs01(e2, cached))
        return _conv3(u, p1, p2, c_ref, b_ref)

    val = branch(wv_ref, cv_ref, bv_ref, sv_ref, uv_ref)
    gate = branch(wg_ref, cg_ref, bg_ref, sg_ref, ug_ref)
    act_ref[...] = (_gelu_tanh(gate) * val).astype(act_ref.dtype)


def _ffn_up_kernel(tiles_per_seq, pos_ref, e1_ref, e2_ref, h_ref, hs_ref, wv_ref, wg_ref, cv_ref, cg_ref,
                   bv_ref, bg_ref, sv_ref, sg_ref,
                   act_ref, tv_ref, tg_ref, acts_ref, uv_ref, ug_ref, wvb_scr, wgb_scr, carry_v, carry_g):
    _ffn_up_prompt_kernel(tiles_per_seq, h_ref, wv_ref, wg_ref, cv_ref, cg_ref, bv_ref, bg_ref,
                          act_ref, tv_ref, tg_ref, wvb_scr, wgb_scr, carry_v, carry_g)

    @pl.when(pl.program_id(1) == pl.num_programs(1) - 1)
    def _():
        _ffn_up_sample_kernel(pos_ref, e1_ref, e2_ref, hs_ref, wvb_scr, wgb_scr, cv_ref, cg_ref, bv_ref, bg_ref,
                              sv_ref, sg_ref, acts_ref, uv_ref, ug_ref)


def _ffn_up(h, hs, w_up, w_conv, b_conv, conv_state, l, tm, tiles_per_seq, t_new):
    r = h.shape[0]
    rs = hs.shape[0]
    tf = TF_UP
    nt = r // tm
    nj = D_FF // tf
    ns = conv_state.shape[1]
    w_spec = lambda off: pl.BlockSpec((None, D_MODEL, tf), lambda j, i: (l, 0, j + off))
    c_spec = lambda off: pl.BlockSpec((None, CONV_W, tf), lambda j, i: (l, 0, j + off))
    b_spec = lambda off: pl.BlockSpec((None, 1, tf), lambda j, i: (l, 0, j + off))
    s_spec = lambda off: pl.BlockSpec((None, ns, tf), lambda j, i: (l, 0, j + off))
    tail_spec = pl.BlockSpec((None, 8, tf), lambda j, i: (i, 0, j))
    u_spec = pl.BlockSpec((rs, tf), lambda j, i: (0, j))
    fixed = lambda shape: pl.BlockSpec(shape, lambda j, i: (0, 0))
    pos = jnp.asarray((np.arange(rs) % t_new).astype(np.int32).reshape(rs, 1))
    e1, e2 = _history_selectors(rs // t_new, t_new)
    return pl.pallas_call(
        functools.partial(_ffn_up_kernel, tiles_per_seq),
        grid=(nj, nt),
        in_specs=[
            fixed((rs, 1)), fixed((rs, ns)), fixed((rs, ns)),
            pl.BlockSpec((tm, D_MODEL), lambda j, i: (i, 0)),
            fixed((rs, D_MODEL)),
            w_spec(0), w_spec(nj), c_spec(0), c_spec(nj), b_spec(0), b_spec(nj), s_spec(0), s_spec(nj),
        ],
        out_specs=[
            pl.BlockSpec((None, tm, tf), lambda j, i: (j, i, 0)), tail_spec, tail_spec,
            pl.BlockSpec((None, rs, tf), lambda j, i: (j, 0, 0)), u_spec, u_spec,
        ],
        out_shape=[
            jax.ShapeDtypeStruct((nj, r, tf), BF16),
            jax.ShapeDtypeStruct((nt, 8, D_FF), F32),
            jax.ShapeDtypeStruct((nt, 8, D_FF), F32),
            jax.ShapeDtypeStruct((nj, rs, tf), BF16),
            jax.ShapeDtypeStruct((rs, D_FF), F32),
            jax.ShapeDtypeStruct((rs, D_FF), F32),
        ],
        scratch_shapes=[
            pltpu.VMEM((D_MODEL, tf), BF16),
            pltpu.VMEM((D_MODEL, tf), BF16),
            pltpu.VMEM((8, tf), F32),
            pltpu.VMEM((8, tf), F32),
        ],
        compiler_params=_cparams(("arbitrary", "arbitrary")),
        name="ffn_up",
    )(pos, e1, e2, h, hs, w_up, w_up, w_conv, w_conv, b_conv, b_conv, conv_state, conv_state)


def _down_accumulate(act_ref, w_ref, x_ref, x_out_ref):
    @pl.when(pl.program_id(1) == 0)
    def _():
        x_out_ref[...] = x_ref[...]

    act = act_ref[...]
    step = 512
    for n in range(D_MODEL // step):
        cols = slice(n * step, (n + 1) * step)
        x_out_ref[:, cols] += _dot(act, w_ref[:, cols])


def _down_last_kernel(act_ref, w_ref, x_ref, x_out_ref):
    _down_accumulate(act_ref, w_ref, x_ref, x_out_ref)


def _down_norm_kernel(act_ref, w_ref, x_ref, ln_ref, wgk_ref, x_out_ref, h_ref, gk_ref):
    _down_accumulate(act_ref, w_ref, x_ref, x_out_ref)

    @pl.when(pl.program_id(1) == pl.num_programs(1) - 1)
    def _():
        h = _rms(x_out_ref[...], ln_ref[...]).astype(BF16)
        h_ref[...] = h
        gk_ref[...] = _dot_nt(h, wgk_ref[...].astype(BF16))


def _down(act, w_down, x, ln1, w_in_t, l, tm, with_norm):
    nk, r, tk = act.shape
    row = lambda i, k: (i, 0)
    in_specs = [
        pl.BlockSpec((None, tm, tk), lambda i, k: (k, i, 0)),
        pl.BlockSpec((None, tk, D_MODEL), lambda i, k: (l, k, 0)),
        pl.BlockSpec((tm, D_MODEL), row),
    ]
    out_specs = [pl.BlockSpec((tm, D_MODEL), row)]
    out_shape = [jax.ShapeDtypeStruct((r, D_MODEL), F32)]
    args = [act, w_down, x]
    if with_norm:
        in_specs += [_ln_spec(l + 1, 2), _wgk_spec(l + 1, 2)]
        out_specs += [pl.BlockSpec((tm, D_MODEL), row), pl.BlockSpec((tm, GATE_RANK), row)]
        out_shape += [jax.ShapeDtypeStruct((r, D_MODEL), BF16), jax.ShapeDtypeStruct((r, GATE_RANK), F32)]
        args += [ln1, w_in_t]
    out = pl.pallas_call(
        _down_norm_kernel if with_norm else _down_last_kernel,
        grid=(r // tm, nk),
        in_specs=in_specs,
        out_specs=out_specs,
        out_shape=out_shape,
        compiler_params=_cparams(("parallel", "arbitrary")),
        name="down_norm" if with_norm else "down_last",
    )(*args)
    return out if with_norm else (out[0], None, None)


def kernel(x_prompt, x_sample, cache_win_k, cache_win_v, state_gla, state_conv, rel_bias,
           ln1, w_in, ln_q, ln_k, sinks, w_gk2, b_gk, ln_o, w_oa, w_ob, w_out,
           ln2, w_up, w_conv, b_conv, w_down):
    bp, seq, _ = x_prompt.shape
    bs, t_new, _ = x_sample.shape
    tm_p = TM_PROMPT
    tm_s = bs * t_new
    tiles_per_seq = seq // tm_p

    ln1_3 = ln1.reshape(DEPTH, 1, D_MODEL)
    ln2_3 = ln2.reshape(DEPTH, 1, D_MODEL)
    w_in_t = jnp.swapaxes(w_in, 1, 2)
    w_oa_b, w_ob_b, w_out_b, w_down_b = (w.astype(BF16) for w in (w_oa, w_ob, w_out, w_down))
    w2_b = w_gk2.astype(BF16)
    b_conv3 = b_conv.reshape(DEPTH, 1, 2 * D_FF)
    lnq2 = jnp.tile(ln_q, (1, 2)).reshape(DEPTH, 1, LANES)
    lnk2 = jnp.tile(ln_k, (1, 2)).reshape(DEPTH, 1, LANES)

    bias_p = jnp.stack([_bias_table(rel_bias, jnp.asarray(_dmap_prompt(first))) for first in (True, False)])
    bias_s = _bias_table(rel_bias, jnp.asarray(_dmap_sample(t_new, SAMPLE_GROUP_A)))

    xp = x_prompt.reshape(bp * seq, D_MODEL)
    xs = x_sample.reshape(bs * t_new, D_MODEL)
    win_k = jnp.transpose(cache_win_k, (0, 1, 3, 4, 2))
    win_v = jnp.transpose(cache_win_v, (0, 1, 3, 4, 2))
    conv_state = state_conv.reshape(DEPTH, bs * (CONV_W - 1), 2 * D_FF)

    hp, gkp = _norm(xp, ln1_3, w_in_t, 0, 512)
    hs, gks = _norm(xs, ln1_3, w_in_t, 0, tm_s)

    kp, vp, gp, cp = [], [], [], []
    ksm, vsm, gsm, csm = [], [], [], []
    for l in range(DEPTH):
        lq, lk, sk = lnq2[l], lnk2[l], sinks[l]
        w2_l, bgk_l, lno_l = w2_b[l], b_gk[l].reshape(1, WIDTH_BK), ln_o[l].reshape(1, DV_B)
        more = l + 1 < DEPTH

        proj = _proj_main(hp, w_in_t, l, tm_p)
        gates, gates_s, oa, kn = _gates_attn(hp, hs, w_in_t, proj, sk, lq, lk, bias_p, l, tm_p, seq)
        ob, s_new = _gla_prompt(proj, gkp, w2_l, bgk_l, lno_l, bp, seq)
        kp.append(kn.reshape(bp, seq, KV_WIDTH_A)[:, -WINDOW:].reshape(bp, WINDOW, N_KV_A, HEAD_DIM))
        vp.append(proj.reshape(bp, seq, MAIN_COLS)[:, -WINDOW:, OFF_VA:OFF_VA + KV_WIDTH_A]
                  .reshape(bp, WINDOW, N_KV_A, HEAD_DIM))
        gp.append(s_new)
        prompt = (oa, ob, gates, xp)

        proj = _proj_main(hs, w_in_t, l, tm_s)
        oa, kn = _attn_sample(proj, win_k, win_v, sk, lq, lk, bias_s, l, bs, t_new)
        ob, s_new = _gla_sample(proj, gks, w2_l, bgk_l, lno_l, state_gla, l, bs, t_new)

        xp, h2, xs, h2_s = _merge(prompt, (oa, ob, gates_s, xs), w_oa_b, w_ob_b, w_out_b, ln2_3, l, TM_MERGE)
        act, tail_v, tail_g, act_s, u_v, u_g = _ffn_up(h2, h2_s, w_up, w_conv, b_conv3, conv_state, l, TM_FFN_UP,
                                                       seq // TM_FFN_UP, t_new)
        xp, hp, gkp = _down(act, w_down_b, xp, ln1_3, w_in_t, l, tm_p, more)
        xs, hs, gks = _down(act_s, w_down_b, xs, ln1_3, w_in_t, l, tm_s, more)
        tails = jnp.concatenate([tail_v, tail_g], axis=-1).reshape(bp, seq // TM_FFN_UP, 8, 2 * D_FF)
        cp.append(tails[:, -1, 8 - (CONV_W - 1):])
        k_new = kn.reshape(bs, t_new, N_KV_A, HEAD_DIM)
        v_new = proj[:, OFF_VA:OFF_VA + KV_WIDTH_A].reshape(bs, t_new, N_KV_A, HEAD_DIM)
        ksm.append(jnp.concatenate([cache_win_k[l][:, t_new:], k_new], axis=1))
        vsm.append(jnp.concatenate([cache_win_v[l][:, t_new:], v_new], axis=1))
        gsm.append(s_new)
        u = jnp.concatenate([u_v, u_g], axis=-1).reshape(bs, t_new, 2 * D_FF)
        csm.append(u[:, t_new - (CONV_W - 1):])

    return (xp.reshape(bp, seq, D_MODEL), xs.reshape(bs, t_new, D_MODEL),
            jnp.stack(kp), jnp.stack(vp), jnp.stack(gp), jnp.stack(cp),
            jnp.stack(ksm), jnp.stack(vsm), jnp.stack(gsm), jnp.stack(csm))
```

```python
import functools
import math

import numpy as np
import jax
import jax.numpy as jnp
from jax import lax
from jax.experimental import pallas as pl
from jax.experimental.pallas import tpu as pltpu

F32 = jnp.float32
BF16 = jnp.bfloat16

D_MODEL = 2048
DEPTH = 4
HEAD_DIM = 64
N_HEADS_A = 16
N_KV_A = 4
WIDTH_A = N_HEADS_A * HEAD_DIM
KV_WIDTH_A = N_KV_A * HEAD_DIM
WINDOW = 128
N_BUCKETS = 32
MAX_DISTANCE = 128
N_HEADS_B = 4
DK_B = 128
DV_B = 256
WIDTH_BK = N_HEADS_B * DK_B
WIDTH_BV = N_HEADS_B * DV_B
GATE_RANK = 16
GATE_TAU = 16.0
D_FF = 5632
CONV_W = 3
EPS = 1e-6

OFF_QA = 0
OFF_KA = OFF_QA + WIDTH_A
OFF_VA = OFF_KA + KV_WIDTH_A
OFF_QB = OFF_VA + KV_WIDTH_A
OFF_KB = OFF_QB + WIDTH_BK
OFF_VB = OFF_KB + WIDTH_BK
OFF_RB = OFF_VB + WIDTH_BV
OFF_GK = OFF_RB + WIDTH_BV
OFF_GA = OFF_GK + GATE_RANK
MAIN_COLS = OFF_GK
GATE_COLS = 2 * D_MODEL

LANES = 128
LOG2E = math.log2(math.e)
NEG_BIG = -1e30
VMEM_LIMIT = 56 * 1024 * 1024
VMEM_LIMIT_PROJ = 58 * 1024 * 1024

TM_PROMPT = 1024
TM_FFN_UP = 1024
TM_MERGE = 256
TN_MAIN = 1536
TN_GATES = 1024
ATTN_BLOCKS_PER_STEP = 2
TF_UP = 512
GLA_CHUNK_P = 128
GLA_TBLOCK = 512
SAMPLE_GROUP_A = 8
SAMPLE_GROUP_B = 4


def _cparams(sem, vmem_limit=VMEM_LIMIT):
    return pltpu.CompilerParams(dimension_semantics=sem, vmem_limit_bytes=vmem_limit)


def _dot(a, b):
    return jnp.dot(a, b, preferred_element_type=F32)


def _dot_nt(a, b):
    return lax.dot_general(a, b, (((1,), (1,)), ((), ())), preferred_element_type=F32)


def _dot_tn(a, b):
    return lax.dot_general(a, b, (((0,), (0,)), ((), ())), preferred_element_type=F32)


def _split3(x):
    hi = x.astype(BF16)
    r = x - hi.astype(F32)
    mid = r.astype(BF16)
    lo = (r - mid.astype(F32)).astype(BF16)
    return hi, mid, lo


def _dot_exact_lhs01(m01, x):
    hi, mid, lo = _split3(x)
    return _dot(m01, hi) + _dot(m01, mid) + _dot(m01, lo)


def _rms(x, g):
    ms = jnp.mean(x * x, axis=-1, keepdims=True)
    return x * lax.rsqrt(ms + EPS) * g


def _norm_kernel(x_ref, ln_ref, wgk_ref, h_ref, gk_ref):
    h = _rms(x_ref[...], ln_ref[...]).astype(BF16)
    h_ref[...] = h
    gk_ref[...] = _dot_nt(h, wgk_ref[...].astype(BF16))


def _ln_spec(l, nargs):
    if nargs == 1:
        return pl.BlockSpec((None, 1, D_MODEL), lambda i: (l, 0, 0))
    return pl.BlockSpec((None, 1, D_MODEL), lambda i, j: (l, 0, 0))


def _wgk_spec(l, nargs):
    blk = OFF_GK // GATE_RANK
    if nargs == 1:
        return pl.BlockSpec((None, GATE_RANK, D_MODEL), lambda i: (l, blk, 0))
    return pl.BlockSpec((None, GATE_RANK, D_MODEL), lambda i, j: (l, blk, 0))


def _norm(x, ln1, w_in_t, l, tm):
    r = x.shape[0]
    return pl.pallas_call(
        _norm_kernel,
        grid=(r // tm,),
        in_specs=[pl.BlockSpec((tm, D_MODEL), lambda i: (i, 0)), _ln_spec(l, 1), _wgk_spec(l, 1)],
        out_specs=[pl.BlockSpec((tm, D_MODEL), lambda i: (i, 0)), pl.BlockSpec((tm, GATE_RANK), lambda i: (i, 0))],
        out_shape=[jax.ShapeDtypeStruct((r, D_MODEL), BF16), jax.ShapeDtypeStruct((r, GATE_RANK), F32)],
        compiler_params=_cparams(("parallel",)),
        name="norm",
    )(x, ln1, w_in_t)


def _proj_kernel(h_ref, hs_ref, w_ref, o_ref, os_ref, wb_scr):
    i = pl.program_id(1)

    @pl.when(i == 0)
    def _():
        wb_scr[...] = w_ref[...].astype(BF16)

    o_ref[...] = _dot_nt(h_ref[...], wb_scr[...])

    @pl.when(i == pl.num_programs(1) - 1)
    def _():
        os_ref[...] = _dot_nt(hs_ref[...], wb_scr[...])


def _proj_main(h, hs, w_in_t, l, tm):
    r = h.shape[0]
    rs = hs.shape[0]
    tn = TN_MAIN
    return pl.pallas_call(
        _proj_kernel,
        grid=(MAIN_COLS // tn, r // tm),
        in_specs=[
            pl.BlockSpec((tm, D_MODEL), lambda j, i: (i, 0)),
            pl.BlockSpec((rs, D_MODEL), lambda j, i: (0, 0), pipeline_mode=pl.Buffered(1)),
            pl.BlockSpec((None, tn, D_MODEL), lambda j, i: (l, j, 0)),
        ],
        out_specs=[pl.BlockSpec((tm, tn), lambda j, i: (i, j)), pl.BlockSpec((rs, tn), lambda j, i: (0, j))],
        out_shape=[jax.ShapeDtypeStruct((r, MAIN_COLS), F32), jax.ShapeDtypeStruct((rs, MAIN_COLS), F32)],
        scratch_shapes=[pltpu.VMEM((tn, D_MODEL), BF16)],
        compiler_params=_cparams(("arbitrary", "arbitrary"), VMEM_LIMIT_PROJ),
        name="proj_main",
    )(h, hs, w_in_t)


def _bucket_ranges():
    d = np.arange(WINDOW)
    max_exact = N_BUCKETS // 2
    df = np.maximum(d, 1).astype(np.float32)
    large = max_exact + (np.log(df / np.float32(max_exact)) / np.float32(math.log(MAX_DISTANCE / max_exact))
                         * np.float32(N_BUCKETS - max_exact)).astype(np.int32)
    large = np.minimum(large, N_BUCKETS - 1)
    bucket = np.where(d < max_exact, d, large)
    ranges = []
    for b in range(N_BUCKETS):
        idx = np.nonzero(bucket == b)[0]
        if idx.size:
            assert idx[-1] - idx[0] + 1 == idx.size
            ranges.append((b, int(idx[0]), int(idx[-1])))
    return ranges


def _bias_kernel(rel_ref, d_ref, o_ref):
    h = pl.program_id(0)
    d = d_ref[...]
    val = jnp.full(d.shape, NEG_BIG, F32)
    for b, lo, hi in _bucket_ranges():
        val = jnp.where((d >= lo) & (d <= hi), rel_ref[b, h] * LOG2E, val)
    o_ref[...] = val


def _bias_table(rel_bias, dmap):
    mq, nk = dmap.shape
    return pl.pallas_call(
        _bias_kernel,
        grid=(N_HEADS_A,),
        in_specs=[
            pl.BlockSpec(memory_space=pltpu.SMEM),
            pl.BlockSpec((mq, nk), lambda h: (0, 0)),
        ],
        out_specs=pl.BlockSpec((None, mq, nk), lambda h: (h, 0, 0)),
        out_shape=jax.ShapeDtypeStruct((N_HEADS_A, mq, nk), F32),
        compiler_params=_cparams(("parallel",)),
        name="bias_table",
    )(rel_bias, dmap)


def _dmap_prompt(first_block):
    i = np.arange(WINDOW)[:, None]
    j = np.arange(2 * WINDOW)[None, :]
    d = i + WINDOW - j
    ok = (d >= 0) & (d < WINDOW)
    if first_block:
        ok = ok & (j >= WINDOW)
    return np.where(ok, d, -1).astype(np.int32)


def _dmap_sample(t_new, group):
    nk = group * WINDOW + LANES
    rows = np.arange(group * t_new)
    rb, rt = rows // t_new, rows % t_new
    d = np.full((group * t_new, nk), -1, np.int64)
    cols = np.arange(group * WINDOW)
    cb, cs = cols // WINDOW, cols % WINDOW
    dw = rt[:, None] + WINDOW - cs[None, :]
    ok = (rb[:, None] == cb[None, :]) & (dw >= 0) & (dw < WINDOW)
    d[:, :group * WINDOW] = np.where(ok, dw, -1)
    ncols = np.arange(group * t_new)
    nb, nu = ncols // t_new, ncols % t_new
    dn = rt[:, None] - nu[None, :]
    okn = (rb[:, None] == nb[None, :]) & (dn >= 0)
    d[:, group * WINDOW:group * WINDOW + group * t_new] = np.where(okn, dn, -1)
    return d.astype(np.int32)


def _group_ones():
    r = lax.broadcasted_iota(jnp.int32, (LANES, LANES), 0)
    c = lax.broadcasted_iota(jnp.int32, (LANES, LANES), 1)
    low_c = jnp.where(c < HEAD_DIM, 1.0, 0.0)
    return jnp.where(r < HEAD_DIM, low_c, 1.0 - low_c).astype(BF16)


def _head_norm(x, gmat, ln2):
    sq = x * x
    hi = sq.astype(BF16)
    lo = (sq - hi.astype(F32)).astype(BF16)
    ms = (_dot(hi, gmat) + _dot(lo, gmat)) * (1.0 / HEAD_DIM)
    return x * lax.rsqrt(ms + EPS) * ln2


def _split_heads(x):
    lane = lax.broadcasted_iota(jnp.int32, (x.shape[0], LANES), 1)
    low = lane < HEAD_DIM
    lo_parts, hi_parts = [], []
    for c in range(KV_WIDTH_A // LANES):
        xc = x[:, c * LANES:(c + 1) * LANES]
        xr = pltpu.roll(xc, HEAD_DIM, 1)
        lo_parts += [jnp.where(low, xc, 0.0).astype(BF16), jnp.where(low, xr, 0.0).astype(BF16)]
        hi_parts += [jnp.where(low, 0.0, xr).astype(BF16), jnp.where(low, 0.0, xc).astype(BF16)]
    return lo_parts, hi_parts


def _attend(q_cols, k_parts, v_parts, bias_ref, sink_ref):
    mq = q_cols[0].shape[0]
    pairs_per_kv = N_HEADS_A // N_KV_A // 2
    assert pairs_per_kv == 2
    first_rows = lax.broadcasted_iota(jnp.int32, (pairs_per_kv * mq, 1), 0) < mq
    outs = [None] * (N_HEADS_A // 2)
    for kh in range(N_KV_A):
        pcs = [kh * pairs_per_kv + p for p in range(pairs_per_kv)]
        q_stack = jnp.concatenate([q_cols[pc] for pc in pcs], axis=0)
        for half in range(2):
            heads = [2 * pc + half for pc in pcs]
            bias = jnp.concatenate([bias_ref[h] for h in heads], axis=0)
            sink = jnp.where(first_rows, sink_ref[heads[0]], sink_ref[heads[1]]) * LOG2E
            s = _dot_nt(q_stack, k_parts[half][kh]) + bias
            m = jnp.maximum(jnp.max(s, axis=-1, keepdims=True), sink)
            e = jnp.exp2(s - m)
            den = jnp.sum(e, axis=-1, keepdims=True) + jnp.exp2(sink - m)
            o = _dot(e.astype(BF16), v_parts[half][kh]) * (1.0 / den)
            for p, pc in enumerate(pcs):
                o_p = o[p * mq:(p + 1) * mq, :]
                outs[pc] = o_p if outs[pc] is None else outs[pc] + o_p
    return outs


def _gates_attn_kernel(blocks_per_seq, sink_ref, h_ref, hs_ref, w_ref, q_ref, k_ref, v_ref, lnq_ref, lnk_ref,
                       bias_ref, g_ref, gs_ref, o_ref, kn_ref, wb_scr, kprev_scr, vprev_scr, o_scr, kn_scr):
    j = pl.program_id(0)
    i = pl.program_id(1)
    step = j * pl.num_programs(1) + i

    @pl.when(i == 0)
    def _():
        wb_scr[...] = w_ref[...].astype(BF16)

    @pl.when(i == pl.num_programs(1) - 1)
    def _():
        gs_ref[...] = jax.nn.sigmoid(_dot_nt(hs_ref[...], wb_scr[...])).astype(gs_ref.dtype)

    @pl.when(step == 0)
    def _():
        kprev_scr[...] = jnp.zeros_like(kprev_scr)
        vprev_scr[...] = jnp.zeros_like(vprev_scr)

    gmat = _group_ones()
    lnq = lnq_ref[...]
    lnk = lnk_ref[...]
    scale = HEAD_DIM ** -0.5 * LOG2E
    for t in range(ATTN_BLOCKS_PER_STEP):
        rows = slice(t * WINDOW, (t + 1) * WINDOW)
        block = step * ATTN_BLOCKS_PER_STEP + t
        table = bias_ref.at[jnp.where(block % blocks_per_seq == 0, 0, 1)]
        kn_own = jnp.concatenate(
            [_head_norm(k_ref[rows, c * LANES:(c + 1) * LANES], gmat, lnk) for c in range(KV_WIDTH_A // LANES)],
            axis=1)
        kn_scr[rows, :] = kn_own
        k_own = _split_heads(kn_own)
        v_own = _split_heads(v_ref[rows, :])
        k_parts = [[jnp.concatenate([kprev_scr[half, kh], k_own[half][kh]], axis=0) for kh in range(N_KV_A)]
                   for half in range(2)]
        v_parts = [[jnp.concatenate([vprev_scr[half, kh], v_own[half][kh]], axis=0) for kh in range(N_KV_A)]
                   for half in range(2)]
        q_cols = [(_head_norm(q_ref[rows, c * LANES:(c + 1) * LANES], gmat, lnq) * scale).astype(BF16)
                  for c in range(WIDTH_A // LANES)]
        outs = _attend(q_cols, k_parts, v_parts, table, sink_ref)
        for c, o in enumerate(outs):
            o_scr[rows, c * LANES:(c + 1) * LANES] = o.astype(o_scr.dtype)
        for half in range(2):
            for kh in range(N_KV_A):
                kprev_scr[half, kh] = k_own[half][kh]
                vprev_scr[half, kh] = v_own[half][kh]

    g_ref[...] = jax.nn.sigmoid(_dot_nt(h_ref[...], wb_scr[...])).astype(g_ref.dtype)
    o_ref[...] = o_scr[...]
    kn_ref[...] = kn_scr[...]


def _gates_attn(h, hs, w_in_t, proj, sinks_l, lnq2, lnk2, bias_p, l, tm, seq):
    r = h.shape[0]
    rs = hs.shape[0]
    tn = TN_GATES
    nj, nt = GATE_COLS // tn, r // tm
    rows = ATTN_BLOCKS_PER_STEP * WINDOW
    assert nj * nt * rows == r
    kcol = OFF_KA // KV_WIDTH_A
    vcol = OFF_VA // KV_WIDTH_A
    step = lambda j, i: j * nt + i
    return pl.pallas_call(
        functools.partial(_gates_attn_kernel, seq // WINDOW),
        grid=(nj, nt),
        in_specs=[
            pl.BlockSpec(memory_space=pltpu.SMEM),
            pl.BlockSpec((tm, D_MODEL), lambda j, i: (i, 0)),
            pl.BlockSpec((rs, D_MODEL), lambda j, i: (0, 0)),
            pl.BlockSpec((pl.Squeezed(), pl.Element(tn), pl.Element(D_MODEL)),
                         lambda j, i: (l, pl.multiple_of(OFF_GA + j * tn, GATE_RANK), 0)),
            pl.BlockSpec((rows, WIDTH_A), lambda j, i: (step(j, i), 0)),
            pl.BlockSpec((rows, KV_WIDTH_A), lambda j, i: (step(j, i), kcol)),
            pl.BlockSpec((rows, KV_WIDTH_A), lambda j, i: (step(j, i), vcol)),
            pl.BlockSpec((1, LANES), lambda j, i: (0, 0)),
            pl.BlockSpec((1, LANES), lambda j, i: (0, 0)),
            pl.BlockSpec((2, N_HEADS_A, WINDOW, 2 * WINDOW), lambda j, i: (0, 0, 0, 0)),
        ],
        out_specs=[
            pl.BlockSpec((tm, tn), lambda j, i: (i, j)),
            pl.BlockSpec((rs, tn), lambda j, i: (0, j)),
            pl.BlockSpec((rows, WIDTH_A), lambda j, i: (step(j, i), 0)),
            pl.BlockSpec((rows, KV_WIDTH_A), lambda j, i: (step(j, i), 0)),
        ],
        out_shape=[
            jax.ShapeDtypeStruct((r, GATE_COLS), BF16),
            jax.ShapeDtypeStruct((rs, GATE_COLS), BF16),
            jax.ShapeDtypeStruct((r, WIDTH_A), BF16),
            jax.ShapeDtypeStruct((r, KV_WIDTH_A), F32),
        ],
        scratch_shapes=[
            pltpu.VMEM((tn, D_MODEL), BF16),
            pltpu.VMEM((2, N_KV_A, WINDOW, LANES), BF16),
            pltpu.VMEM((2, N_KV_A, WINDOW, LANES), BF16),
            pltpu.VMEM((rows, WIDTH_A), BF16),
            pltpu.VMEM((rows, KV_WIDTH_A), F32),
        ],
        compiler_params=_cparams(("arbitrary", "arbitrary")),
        name="gates_attn",
    )(sinks_l, h, hs, w_in_t, proj, proj, proj, lnq2, lnk2, bias_p)


def _attn_sample_kernel(sink_ref, q_ref, kn_in_ref, vn_ref, wk_ref, wv_ref, lnq_ref, lnk_ref, bias_ref,
                        o_ref, kn_ref):
    gmat = _group_ones()
    lnq = lnq_ref[...]
    lnk = lnk_ref[...]
    scale = HEAD_DIM ** -0.5 * LOG2E
    rows = q_ref.shape[0]
    g = wk_ref.shape[0]
    kn_new = jnp.concatenate(
        [_head_norm(kn_in_ref[:, c * LANES:(c + 1) * LANES], gmat, lnk) for c in range(KV_WIDTH_A // LANES)], axis=1)
    kn_ref[...] = kn_new
    pad = jnp.zeros((LANES - rows, KV_WIDTH_A), F32)

    def window_rows(ref):
        heads_per_col = LANES // HEAD_DIM
        cols = []
        for c in range(KV_WIDTH_A // LANES):
            blocks = [jnp.concatenate([ref[b, c * heads_per_col + j] for j in range(heads_per_col)], axis=0).T
                      for b in range(g)]
            cols.append(jnp.concatenate(blocks, axis=0))
        return jnp.concatenate(cols, axis=1)

    k = jnp.concatenate([window_rows(wk_ref), kn_new, pad], axis=0)
    v = jnp.concatenate([window_rows(wv_ref), vn_ref[...], pad], axis=0)
    q_cols = [(_head_norm(q_ref[:, c * LANES:(c + 1) * LANES], gmat, lnq) * scale).astype(BF16)
              for c in range(WIDTH_A // LANES)]
    outs = _attend(q_cols, _split_heads(k), _split_heads(v), bias_ref, sink_ref)
    for c, o in enumerate(outs):
        o_ref[:, c * LANES:(c + 1) * LANES] = o.astype(o_ref.dtype)


def _attn_sample(proj, win_k, win_v, sinks_l, lnq2, lnk2, bias_s, l, batch, t_new):
    g = SAMPLE_GROUP_A
    rows = g * t_new
    nk = g * WINDOW + LANES
    kcol = OFF_KA // KV_WIDTH_A
    vcol = OFF_VA // KV_WIDTH_A
    return pl.pallas_call(
        _attn_sample_kernel,
        grid=(batch // g,),
        in_specs=[
            pl.BlockSpec(memory_space=pltpu.SMEM),
            pl.BlockSpec((rows, WIDTH_A), lambda i: (i, 0)),
            pl.BlockSpec((rows, KV_WIDTH_A), lambda i: (i, kcol)),
            pl.BlockSpec((rows, KV_WIDTH_A), lambda i: (i, vcol)),
            pl.BlockSpec((None, g, N_KV_A, HEAD_DIM, WINDOW), lambda i: (l, i, 0, 0, 0)),
            pl.BlockSpec((None, g, N_KV_A, HEAD_DIM, WINDOW), lambda i: (l, i, 0, 0, 0)),
            pl.BlockSpec((1, LANES), lambda i: (0, 0)),
            pl.BlockSpec((1, LANES), lambda i: (0, 0)),
            pl.BlockSpec((N_HEADS_A, rows, nk), lambda i: (0, 0, 0)),
        ],
        out_specs=[
            pl.BlockSpec((rows, WIDTH_A), lambda i: (i, 0)),
            pl.BlockSpec((rows, KV_WIDTH_A), lambda i: (i, 0)),
        ],
        out_shape=[
            jax.ShapeDtypeStruct((batch * t_new, WIDTH_A), BF16),
            jax.ShapeDtypeStruct((batch * t_new, KV_WIDTH_A), F32),
        ],
        compiler_params=_cparams(("parallel",)),
        name="attn_sample",
    )(sinks_l, proj, proj, proj, win_k, win_v, lnq2, lnk2, bias_s)


def _log_decay(gk, w2_ref, bgk_ref):
    x = _dot(gk.astype(BF16), w2_ref[...]) + bgk_ref[...]
    log_sig = jnp.minimum(x, 0.0) - jnp.log(1.0 + jnp.exp(-jnp.abs(x)))
    return log_sig * (1.0 / GATE_TAU)


def _column(row):
    r = lax.broadcasted_iota(jnp.int32, (LANES, LANES), 0)
    c = lax.broadcasted_iota(jnp.int32, (LANES, LANES), 1)
    return jnp.sum(jnp.where(r == c, row, 0.0), axis=1, keepdims=True)


def _gla_out(o, ln_o, rgate):
    return (_rms(o, ln_o) * (rgate * jax.nn.sigmoid(rgate))).astype(BF16)


def _gla_prompt_kernel(q_ref, k_ref, va_ref, vb_ref, ra_ref, rb_ref, gk_ref, w2_ref, bgk_ref, lno_ref,
                       o_ref, s_out_ref, s_scr, b_scr):
    t = pl.program_id(1)
    c_len = GLA_CHUNK_P
    tb = q_ref.shape[0]
    n_chunks = tb // c_len
    half = c_len // 2
    scale = DK_B ** -0.5

    @pl.when(t == 0)
    def _():
        s_scr[...] = jnp.zeros_like(s_scr)

    shift = c_len.bit_length() - 1
    rb = lax.broadcasted_iota(jnp.int32, (tb, tb), 0)
    cb = lax.broadcasted_iota(jnp.int32, (tb, tb), 1)
    same_chunk = lax.shift_right_logical(rb, shift) == lax.shift_right_logical(cb, shift)
    tri_blocks = jnp.where(same_chunk & (rb >= cb), 1.0, 0.0).astype(BF16)
    b_scr[...] = _dot_exact_lhs01(tri_blocks, _log_decay(gk_ref[...], w2_ref, bgk_ref))

    ri = lax.broadcasted_iota(jnp.int32, (c_len, c_len), 0)
    ci = lax.broadcasted_iota(jnp.int32, (c_len, c_len), 1)
    quarter = half // 2
    same_half = jnp.where(ci < half, jnp.where(ri < half, 1, 0), jnp.where(ri < half, 0, 1))
    own_causal = (ri >= ci) & (same_half == 1)
    upper_row = lax.broadcasted_iota(jnp.int32, (c_len, 1), 0) >= half
    ln_o = lno_ref[...]
    v_refs = (va_ref, vb_ref)
    r_refs = (ra_ref, rb_ref)

    def chunk(c, carry):
        rows = pl.ds(pl.multiple_of(c * c_len, c_len), c_len)
        for h in range(N_HEADS_B):
            ks = slice(h * DK_B, (h + 1) * DK_B)
            vs = slice((h % 2) * DV_B, (h % 2 + 1) * DV_B)
            bh = b_scr[rows, ks]
            b_last = bh[c_len - 1:c_len, :]
            b_half = bh[half - 1:half, :]
            q = q_ref[rows, ks] * scale
            k = k_ref[rows, ks]
            v = v_refs[h // 2][rows, vs].astype(BF16)
            q_inter = (q * jnp.exp(bh)).astype(BF16)
            r_own = jnp.where(upper_row, bh[half + quarter - 1:half + quarter, :], bh[quarter - 1:quarter, :])
            a_own = _dot_nt((q * jnp.exp(bh - r_own)).astype(BF16), (k * jnp.exp(r_own - bh)).astype(BF16))
            q_x = jnp.where(upper_row, q * jnp.exp(bh - b_half), 0.0).astype(BF16)
            k_x = jnp.where(upper_row, 0.0, k * jnp.exp(b_half - bh)).astype(BF16)
            a = (jnp.where(own_causal, a_own, 0.0) + _dot_nt(q_x, k_x)).astype(BF16)
            k_state = (k * jnp.exp(b_last - bh)).astype(BF16)
            s_old = s_scr[h]
            o = _dot(a, v) + _dot(q_inter, s_old.astype(BF16))
            s_scr[h] = _column(jnp.exp(b_last)) * s_old + _dot_tn(k_state, v)
            o_ref[rows, h * DV_B:(h + 1) * DV_B] = _gla_out(o, ln_o, r_refs[h // 2][rows, vs])
        return carry

    lax.fori_loop(0, n_chunks, chunk, 0, unroll=4)

    @pl.when(t == pl.num_programs(1) - 1)
    def _():
        s_out_ref[...] = s_scr[...]


def _gla_prompt(proj, gk, w2_l, bgk_l, lno_l, batch, seq):
    tb = GLA_TBLOCK
    nt = seq // tb
    w = 2 * DV_B
    row = lambda b, t: b * nt + t
    spec = lambda col: pl.BlockSpec((tb, w), lambda b, t: (row(b, t), col))
    return pl.pallas_call(
        _gla_prompt_kernel,
        grid=(batch, nt),
        in_specs=[
            spec(OFF_QB // w), spec(OFF_KB // w),
            spec(OFF_VB // w), spec(OFF_VB // w + 1),
            spec(OFF_RB // w), spec(OFF_RB // w + 1),
            pl.BlockSpec((tb, GATE_RANK), lambda b, t: (row(b, t), 0)),
            pl.BlockSpec((GATE_RANK, WIDTH_BK), lambda b, t: (0, 0)),
            pl.BlockSpec((1, WIDTH_BK), lambda b, t: (0, 0)),
            pl.BlockSpec((1, DV_B), lambda b, t: (0, 0)),
        ],
        out_specs=[
            pl.BlockSpec((tb, WIDTH_BV), lambda b, t: (row(b, t), 0)),
            pl.BlockSpec((None, N_HEADS_B, DK_B, DV_B), lambda b, t: (b, 0, 0, 0)),
        ],
        out_shape=[
            jax.ShapeDtypeStruct((batch * seq, WIDTH_BV), BF16),
            jax.ShapeDtypeStruct((batch, N_HEADS_B, DK_B, DV_B), F32),
        ],
        scratch_shapes=[pltpu.VMEM((N_HEADS_B, DK_B, DV_B), F32), pltpu.VMEM((tb, WIDTH_BK), F32)],
        compiler_params=_cparams(("parallel", "arbitrary")),
        name="gla_prompt",
    )(proj, proj, proj, proj, proj, proj, gk, w2_l, bgk_l, lno_l)


def _gla_sample_kernel(t_new, q_ref, k_ref, va_ref, vb_ref, ra_ref, rb_ref, gk_ref, w2_ref, bgk_ref, lno_ref,
                       s_ref, o_ref, s_out_ref):
    rows = q_ref.shape[0]
    n_seq = rows // t_new
    scale = DK_B ** -0.5
    ri = lax.broadcasted_iota(jnp.int32, (rows, rows), 0)
    ci = lax.broadcasted_iota(jnp.int32, (rows, rows), 1)
    same = None
    for sq in range(n_seq):
        lo, hi = sq * t_new, (sq + 1) * t_new
        blk = (ri >= lo) & (ri < hi) & (ci >= lo) & (ci < hi)
        same = blk if same is None else (same | blk)
    causal = same & (ri >= ci)
    tri = jnp.where(causal, 1.0, 0.0).astype(BF16)
    rcol = lax.broadcasted_iota(jnp.int32, (rows, 1), 0)
    ln_o = lno_ref[...]
    v_refs = (va_ref, vb_ref)
    r_refs = (ra_ref, rb_ref)

    g = _log_decay(gk_ref[...], w2_ref, bgk_ref)
    b = _dot_exact_lhs01(tri, g)
    for h in range(N_HEADS_B):
        ks = slice(h * DK_B, (h + 1) * DK_B)
        vs = slice((h % 2) * DV_B, (h % 2 + 1) * DV_B)
        bh = b[:, ks]
        gh = g[:, ks]
        q = q_ref[:, ks] * scale
        k = k_ref[:, ks]
        v = v_refs[h // 2][:, vs].astype(BF16)
        q_inter = q * jnp.exp(bh)
        k_t = (k * jnp.exp(-bh)).astype(BF16)
        a = jnp.where(causal, _dot_nt(q_inter.astype(BF16), k_t), 0.0).astype(BF16)
        o = _dot(a, v)
        for sq in range(n_seq):
            mine = (rcol >= sq * t_new) & (rcol < (sq + 1) * t_new)
            b_last = jnp.sum(jnp.where(mine, gh, 0.0), axis=0, keepdims=True)
            k_state = jnp.where(mine, k * jnp.exp(b_last - bh), 0.0).astype(BF16)
            q_mine = jnp.where(mine, q_inter, 0.0).astype(BF16)
            s_old = s_ref[sq, h]
            o = o + _dot(q_mine, s_old.astype(BF16))
            s_out_ref[sq, h] = _column(jnp.exp(b_last)) * s_old + _dot_tn(k_state, v)
        o_ref[:, h * DV_B:(h + 1) * DV_B] = _gla_out(o, ln_o, r_refs[h // 2][:, vs])


def _gla_sample(proj, gk, w2_l, bgk_l, lno_l, state, l, batch, t_new):
    g = SAMPLE_GROUP_B
    rows = g * t_new
    w = 2 * DV_B
    spec = lambda col: pl.BlockSpec((rows, w), lambda i: (i, col))
    return pl.pallas_call(
        functools.partial(_gla_sample_kernel, t_new),
        grid=(batch // g,),
        in_specs=[
            spec(OFF_QB // w), spec(OFF_KB // w),
            spec(OFF_VB // w), spec(OFF_VB // w + 1),
            spec(OFF_RB // w), spec(OFF_RB // w + 1),
            pl.BlockSpec((rows, GATE_RANK), lambda i: (i, 0)),
            pl.BlockSpec((GATE_RANK, WIDTH_BK), lambda i: (0, 0)),
            pl.BlockSpec((1, WIDTH_BK), lambda i: (0, 0)),
            pl.BlockSpec((1, DV_B), lambda i: (0, 0)),
            pl.BlockSpec((None, g, N_HEADS_B, DK_B, DV_B), lambda i: (l, i, 0, 0, 0)),
        ],
        out_specs=[
            pl.BlockSpec((rows, WIDTH_BV), lambda i: (i, 0)),
            pl.BlockSpec((g, N_HEADS_B, DK_B, DV_B), lambda i: (i, 0, 0, 0)),
        ],
        out_shape=[
            jax.ShapeDtypeStruct((batch * t_new, WIDTH_BV), BF16),
            jax.ShapeDtypeStruct((batch, N_HEADS_B, DK_B, DV_B), F32),
        ],
        compiler_params=_cparams(("parallel",)),
        name="gla_sample",
    )(proj, proj, proj, proj, proj, proj, gk, w2_l, bgk_l, lno_l, state)


def _merge_rows(oa_ref, ob_ref, sa_ref, sb_ref, x_ref, woa_ref, wob_ref, wout_ref, ln_ref, x_out_ref, h_ref):
    a = _dot(oa_ref[...], woa_ref[...])
    b = _dot(ob_ref[...], wob_ref[...])
    mix = (sa_ref[...].astype(F32) * a + sb_ref[...].astype(F32) * b).astype(BF16)
    x_new = x_ref[...] + _dot(mix, wout_ref[...])
    x_out_ref[...] = x_new
    h_ref[...] = _rms(x_new, ln_ref[...]).astype(BF16)


def _merge_kernel(oa_ref, ob_ref, sa_ref, sb_ref, x_ref, oa_s_ref, ob_s_ref, sa_s_ref, sb_s_ref, x_s_ref,
                  woa_ref, wob_ref, wout_ref, ln_ref, x_out_ref, h_ref, x_s_out_ref, h_s_ref):
    _merge_rows(oa_ref, ob_ref, sa_ref, sb_ref, x_ref, woa_ref, wob_ref, wout_ref, ln_ref, x_out_ref, h_ref)

    @pl.when(pl.program_id(0) == pl.num_programs(0) - 1)
    def _():
        _merge_rows(oa_s_ref, ob_s_ref, sa_s_ref, sb_s_ref, x_s_ref, woa_ref, wob_ref, wout_ref, ln_ref,
                    x_s_out_ref, h_s_ref)


def _merge(prompt, sample, w_oa, w_ob, w_out, ln2, l, tm):
    r = prompt[3].shape[0]
    rs = sample[3].shape[0]
    resident = dict(pipeline_mode=pl.Buffered(1))
    row = lambda i: (i, 0)
    fixed = lambda i: (0, 0)
    group_specs = lambda rows, at, at1: [
        pl.BlockSpec((rows, WIDTH_A), at),
        pl.BlockSpec((rows, WIDTH_BV), at),
        pl.BlockSpec((rows, D_MODEL), at),
        pl.BlockSpec((rows, D_MODEL), at1),
        pl.BlockSpec((rows, D_MODEL), at),
    ]
    operands = lambda g: (g[0], g[1], g[2], g[2], g[3])
    return pl.pallas_call(
        _merge_kernel,
        grid=(r // tm,),
        in_specs=group_specs(tm, row, lambda i: (i, 1)) + group_specs(rs, fixed, lambda i: (0, 1)) + [
            pl.BlockSpec((None, WIDTH_A, D_MODEL), lambda i: (l, 0, 0), **resident),
            pl.BlockSpec((None, WIDTH_BV, D_MODEL), lambda i: (l, 0, 0), **resident),
            pl.BlockSpec((None, D_MODEL, D_MODEL), lambda i: (l, 0, 0), **resident),
            _ln_spec(l, 1),
        ],
        out_specs=[pl.BlockSpec((tm, D_MODEL), row), pl.BlockSpec((tm, D_MODEL), row),
                   pl.BlockSpec((rs, D_MODEL), fixed), pl.BlockSpec((rs, D_MODEL), fixed)],
        out_shape=[jax.ShapeDtypeStruct((r, D_MODEL), F32), jax.ShapeDtypeStruct((r, D_MODEL), BF16),
                   jax.ShapeDtypeStruct((rs, D_MODEL), F32), jax.ShapeDtypeStruct((rs, D_MODEL), BF16)],
        compiler_params=_cparams(("arbitrary",)),
        name="merge",
    )(*operands(prompt), *operands(sample), w_oa, w_ob, w_out, ln2)


def _gelu_tanh(x):
    return x * (0.5 * (1.0 + jnp.tanh(math.sqrt(2.0 / math.pi) * (x + 0.044715 * (x * x * x)))))


def _conv3(u, p1, p2, w_ref, b_ref):
    return w_ref[0:1, :] * p2 + w_ref[1:2, :] * p1 + w_ref[2:3, :] * u + b_ref[...]


def _history_selectors(n_seq, t_new):
    hist = CONV_W - 1
    e1 = np.zeros((n_seq * t_new, n_seq * hist), np.float32)
    e2 = np.zeros((n_seq * t_new, n_seq * hist), np.float32)
    for b in range(n_seq):
        for t in range(min(t_new, hist)):
            if t < 1:
                e1[b * t_new + t, b * hist + hist - 1 + t] = 1.0
            e2[b * t_new + t, b * hist + t] = 1.0
    return jnp.asarray(e1, BF16), jnp.asarray(e2, BF16)


def _ffn_up_prompt_kernel(tiles_per_seq, h_ref, wv_ref, wg_ref, cv_ref, cg_ref, bv_ref, bg_ref,
                          act_ref, tv_ref, tg_ref, wvb_scr, wgb_scr, carry_v, carry_g):
    i = pl.program_id(1)

    @pl.when(i == 0)
    def _():
        wvb_scr[...] = wv_ref[...].astype(BF16)
        wgb_scr[...] = wg_ref[...].astype(BF16)

    h = h_ref[...]
    tm = h.shape[0]
    first = (i % tiles_per_seq) == 0
    row8 = lax.broadcasted_iota(jnp.int32, (8, 1), 0)

    def branch(wb_scr, carry, c_ref, b_ref, tail_ref):
        u = _dot(h, wb_scr[...])
        prev = jnp.where(first, 0.0, carry[...])
        tail = u[tm - 8:tm, :]
        carry[...] = tail
        tail_ref[...] = tail
        p1 = pltpu.roll(u, 1, 0)
        p2 = pltpu.roll(u, 2, 0)
        body = _conv3(u, p1, p2, c_ref, b_ref)
        p1_top = jnp.where(row8 < 1, pltpu.roll(prev, 1, 0), p1[0:8, :])
        p2_top = jnp.where(row8 < 2, pltpu.roll(prev, 2, 0), p2[0:8, :])
        top = _conv3(u[0:8, :], p1_top, p2_top, c_ref, b_ref)
        return jnp.concatenate([top, body[8:, :]], axis=0)

    gate = _gelu_tanh(branch(wgb_scr, carry_g, cg_ref, bg_ref, tg_ref))
    val = branch(wvb_scr, carry_v, cv_ref, bv_ref, tv_ref)
    act_ref[...] = (gate * val).astype(act_ref.dtype)


def _ffn_up_sample_kernel(pos_ref, e1_ref, e2_ref, h_ref, wv_ref, wg_ref, cv_ref, cg_ref, bv_ref, bg_ref,
                          sv_ref, sg_ref, act_ref, uv_ref, ug_ref):
    h = h_ref[...]
    pos = pos_ref[...]
    e1 = e1_ref[...]
    e2 = e2_ref[...]

    def branch(w_ref, c_ref, b_ref, s_ref, u_ref):
        u = _dot(h, w_ref[...].astype(BF16))
        u_ref[...] = u
        cached = s_ref[...]
        p1 = jnp.where(pos >= 1, pltpu.roll(u, 1, 0), _dot_exact_lhs01(e1, cached))
        p2 = jnp.where(pos >= 2, pltpu.roll(u, 2, 0), _dot_exact_lhs01(e2, cached))
        return _conv3(u, p1, p2, c_ref, b_ref)

    val = branch(wv_ref, cv_ref, bv_ref, sv_ref, uv_ref)
    gate = branch(wg_ref, cg_ref, bg_ref, sg_ref, ug_ref)
    act_ref[...] = (_gelu_tanh(gate) * val).astype(act_ref.dtype)


def _ffn_up_kernel(tiles_per_seq, pos_ref, e1_ref, e2_ref, h_ref, hs_ref, wv_ref, wg_ref, cv_ref, cg_ref,
                   bv_ref, bg_ref, sv_ref, sg_ref,
                   act_ref, tv_ref, tg_ref, acts_ref, uv_ref, ug_ref, wvb_scr, wgb_scr, carry_v, carry_g):
    _ffn_up_prompt_kernel(tiles_per_seq, h_ref, wv_ref, wg_ref, cv_ref, cg_ref, bv_ref, bg_ref,
                          act_ref, tv_ref, tg_ref, wvb_scr, wgb_scr, carry_v, carry_g)

    @pl.when(pl.program_id(1) == pl.num_programs(1) - 1)
    def _():
        _ffn_up_sample_kernel(pos_ref, e1_ref, e2_ref, hs_ref, wvb_scr, wgb_scr, cv_ref, cg_ref, bv_ref, bg_ref,
                              sv_ref, sg_ref, acts_ref, uv_ref, ug_ref)


def _ffn_up(h, hs, w_up, w_conv, b_conv, conv_state, l, tm, tiles_per_seq, t_new):
    r = h.shape[0]
    rs = hs.shape[0]
    tf = TF_UP
    nt = r // tm
    nj = D_FF // tf
    ns = conv_state.shape[1]
    w_spec = lambda off: pl.BlockSpec((None, D_MODEL, tf), lambda j, i: (l, 0, j + off))
    c_spec = lambda off: pl.BlockSpec((None, CONV_W, tf), lambda j, i: (l, 0, j + off))
    b_spec = lambda off: pl.BlockSpec((None, 1, tf), lambda j, i: (l, 0, j + off))
    s_spec = lambda off: pl.BlockSpec((None, ns, tf), lambda j, i: (l, 0, j + off))
    tail_spec = pl.BlockSpec((None, 8, tf), lambda j, i: (i, 0, j))
    u_spec = pl.BlockSpec((rs, tf), lambda j, i: (0, j))
    fixed = lambda shape: pl.BlockSpec(shape, lambda j, i: (0, 0))
    pos = jnp.asarray((np.arange(rs) % t_new).astype(np.int32).reshape(rs, 1))
    e1, e2 = _history_selectors(rs // t_new, t_new)
    return pl.pallas_call(
        functools.partial(_ffn_up_kernel, tiles_per_seq),
        grid=(nj, nt),
        in_specs=[
            fixed((rs, 1)), fixed((rs, ns)), fixed((rs, ns)),
            pl.BlockSpec((tm, D_MODEL), lambda j, i: (i, 0)),
            fixed((rs, D_MODEL)),
            w_spec(0), w_spec(nj), c_spec(0), c_spec(nj), b_spec(0), b_spec(nj), s_spec(0), s_spec(nj),
        ],
        out_specs=[
            pl.BlockSpec((None, tm, tf), lambda j, i: (j, i, 0)), tail_spec, tail_spec,
            pl.BlockSpec((None, rs, tf), lambda j, i: (j, 0, 0)), u_spec, u_spec,
        ],
        out_shape=[
            jax.ShapeDtypeStruct((nj, r, tf), BF16),
            jax.ShapeDtypeStruct((nt, 8, D_FF), F32),
            jax.ShapeDtypeStruct((nt, 8, D_FF), F32),
            jax.ShapeDtypeStruct((nj, rs, tf), BF16),
            jax.ShapeDtypeStruct((rs, D_FF), F32),
            jax.ShapeDtypeStruct((rs, D_FF), F32),
        ],
        scratch_shapes=[
            pltpu.VMEM((D_MODEL, tf), BF16),
            pltpu.VMEM((D_MODEL, tf), BF16),
            pltpu.VMEM((8, tf), F32),
            pltpu.VMEM((8, tf), F32),
        ],
        compiler_params=_cparams(("arbitrary", "arbitrary")),
        name="ffn_up",
    )(pos, e1, e2, h, hs, w_up, w_up, w_conv, w_conv, b_conv, b_conv, conv_state, conv_state)


def _down_accumulate(act_ref, w_ref, x_ref, x_out_ref):
    @pl.when(pl.program_id(1) == 0)
    def _():
        x_out_ref[...] = x_ref[...]

    act = act_ref[...]
    step = 512
    for n in range(D_MODEL // step):
        cols = slice(n * step, (n + 1) * step)
        x_out_ref[:, cols] += _dot(act, w_ref[:, cols])


def _down_last_kernel(act_ref, w_ref, x_ref, x_out_ref):
    _down_accumulate(act_ref, w_ref, x_ref, x_out_ref)


def _down_norm_kernel(act_ref, w_ref, x_ref, ln_ref, wgk_ref, x_out_ref, h_ref, gk_ref):
    _down_accumulate(act_ref, w_ref, x_ref, x_out_ref)

    @pl.when(pl.program_id(1) == pl.num_programs(1) - 1)
    def _():
        h = _rms(x_out_ref[...], ln_ref[...]).astype(BF16)
        h_ref[...] = h
        gk_ref[...] = _dot_nt(h, wgk_ref[...].astype(BF16))


def _down(act, w_down, x, ln1, w_in_t, l, tm, with_norm):
    nk, r, tk = act.shape
    row = lambda i, k: (i, 0)
    in_specs = [
        pl.BlockSpec((None, tm, tk), lambda i, k: (k, i, 0)),
        pl.BlockSpec((None, tk, D_MODEL), lambda i, k: (l, k, 0)),
        pl.BlockSpec((tm, D_MODEL), row),
    ]
    out_specs = [pl.BlockSpec((tm, D_MODEL), row)]
    out_shape = [jax.ShapeDtypeStruct((r, D_MODEL), F32)]
    args = [act, w_down, x]
    if with_norm:
        in_specs += [_ln_spec(l + 1, 2), _wgk_spec(l + 1, 2)]
        out_specs += [pl.BlockSpec((tm, D_MODEL), row), pl.BlockSpec((tm, GATE_RANK), row)]
        out_shape += [jax.ShapeDtypeStruct((r, D_MODEL), BF16), jax.ShapeDtypeStruct((r, GATE_RANK), F32)]
        args += [ln1, w_in_t]
    out = pl.pallas_call(
        _down_norm_kernel if with_norm else _down_last_kernel,
        grid=(r // tm, nk),
        in_specs=in_specs,
        out_specs=out_specs,
        out_shape=out_shape,
        compiler_params=_cparams(("parallel", "arbitrary")),
        name="down_norm" if with_norm else "down_last",
    )(*args)
    return out if with_norm else (out[0], None, None)


def kernel(x_prompt, x_sample, cache_win_k, cache_win_v, state_gla, state_conv, rel_bias,
           ln1, w_in, ln_q, ln_k, sinks, w_gk2, b_gk, ln_o, w_oa, w_ob, w_out,
           ln2, w_up, w_conv, b_conv, w_down):
    bp, seq, _ = x_prompt.shape
    bs, t_new, _ = x_sample.shape
    tm_p = TM_PROMPT
    tm_s = bs * t_new
    tiles_per_seq = seq // tm_p

    ln1_3 = ln1.reshape(DEPTH, 1, D_MODEL)
    ln2_3 = ln2.reshape(DEPTH, 1, D_MODEL)
    w_in_t = jnp.swapaxes(w_in, 1, 2)
    w_oa_b, w_ob_b, w_out_b, w_down_b = (w.astype(BF16) for w in (w_oa, w_ob, w_out, w_down))
    w2_b = w_gk2.astype(BF16)
    b_conv3 = b_conv.reshape(DEPTH, 1, 2 * D_FF)
    lnq2 = jnp.tile(ln_q, (1, 2)).reshape(DEPTH, 1, LANES)
    lnk2 = jnp.tile(ln_k, (1, 2)).reshape(DEPTH, 1, LANES)

    bias_p = jnp.stack([_bias_table(rel_bias, jnp.asarray(_dmap_prompt(first))) for first in (True, False)])
    bias_s = _bias_table(rel_bias, jnp.asarray(_dmap_sample(t_new, SAMPLE_GROUP_A)))

    xp = x_prompt.reshape(bp * seq, D_MODEL)
    xs = x_sample.reshape(bs * t_new, D_MODEL)
    win_k = jnp.transpose(cache_win_k, (0, 1, 3, 4, 2))
    win_v = jnp.transpose(cache_win_v, (0, 1, 3, 4, 2))
    conv_state = state_conv.reshape(DEPTH, bs * (CONV_W - 1), 2 * D_FF)

    hp, gkp = _norm(xp, ln1_3, w_in_t, 0, 512)
    hs, gks = _norm(xs, ln1_3, w_in_t, 0, tm_s)

    kp, vp, gp, cp = [], [], [], []
    ksm, vsm, gsm, csm = [], [], [], []
    for l in range(DEPTH):
        lq, lk, sk = lnq2[l], lnk2[l], sinks[l]
        w2_l, bgk_l, lno_l = w2_b[l], b_gk[l].reshape(1, WIDTH_BK), ln_o[l].reshape(1, DV_B)
        more = l + 1 < DEPTH

        proj, proj_s = _proj_main(hp, hs, w_in_t, l, tm_p)
        gates, gates_s, oa, kn = _gates_attn(hp, hs, w_in_t, proj, sk, lq, lk, bias_p, l, tm_p, seq)
        ob, s_new = _gla_prompt(proj, gkp, w2_l, bgk_l, lno_l, bp, seq)
        kp.append(kn.reshape(bp, seq, KV_WIDTH_A)[:, -WINDOW:].reshape(bp, WINDOW, N_KV_A, HEAD_DIM))
        vp.append(proj.reshape(bp, seq, MAIN_COLS)[:, -WINDOW:, OFF_VA:OFF_VA + KV_WIDTH_A]
                  .reshape(bp, WINDOW, N_KV_A, HEAD_DIM))
        gp.append(s_new)
        prompt = (oa, ob, gates, xp)

        proj = proj_s
        oa, kn = _attn_sample(proj, win_k, win_v, sk, lq, lk, bias_s, l, bs, t_new)
        ob, s_new = _gla_sample(proj, gks, w2_l, bgk_l, lno_l, state_gla, l, bs, t_new)

        xp, h2, xs, h2_s = _merge(prompt, (oa, ob, gates_s, xs), w_oa_b, w_ob_b, w_out_b, ln2_3, l, TM_MERGE)
        act, tail_v, tail_g, act_s, u_v, u_g = _ffn_up(h2, h2_s, w_up, w_conv, b_conv3, conv_state, l, TM_FFN_UP,
                                                       seq // TM_FFN_UP, t_new)
        xp, hp, gkp = _down(act, w_down_b, xp, ln1_3, w_in_t, l, tm_p, more)
        xs, hs, gks = _down(act_s, w_down_b, xs, ln1_3, w_in_t, l, tm_s, more)
        tails = jnp.concatenate([tail_v, tail_g], axis=-1).reshape(bp, seq // TM_FFN_UP, 8, 2 * D_FF)
        cp.append(tails[:, -1, 8 - (CONV_W - 1):])
        k_new = kn.reshape(bs, t_new, N_KV_A, HEAD_DIM)
        v_new = proj[:, OFF_VA:OFF_VA + KV_WIDTH_A].reshape(bs, t_new, N_KV_A, HEAD_DIM)
        ksm.append(jnp.concatenate([cache_win_k[l][:, t_new:], k_new], axis=1))
        vsm.append(jnp.concatenate([cache_win_v[l][:, t_new:], v_new], axis=1))
        gsm.append(s_new)
        u = jnp.concatenate([u_v, u_g], axis=-1).reshape(bs, t_new, 2 * D_FF)
        csm.append(u[:, t_new - (CONV_W - 1):])

    return (xp.reshape(bp, seq, D_MODEL), xs.reshape(bs, t_new, D_MODEL),
            jnp.stack(kp), jnp.stack(vp), jnp.stack(gp), jnp.stack(cp),
            jnp.stack(ksm), jnp.stack(vsm), jnp.stack(gsm), jnp.stack(csm))
```

```python
import functools
import math

import numpy as np
import jax
import jax.numpy as jnp
from jax import lax
from jax.experimental import pallas as pl
from jax.experimental.pallas import tpu as pltpu

F32 = jnp.float32
BF16 = jnp.bfloat16

D_MODEL = 2048
DEPTH = 4
HEAD_DIM = 64
N_HEADS_A = 16
N_KV_A = 4
WIDTH_A = N_HEADS_A * HEAD_DIM
KV_WIDTH_A = N_KV_A * HEAD_DIM
WINDOW = 128
N_BUCKETS = 32
MAX_DISTANCE = 128
N_HEADS_B = 4
DK_B = 128
DV_B = 256
WIDTH_BK = N_HEADS_B * DK_B
WIDTH_BV = N_HEADS_B * DV_B
GATE_RANK = 16
GATE_TAU = 16.0
D_FF = 5632
CONV_W = 3
EPS = 1e-6

OFF_QA = 0
OFF_KA = OFF_QA + WIDTH_A
OFF_VA = OFF_KA + KV_WIDTH_A
OFF_QB = OFF_VA + KV_WIDTH_A
OFF_KB = OFF_QB + WIDTH_BK
OFF_VB = OFF_KB + WIDTH_BK
OFF_RB = OFF_VB + WIDTH_BV
OFF_GK = OFF_RB + WIDTH_BV
OFF_GA = OFF_GK + GATE_RANK
MAIN_COLS = OFF_GK
GATE_COLS = 2 * D_MODEL

LANES = 128
LOG2E = math.log2(math.e)
NEG_BIG = -1e30
VMEM_LIMIT = 56 * 1024 * 1024

TM_PROMPT = 1024
TM_FFN_UP = 1024
TM_MERGE = 256
TN_MAIN = 1536
TN_GATES = 1024
ATTN_BLOCKS_PER_STEP = 2
TF_UP = 512
GLA_CHUNK_P = 128
GLA_TBLOCK = 512
SAMPLE_GROUP_A = 8
SAMPLE_GROUP_B = 4


def _cparams(sem):
    return pltpu.CompilerParams(dimension_semantics=sem, vmem_limit_bytes=VMEM_LIMIT)


def _dot(a, b):
    return jnp.dot(a, b, preferred_element_type=F32)


def _dot_nt(a, b):
    return lax.dot_general(a, b, (((1,), (1,)), ((), ())), preferred_element_type=F32)


def _dot_tn(a, b):
    return lax.dot_general(a, b, (((0,), (0,)), ((), ())), preferred_element_type=F32)


def _split3(x):
    hi = x.astype(BF16)
    r = x - hi.astype(F32)
    mid = r.astype(BF16)
    lo = (r - mid.astype(F32)).astype(BF16)
    return hi, mid, lo


def _dot_exact_lhs01(m01, x):
    hi, mid, lo = _split3(x)
    return _dot(m01, hi) + _dot(m01, mid) + _dot(m01, lo)


def _rms(x, g):
    ms = jnp.mean(x * x, axis=-1, keepdims=True)
    return x * lax.rsqrt(ms + EPS) * g


def _norm_kernel(x_ref, ln_ref, wgk_ref, h_ref, gk_ref):
    h = _rms(x_ref[...], ln_ref[...]).astype(BF16)
    h_ref[...] = h
    gk_ref[...] = _dot_nt(h, wgk_ref[...].astype(BF16))


def _ln_spec(l, nargs):
    if nargs == 1:
        return pl.BlockSpec((None, 1, D_MODEL), lambda i: (l, 0, 0))
    return pl.BlockSpec((None, 1, D_MODEL), lambda i, j: (l, 0, 0))


def _wgk_spec(l, nargs):
    blk = OFF_GK // GATE_RANK
    if nargs == 1:
        return pl.BlockSpec((None, GATE_RANK, D_MODEL), lambda i: (l, blk, 0))
    return pl.BlockSpec((None, GATE_RANK, D_MODEL), lambda i, j: (l, blk, 0))


def _norm(x, ln1, w_in_t, l, tm):
    r = x.shape[0]
    return pl.pallas_call(
        _norm_kernel,
        grid=(r // tm,),
        in_specs=[pl.BlockSpec((tm, D_MODEL), lambda i: (i, 0)), _ln_spec(l, 1), _wgk_spec(l, 1)],
        out_specs=[pl.BlockSpec((tm, D_MODEL), lambda i: (i, 0)), pl.BlockSpec((tm, GATE_RANK), lambda i: (i, 0))],
        out_shape=[jax.ShapeDtypeStruct((r, D_MODEL), BF16), jax.ShapeDtypeStruct((r, GATE_RANK), F32)],
        compiler_params=_cparams(("parallel",)),
        name="norm",
    )(x, ln1, w_in_t)


def _proj_kernel(h_ref, w_ref, o_ref, wb_scr):
    @pl.when(pl.program_id(1) == 0)
    def _():
        wb_scr[...] = w_ref[...].astype(BF16)

    o_ref[...] = _dot_nt(h_ref[...], wb_scr[...])


def _proj_main(h, w_in_t, l, tm):
    r = h.shape[0]
    tn = TN_MAIN
    return pl.pallas_call(
        _proj_kernel,
        grid=(MAIN_COLS // tn, r // tm),
        in_specs=[
            pl.BlockSpec((tm, D_MODEL), lambda j, i: (i, 0)),
            pl.BlockSpec((None, tn, D_MODEL), lambda j, i: (l, j, 0)),
        ],
        out_specs=pl.BlockSpec((tm, tn), lambda j, i: (i, j)),
        out_shape=jax.ShapeDtypeStruct((r, MAIN_COLS), F32),
        scratch_shapes=[pltpu.VMEM((tn, D_MODEL), BF16)],
        compiler_params=_cparams(("arbitrary", "arbitrary")),
        name="proj_main",
    )(h, w_in_t)


def _bucket_ranges():
    d = np.arange(WINDOW)
    max_exact = N_BUCKETS // 2
    df = np.maximum(d, 1).astype(np.float32)
    large = max_exact + (np.log(df / np.float32(max_exact)) / np.float32(math.log(MAX_DISTANCE / max_exact))
                         * np.float32(N_BUCKETS - max_exact)).astype(np.int32)
    large = np.minimum(large, N_BUCKETS - 1)
    bucket = np.where(d < max_exact, d, large)
    ranges = []
    for b in range(N_BUCKETS):
        idx = np.nonzero(bucket == b)[0]
        if idx.size:
            assert idx[-1] - idx[0] + 1 == idx.size
            ranges.append((b, int(idx[0]), int(idx[-1])))
    return ranges


def _bias_kernel(rel_ref, d_ref, o_ref):
    h = pl.program_id(0)
    d = d_ref[...]
    val = jnp.full(d.shape, NEG_BIG, F32)
    for b, lo, hi in _bucket_ranges():
        val = jnp.where((d >= lo) & (d <= hi), rel_ref[b, h] * LOG2E, val)
    o_ref[...] = val


def _bias_table(rel_bias, dmap):
    mq, nk = dmap.shape
    return pl.pallas_call(
        _bias_kernel,
        grid=(N_HEADS_A,),
        in_specs=[
            pl.BlockSpec(memory_space=pltpu.SMEM),
            pl.BlockSpec((mq, nk), lambda h: (0, 0)),
        ],
        out_specs=pl.BlockSpec((None, mq, nk), lambda h: (h, 0, 0)),
        out_shape=jax.ShapeDtypeStruct((N_HEADS_A, mq, nk), F32),
        compiler_params=_cparams(("parallel",)),
        name="bias_table",
    )(rel_bias, dmap)


def _dmap_prompt(first_block):
    i = np.arange(WINDOW)[:, None]
    j = np.arange(2 * WINDOW)[None, :]
    d = i + WINDOW - j
    ok = (d >= 0) & (d < WINDOW)
    if first_block:
        ok = ok & (j >= WINDOW)
    return np.where(ok, d, -1).astype(np.int32)


def _dmap_sample(t_new, group):
    nk = group * WINDOW + LANES
    rows = np.arange(group * t_new)
    rb, rt = rows // t_new, rows % t_new
    d = np.full((group * t_new, nk), -1, np.int64)
    cols = np.arange(group * WINDOW)
    cb, cs = cols // WINDOW, cols % WINDOW
    dw = rt[:, None] + WINDOW - cs[None, :]
    ok = (rb[:, None] == cb[None, :]) & (dw >= 0) & (dw < WINDOW)
    d[:, :group * WINDOW] = np.where(ok, dw, -1)
    ncols = np.arange(group * t_new)
    nb, nu = ncols // t_new, ncols % t_new
    dn = rt[:, None] - nu[None, :]
    okn = (rb[:, None] == nb[None, :]) & (dn >= 0)
    d[:, group * WINDOW:group * WINDOW + group * t_new] = np.where(okn, dn, -1)
    return d.astype(np.int32)


def _group_ones():
    r = lax.broadcasted_iota(jnp.int32, (LANES, LANES), 0)
    c = lax.broadcasted_iota(jnp.int32, (LANES, LANES), 1)
    low_c = jnp.where(c < HEAD_DIM, 1.0, 0.0)
    return jnp.where(r < HEAD_DIM, low_c, 1.0 - low_c).astype(BF16)


def _head_norm(x, gmat, ln2):
    sq = x * x
    hi = sq.astype(BF16)
    lo = (sq - hi.astype(F32)).astype(BF16)
    ms = (_dot(hi, gmat) + _dot(lo, gmat)) * (1.0 / HEAD_DIM)
    return x * lax.rsqrt(ms + EPS) * ln2


def _split_heads(x):
    lane = lax.broadcasted_iota(jnp.int32, (x.shape[0], LANES), 1)
    low = lane < HEAD_DIM
    lo_parts, hi_parts = [], []
    for c in range(KV_WIDTH_A // LANES):
        xc = x[:, c * LANES:(c + 1) * LANES]
        xr = pltpu.roll(xc, HEAD_DIM, 1)
        lo_parts += [jnp.where(low, xc, 0.0).astype(BF16), jnp.where(low, xr, 0.0).astype(BF16)]
        hi_parts += [jnp.where(low, 0.0, xr).astype(BF16), jnp.where(low, 0.0, xc).astype(BF16)]
    return lo_parts, hi_parts


def _attend(q_cols, k_parts, v_parts, bias_ref, sink_ref):
    mq = q_cols[0].shape[0]
    pairs_per_kv = N_HEADS_A // N_KV_A // 2
    assert pairs_per_kv == 2
    first_rows = lax.broadcasted_iota(jnp.int32, (pairs_per_kv * mq, 1), 0) < mq
    outs = [None] * (N_HEADS_A // 2)
    for kh in range(N_KV_A):
        pcs = [kh * pairs_per_kv + p for p in range(pairs_per_kv)]
        q_stack = jnp.concatenate([q_cols[pc] for pc in pcs], axis=0)
        for half in range(2):
            heads = [2 * pc + half for pc in pcs]
            bias = jnp.concatenate([bias_ref[h] for h in heads], axis=0)
            sink = jnp.where(first_rows, sink_ref[heads[0]], sink_ref[heads[1]]) * LOG2E
            s = _dot_nt(q_stack, k_parts[half][kh]) + bias
            m = jnp.max(s, axis=-1, keepdims=True)
            e = jnp.exp2(s - m)
            den = jnp.sum(e, axis=-1, keepdims=True) + jnp.exp2(sink - m)
            o = _dot(e.astype(BF16), v_parts[half][kh]) * (1.0 / den)
            for p, pc in enumerate(pcs):
                o_p = o[p * mq:(p + 1) * mq, :]
                outs[pc] = o_p if outs[pc] is None else outs[pc] + o_p
    return outs


def _gates_attn_kernel(blocks_per_seq, sink_ref, h_ref, hs_ref, w_ref, q_ref, k_ref, v_ref, lnq_ref, lnk_ref,
                       bias_ref, g_ref, gs_ref, o_ref, kn_ref, wb_scr, kprev_scr, vprev_scr, o_scr, kn_scr):
    j = pl.program_id(0)
    i = pl.program_id(1)
    step = j * pl.num_programs(1) + i

    @pl.when(i == 0)
    def _():
        wb_scr[...] = w_ref[...].astype(BF16)

    @pl.when(i == pl.num_programs(1) - 1)
    def _():
        gs_ref[...] = jax.nn.sigmoid(_dot_nt(hs_ref[...], wb_scr[...])).astype(gs_ref.dtype)

    @pl.when(step == 0)
    def _():
        kprev_scr[...] = jnp.zeros_like(kprev_scr)
        vprev_scr[...] = jnp.zeros_like(vprev_scr)

    gmat = _group_ones()
    lnq = lnq_ref[...]
    lnk = lnk_ref[...]
    scale = HEAD_DIM ** -0.5 * LOG2E
    for t in range(ATTN_BLOCKS_PER_STEP):
        rows = slice(t * WINDOW, (t + 1) * WINDOW)
        block = step * ATTN_BLOCKS_PER_STEP + t
        table = bias_ref.at[jnp.where(block % blocks_per_seq == 0, 0, 1)]
        kn_own = jnp.concatenate(
            [_head_norm(k_ref[rows, c * LANES:(c + 1) * LANES], gmat, lnk) for c in range(KV_WIDTH_A // LANES)],
            axis=1)
        kn_scr[rows, :] = kn_own
        k_own = _split_heads(kn_own)
        v_own = _split_heads(v_ref[rows, :])
        k_parts = [[jnp.concatenate([kprev_scr[half, kh], k_own[half][kh]], axis=0) for kh in range(N_KV_A)]
                   for half in range(2)]
        v_parts = [[jnp.concatenate([vprev_scr[half, kh], v_own[half][kh]], axis=0) for kh in range(N_KV_A)]
                   for half in range(2)]
        q_cols = [(_head_norm(q_ref[rows, c * LANES:(c + 1) * LANES], gmat, lnq) * scale).astype(BF16)
                  for c in range(WIDTH_A // LANES)]
        outs = _attend(q_cols, k_parts, v_parts, table, sink_ref)
        for c, o in enumerate(outs):
            o_scr[rows, c * LANES:(c + 1) * LANES] = o.astype(o_scr.dtype)
        for half in range(2):
            for kh in range(N_KV_A):
                kprev_scr[half, kh] = k_own[half][kh]
                vprev_scr[half, kh] = v_own[half][kh]

    g_ref[...] = jax.nn.sigmoid(_dot_nt(h_ref[...], wb_scr[...])).astype(g_ref.dtype)
    o_ref[...] = o_scr[...]
    kn_ref[...] = kn_scr[...]


def _gates_attn(h, hs, w_in_t, proj, sinks_l, lnq2, lnk2, bias_p, l, tm, seq):
    r = h.shape[0]
    rs = hs.shape[0]
    tn = TN_GATES
    nj, nt = GATE_COLS // tn, r // tm
    rows = ATTN_BLOCKS_PER_STEP * WINDOW
    assert nj * nt * rows == r
    kcol = OFF_KA // KV_WIDTH_A
    vcol = OFF_VA // KV_WIDTH_A
    step = lambda j, i: j * nt + i
    return pl.pallas_call(
        functools.partial(_gates_attn_kernel, seq // WINDOW),
        grid=(nj, nt),
        in_specs=[
            pl.BlockSpec(memory_space=pltpu.SMEM),
            pl.BlockSpec((tm, D_MODEL), lambda j, i: (i, 0)),
            pl.BlockSpec((rs, D_MODEL), lambda j, i: (0, 0)),
            pl.BlockSpec((pl.Squeezed(), pl.Element(tn), pl.Element(D_MODEL)),
                         lambda j, i: (l, pl.multiple_of(OFF_GA + j * tn, GATE_RANK), 0)),
            pl.BlockSpec((rows, WIDTH_A), lambda j, i: (step(j, i), 0)),
            pl.BlockSpec((rows, KV_WIDTH_A), lambda j, i: (step(j, i), kcol)),
            pl.BlockSpec((rows, KV_WIDTH_A), lambda j, i: (step(j, i), vcol)),
            pl.BlockSpec((1, LANES), lambda j, i: (0, 0)),
            pl.BlockSpec((1, LANES), lambda j, i: (0, 0)),
            pl.BlockSpec((2, N_HEADS_A, WINDOW, 2 * WINDOW), lambda j, i: (0, 0, 0, 0)),
        ],
        out_specs=[
            pl.BlockSpec((tm, tn), lambda j, i: (i, j)),
            pl.BlockSpec((rs, tn), lambda j, i: (0, j)),
            pl.BlockSpec((rows, WIDTH_A), lambda j, i: (step(j, i), 0)),
            pl.BlockSpec((rows, KV_WIDTH_A), lambda j, i: (step(j, i), 0)),
        ],
        out_shape=[
            jax.ShapeDtypeStruct((r, GATE_COLS), BF16),
            jax.ShapeDtypeStruct((rs, GATE_COLS), BF16),
            jax.ShapeDtypeStruct((r, WIDTH_A), BF16),
            jax.ShapeDtypeStruct((r, KV_WIDTH_A), F32),
        ],
        scratch_shapes=[
            pltpu.VMEM((tn, D_MODEL), BF16),
            pltpu.VMEM((2, N_KV_A, WINDOW, LANES), BF16),
            pltpu.VMEM((2, N_KV_A, WINDOW, LANES), BF16),
            pltpu.VMEM((rows, WIDTH_A), BF16),
            pltpu.VMEM((rows, KV_WIDTH_A), F32),
        ],
        compiler_params=_cparams(("arbitrary", "arbitrary")),
        name="gates_attn",
    )(sinks_l, h, hs, w_in_t, proj, proj, proj, lnq2, lnk2, bias_p)


def _attn_sample_kernel(sink_ref, q_ref, kn_in_ref, vn_ref, wk_ref, wv_ref, lnq_ref, lnk_ref, bias_ref,
                        o_ref, kn_ref):
    gmat = _group_ones()
    lnq = lnq_ref[...]
    lnk = lnk_ref[...]
    scale = HEAD_DIM ** -0.5 * LOG2E
    rows = q_ref.shape[0]
    g = wk_ref.shape[0]
    kn_new = jnp.concatenate(
        [_head_norm(kn_in_ref[:, c * LANES:(c + 1) * LANES], gmat, lnk) for c in range(KV_WIDTH_A // LANES)], axis=1)
    kn_ref[...] = kn_new
    pad = jnp.zeros((LANES - rows, KV_WIDTH_A), F32)

    def window_rows(ref):
        heads_per_col = LANES // HEAD_DIM
        cols = []
        for c in range(KV_WIDTH_A // LANES):
            blocks = [jnp.concatenate([ref[b, c * heads_per_col + j] for j in range(heads_per_col)], axis=0).T
                      for b in range(g)]
            cols.append(jnp.concatenate(blocks, axis=0))
        return jnp.concatenate(cols, axis=1)

    k = jnp.concatenate([window_rows(wk_ref), kn_new, pad], axis=0)
    v = jnp.concatenate([window_rows(wv_ref), vn_ref[...], pad], axis=0)
    q_cols = [(_head_norm(q_ref[:, c * LANES:(c + 1) * LANES], gmat, lnq) * scale).astype(BF16)
              for c in range(WIDTH_A // LANES)]
    outs = _attend(q_cols, _split_heads(k), _split_heads(v), bias_ref, sink_ref)
    for c, o in enumerate(outs):
        o_ref[:, c * LANES:(c + 1) * LANES] = o.astype(o_ref.dtype)


def _attn_sample(proj, win_k, win_v, sinks_l, lnq2, lnk2, bias_s, l, batch, t_new):
    g = SAMPLE_GROUP_A
    rows = g * t_new
    nk = g * WINDOW + LANES
    kcol = OFF_KA // KV_WIDTH_A
    vcol = OFF_VA // KV_WIDTH_A
    return pl.pallas_call(
        _attn_sample_kernel,
        grid=(batch // g,),
        in_specs=[
            pl.BlockSpec(memory_space=pltpu.SMEM),
            pl.BlockSpec((rows, WIDTH_A), lambda i: (i, 0)),
            pl.BlockSpec((rows, KV_WIDTH_A), lambda i: (i, kcol)),
            pl.BlockSpec((rows, KV_WIDTH_A), lambda i: (i, vcol)),
            pl.BlockSpec((None, g, N_KV_A, HEAD_DIM, WINDOW), lambda i: (l, i, 0, 0, 0)),
            pl.BlockSpec((None, g, N_KV_A, HEAD_DIM, WINDOW), lambda i: (l, i, 0, 0, 0)),
            pl.BlockSpec((1, LANES), lambda i: (0, 0)),
            pl.BlockSpec((1, LANES), lambda i: (0, 0)),
            pl.BlockSpec((N_HEADS_A, rows, nk), lambda i: (0, 0, 0)),
        ],
        out_specs=[
            pl.BlockSpec((rows, WIDTH_A), lambda i: (i, 0)),
            pl.BlockSpec((rows, KV_WIDTH_A), lambda i: (i, 0)),
        ],
        out_shape=[
            jax.ShapeDtypeStruct((batch * t_new, WIDTH_A), BF16),
            jax.ShapeDtypeStruct((batch * t_new, KV_WIDTH_A), F32),
        ],
        compiler_params=_cparams(("parallel",)),
        name="attn_sample",
    )(sinks_l, proj, proj, proj, win_k, win_v, lnq2, lnk2, bias_s)


def _log_decay(gk, w2_ref, bgk_ref):
    x = _dot(gk.astype(BF16), w2_ref[...]) + bgk_ref[...]
    log_sig = jnp.minimum(x, 0.0) - jnp.log(1.0 + jnp.exp(-jnp.abs(x)))
    return log_sig * (1.0 / GATE_TAU)


def _column(row):
    r = lax.broadcasted_iota(jnp.int32, (LANES, LANES), 0)
    c = lax.broadcasted_iota(jnp.int32, (LANES, LANES), 1)
    return jnp.sum(jnp.where(r == c, row, 0.0), axis=1, keepdims=True)


def _gla_out(o, ln_o, rgate):
    return (_rms(o, ln_o) * (rgate * jax.nn.sigmoid(rgate))).astype(BF16)


def _gla_prompt_kernel(q_ref, k_ref, va_ref, vb_ref, ra_ref, rb_ref, gk_ref, w2_ref, bgk_ref, lno_ref,
                       o_ref, s_out_ref, s_scr, b_scr):
    t = pl.program_id(1)
    c_len = GLA_CHUNK_P
    tb = q_ref.shape[0]
    n_chunks = tb // c_len
    half = c_len // 2
    scale = DK_B ** -0.5

    @pl.when(t == 0)
    def _():
        s_scr[...] = jnp.zeros_like(s_scr)

    shift = c_len.bit_length() - 1
    rb = lax.broadcasted_iota(jnp.int32, (tb, tb), 0)
    cb = lax.broadcasted_iota(jnp.int32, (tb, tb), 1)
    same_chunk = lax.shift_right_logical(rb, shift) == lax.shift_right_logical(cb, shift)
    tri_blocks = jnp.where(same_chunk & (rb >= cb), 1.0, 0.0).astype(BF16)
    b_scr[...] = _dot_exact_lhs01(tri_blocks, _log_decay(gk_ref[...], w2_ref, bgk_ref))

    ri = lax.broadcasted_iota(jnp.int32, (c_len, c_len), 0)
    ci = lax.broadcasted_iota(jnp.int32, (c_len, c_len), 1)
    quarter = half // 2
    same_half = jnp.where(ci < half, jnp.where(ri < half, 1, 0), jnp.where(ri < half, 0, 1))
    own_causal = (ri >= ci) & (same_half == 1)
    upper_row = lax.broadcasted_iota(jnp.int32, (c_len, 1), 0) >= half
    ln_o = lno_ref[...]
    v_refs = (va_ref, vb_ref)
    r_refs = (ra_ref, rb_ref)

    def chunk(c, carry):
        rows = pl.ds(pl.multiple_of(c * c_len, c_len), c_len)
        for h in range(N_HEADS_B):
            ks = slice(h * DK_B, (h + 1) * DK_B)
            vs = slice((h % 2) * DV_B, (h % 2 + 1) * DV_B)
            bh = b_scr[rows, ks]
            b_last = bh[c_len - 1:c_len, :]
            b_half = bh[half - 1:half, :]
            q = q_ref[rows, ks] * scale
            k = k_ref[rows, ks]
            v = v_refs[h // 2][rows, vs].astype(BF16)
            q_inter = (q * jnp.exp(bh)).astype(BF16)
            r_own = jnp.where(upper_row, bh[half + quarter - 1:half + quarter, :], bh[quarter - 1:quarter, :])
            a_own = _dot_nt((q * jnp.exp(bh - r_own)).astype(BF16), (k * jnp.exp(r_own - bh)).astype(BF16))
            q_x = jnp.where(upper_row, q * jnp.exp(bh - b_half), 0.0).astype(BF16)
            k_x = jnp.where(upper_row, 0.0, k * jnp.exp(b_half - bh)).astype(BF16)
            a = (jnp.where(own_causal, a_own, 0.0) + _dot_nt(q_x, k_x)).astype(BF16)
            k_state = (k * jnp.exp(b_last - bh)).astype(BF16)
            s_old = s_scr[h]
            o = _dot(a, v) + _dot(q_inter, s_old.astype(BF16))
            s_scr[h] = _column(jnp.exp(b_last)) * s_old + _dot_tn(k_state, v)
            o_ref[rows, h * DV_B:(h + 1) * DV_B] = _gla_out(o, ln_o, r_refs[h // 2][rows, vs])
        return carry

    lax.fori_loop(0, n_chunks, chunk, 0, unroll=4)

    @pl.when(t == pl.num_programs(1) - 1)
    def _():
        s_out_ref[...] = s_scr[...]


def _gla_prompt(proj, gk, w2_l, bgk_l, lno_l, batch, seq):
    tb = GLA_TBLOCK
    nt = seq // tb
    w = 2 * DV_B
    row = lambda b, t: b * nt + t
    spec = lambda col: pl.BlockSpec((tb, w), lambda b, t: (row(b, t), col))
    return pl.pallas_call(
        _gla_prompt_kernel,
        grid=(batch, nt),
        in_specs=[
            spec(OFF_QB // w), spec(OFF_KB // w),
            spec(OFF_VB // w), spec(OFF_VB // w + 1),
            spec(OFF_RB // w), spec(OFF_RB // w + 1),
            pl.BlockSpec((tb, GATE_RANK), lambda b, t: (row(b, t), 0)),
            pl.BlockSpec((GATE_RANK, WIDTH_BK), lambda b, t: (0, 0)),
            pl.BlockSpec((1, WIDTH_BK), lambda b, t: (0, 0)),
            pl.BlockSpec((1, DV_B), lambda b, t: (0, 0)),
        ],
        out_specs=[
            pl.BlockSpec((tb, WIDTH_BV), lambda b, t: (row(b, t), 0)),
            pl.BlockSpec((None, N_HEADS_B, DK_B, DV_B), lambda b, t: (b, 0, 0, 0)),
        ],
        out_shape=[
            jax.ShapeDtypeStruct((batch * seq, WIDTH_BV), BF16),
            jax.ShapeDtypeStruct((batch, N_HEADS_B, DK_B, DV_B), F32),
        ],
        scratch_shapes=[pltpu.VMEM((N_HEADS_B, DK_B, DV_B), F32), pltpu.VMEM((tb, WIDTH_BK), F32)],
        compiler_params=_cparams(("parallel", "arbitrary")),
        name="gla_prompt",
    )(proj, proj, proj, proj, proj, proj, gk, w2_l, bgk_l, lno_l)


def _gla_sample_kernel(t_new, q_ref, k_ref, va_ref, vb_ref, ra_ref, rb_ref, gk_ref, w2_ref, bgk_ref, lno_ref,
                       s_ref, o_ref, s_out_ref):
    rows = q_ref.shape[0]
    n_seq = rows // t_new
    scale = DK_B ** -0.5
    ri = lax.broadcasted_iota(jnp.int32, (rows, rows), 0)
    ci = lax.broadcasted_iota(jnp.int32, (rows, rows), 1)
    same = None
    for sq in range(n_seq):
        lo, hi = sq * t_new, (sq + 1) * t_new
        blk = (ri >= lo) & (ri < hi) & (ci >= lo) & (ci < hi)
        same = blk if same is None else (same | blk)
    causal = same & (ri >= ci)
    tri = jnp.where(causal, 1.0, 0.0).astype(BF16)
    rcol = lax.broadcasted_iota(jnp.int32, (rows, 1), 0)
    ln_o = lno_ref[...]
    v_refs = (va_ref, vb_ref)
    r_refs = (ra_ref, rb_ref)

    g = _log_decay(gk_ref[...], w2_ref, bgk_ref)
    b = _dot_exact_lhs01(tri, g)
    for h in range(N_HEADS_B):
        ks = slice(h * DK_B, (h + 1) * DK_B)
        vs = slice((h % 2) * DV_B, (h % 2 + 1) * DV_B)
        bh = b[:, ks]
        gh = g[:, ks]
        q = q_ref[:, ks] * scale
        k = k_ref[:, ks]
        v = v_refs[h // 2][:, vs].astype(BF16)
        q_inter = q * jnp.exp(bh)
        k_t = (k * jnp.exp(-bh)).astype(BF16)
        a = jnp.where(causal, _dot_nt(q_inter.astype(BF16), k_t), 0.0).astype(BF16)
        o = _dot(a, v)
        for sq in range(n_seq):
            mine = (rcol >= sq * t_new) & (rcol < (sq + 1) * t_new)
            b_last = jnp.sum(jnp.where(mine, gh, 0.0), axis=0, keepdims=True)
            k_state = jnp.where(mine, k * jnp.exp(b_last - bh), 0.0).astype(BF16)
            q_mine = jnp.where(mine, q_inter, 0.0).astype(BF16)
            s_old = s_ref[sq, h]
            o = o + _dot(q_mine, s_old.astype(BF16))
            s_out_ref[sq, h] = _column(jnp.exp(b_last)) * s_old + _dot_tn(k_state, v)
        o_ref[:, h * DV_B:(h + 1) * DV_B] = _gla_out(o, ln_o, r_refs[h // 2][:, vs])


def _gla_sample(proj, gk, w2_l, bgk_l, lno_l, state, l, batch, t_new):
    g = SAMPLE_GROUP_B
    rows = g * t_new
    w = 2 * DV_B
    spec = lambda col: pl.BlockSpec((rows, w), lambda i: (i, col))
    return pl.pallas_call(
        functools.partial(_gla_sample_kernel, t_new),
        grid=(batch // g,),
        in_specs=[
            spec(OFF_QB // w), spec(OFF_KB // w),
            spec(OFF_VB // w), spec(OFF_VB // w + 1),
            spec(OFF_RB // w), spec(OFF_RB // w + 1),
            pl.BlockSpec((rows, GATE_RANK), lambda i: (i, 0)),
            pl.BlockSpec((GATE_RANK, WIDTH_BK), lambda i: (0, 0)),
            pl.BlockSpec((1, WIDTH_BK), lambda i: (0, 0)),
            pl.BlockSpec((1, DV_B), lambda i: (0, 0)),
            pl.BlockSpec((None, g, N_HEADS_B, DK_B, DV_B), lambda i: (l, i, 0, 0, 0)),
        ],
        out_specs=[
            pl.BlockSpec((rows, WIDTH_BV), lambda i: (i, 0)),
            pl.BlockSpec((g, N_HEADS_B, DK_B, DV_B), lambda i: (i, 0, 0, 0)),
        ],
        out_shape=[
            jax.ShapeDtypeStruct((batch * t_new, WIDTH_BV), BF16),
            jax.ShapeDtypeStruct((batch, N_HEADS_B, DK_B, DV_B), F32),
        ],
        compiler_params=_cparams(("parallel",)),
        name="gla_sample",
    )(proj, proj, proj, proj, proj, proj, gk, w2_l, bgk_l, lno_l, state)


def _merge_rows(oa_ref, ob_ref, sa_ref, sb_ref, x_ref, woa_ref, wob_ref, wout_ref, ln_ref, x_out_ref, h_ref):
    a = _dot(oa_ref[...], woa_ref[...])
    b = _dot(ob_ref[...], wob_ref[...])
    mix = (sa_ref[...].astype(F32) * a + sb_ref[...].astype(F32) * b).astype(BF16)
    x_new = x_ref[...] + _dot(mix, wout_ref[...])
    x_out_ref[...] = x_new
    h_ref[...] = _rms(x_new, ln_ref[...]).astype(BF16)


def _merge_kernel(oa_ref, ob_ref, sa_ref, sb_ref, x_ref, oa_s_ref, ob_s_ref, sa_s_ref, sb_s_ref, x_s_ref,
                  woa_ref, wob_ref, wout_ref, ln_ref, x_out_ref, h_ref, x_s_out_ref, h_s_ref):
    _merge_rows(oa_ref, ob_ref, sa_ref, sb_ref, x_ref, woa_ref, wob_ref, wout_ref, ln_ref, x_out_ref, h_ref)

    @pl.when(pl.program_id(0) == pl.num_programs(0) - 1)
    def _():
        _merge_rows(oa_s_ref, ob_s_ref, sa_s_ref, sb_s_ref, x_s_ref, woa_ref, wob_ref, wout_ref, ln_ref,
                    x_s_out_ref, h_s_ref)


def _merge(prompt, sample, w_oa, w_ob, w_out, ln2, l, tm):
    r = prompt[3].shape[0]
    rs = sample[3].shape[0]
    resident = dict(pipeline_mode=pl.Buffered(1))
    row = lambda i: (i, 0)
    fixed = lambda i: (0, 0)
    group_specs = lambda rows, at, at1: [
        pl.BlockSpec((rows, WIDTH_A), at),
        pl.BlockSpec((rows, WIDTH_BV), at),
        pl.BlockSpec((rows, D_MODEL), at),
        pl.BlockSpec((rows, D_MODEL), at1),
        pl.BlockSpec((rows, D_MODEL), at),
    ]
    operands = lambda g: (g[0], g[1], g[2], g[2], g[3])
    return pl.pallas_call(
        _merge_kernel,
        grid=(r // tm,),
        in_specs=group_specs(tm, row, lambda i: (i, 1)) + group_specs(rs, fixed, lambda i: (0, 1)) + [
            pl.BlockSpec((None, WIDTH_A, D_MODEL), lambda i: (l, 0, 0), **resident),
            pl.BlockSpec((None, WIDTH_BV, D_MODEL), lambda i: (l, 0, 0), **resident),
            pl.BlockSpec((None, D_MODEL, D_MODEL), lambda i: (l, 0, 0), **resident),
            _ln_spec(l, 1),
        ],
        out_specs=[pl.BlockSpec((tm, D_MODEL), row), pl.BlockSpec((tm, D_MODEL), row),
                   pl.BlockSpec((rs, D_MODEL), fixed), pl.BlockSpec((rs, D_MODEL), fixed)],
        out_shape=[jax.ShapeDtypeStruct((r, D_MODEL), F32), jax.ShapeDtypeStruct((r, D_MODEL), BF16),
                   jax.ShapeDtypeStruct((rs, D_MODEL), F32), jax.ShapeDtypeStruct((rs, D_MODEL), BF16)],
        compiler_params=_cparams(("arbitrary",)),
        name="merge",
    )(*operands(prompt), *operands(sample), w_oa, w_ob, w_out, ln2)


def _gelu_tanh(x):
    return x * (0.5 * (1.0 + jnp.tanh(math.sqrt(2.0 / math.pi) * (x + 0.044715 * (x * x * x)))))


def _conv3(u, p1, p2, w_ref, b_ref):
    return w_ref[0:1, :] * p2 + w_ref[1:2, :] * p1 + w_ref[2:3, :] * u + b_ref[...]


def _history_selectors(n_seq, t_new):
    hist = CONV_W - 1
    e1 = np.zeros((n_seq * t_new, n_seq * hist), np.float32)
    e2 = np.zeros((n_seq * t_new, n_seq * hist), np.float32)
    for b in range(n_seq):
        for t in range(min(t_new, hist)):
            if t < 1:
                e1[b * t_new + t, b * hist + hist - 1 + t] = 1.0
            e2[b * t_new + t, b * hist + t] = 1.0
    return jnp.asarray(e1, BF16), jnp.asarray(e2, BF16)


def _ffn_up_prompt_kernel(tiles_per_seq, h_ref, wv_ref, wg_ref, cv_ref, cg_ref, bv_ref, bg_ref,
                          act_ref, tv_ref, tg_ref, wvb_scr, wgb_scr, carry_v, carry_g):
    i = pl.program_id(1)

    @pl.when(i == 0)
    def _():
        wvb_scr[...] = wv_ref[...].astype(BF16)
        wgb_scr[...] = wg_ref[...].astype(BF16)

    h = h_ref[...]
    tm = h.shape[0]
    first = (i % tiles_per_seq) == 0
    row8 = lax.broadcasted_iota(jnp.int32, (8, 1), 0)

    def branch(wb_scr, carry, c_ref, b_ref, tail_ref):
        u = _dot(h, wb_scr[...])
        prev = jnp.where(first, 0.0, carry[...])
        tail = u[tm - 8:tm, :]
        carry[...] = tail
        tail_ref[...] = tail
        p1 = pltpu.roll(u, 1, 0)
        p2 = pltpu.roll(u, 2, 0)
        body = _conv3(u, p1, p2, c_ref, b_ref)
        p1_top = jnp.where(row8 < 1, pltpu.roll(prev, 1, 0), p1[0:8, :])
        p2_top = jnp.where(row8 < 2, pltpu.roll(prev, 2, 0), p2[0:8, :])
        top = _conv3(u[0:8, :], p1_top, p2_top, c_ref, b_ref)
        return jnp.concatenate([top, body[8:, :]], axis=0)

    gate = _gelu_tanh(branch(wgb_scr, carry_g, cg_ref, bg_ref, tg_ref))
    val = branch(wvb_scr, carry_v, cv_ref, bv_ref, tv_ref)
    act_ref[...] = (gate * val).astype(act_ref.dtype)


def _ffn_up_sample_kernel(pos_ref, e1_ref, e2_ref, h_ref, wv_ref, wg_ref, cv_ref, cg_ref, bv_ref, bg_ref,
                          sv_ref, sg_ref, act_ref, uv_ref, ug_ref):
    h = h_ref[...]
    pos = pos_ref[...]
    e1 = e1_ref[...]
    e2 = e2_ref[...]

    def branch(w_ref, c_ref, b_ref, s_ref, u_ref):
        u = _dot(h, w_ref[...].astype(BF16))
        u_ref[...] = u
        cached = s_ref[...]
        p1 = jnp.where(pos >= 1, pltpu.roll(u, 1, 0), _dot_exact_lhs01(e1, cached))
        p2 = jnp.where(pos >= 2, pltpu.roll(u, 2, 0), _dot_exact_lhs01(e2, cached))
        return _conv3(u, p1, p2, c_ref, b_ref)

    val = branch(wv_ref, cv_ref, bv_ref, sv_ref, uv_ref)
    gate = branch(wg_ref, cg_ref, bg_ref, sg_ref, ug_ref)
    act_ref[...] = (_gelu_tanh(gate) * val).astype(act_ref.dtype)


def _ffn_up_kernel(tiles_per_seq, pos_ref, e1_ref, e2_ref, h_ref, hs_ref, wv_ref, wg_ref, cv_ref, cg_ref,
                   bv_ref, bg_ref, sv_ref, sg_ref,
                   act_ref, tv_ref, tg_ref, acts_ref, uv_ref, ug_ref, wvb_scr, wgb_scr, carry_v, carry_g):
    _ffn_up_prompt_kernel(tiles_per_seq, h_ref, wv_ref, wg_ref, cv_ref, cg_ref, bv_ref, bg_ref,
                          act_ref, tv_ref, tg_ref, wvb_scr, wgb_scr, carry_v, carry_g)

    @pl.when(pl.program_id(1) == pl.num_programs(1) - 1)
    def _():
        _ffn_up_sample_kernel(pos_ref, e1_ref, e2_ref, hs_ref, wvb_scr, wgb_scr, cv_ref, cg_ref, bv_ref, bg_ref,
                              sv_ref, sg_ref, acts_ref, uv_ref, ug_ref)


def _ffn_up(h, hs, w_up, w_conv, b_conv, conv_state, l, tm, tiles_per_seq, t_new):
    r = h.shape[0]
    rs = hs.shape[0]
    tf = TF_UP
    nt = r // tm
    nj = D_FF // tf
    ns = conv_state.shape[1]
    w_spec = lambda off: pl.BlockSpec((None, D_MODEL, tf), lambda j, i: (l, 0, j + off))
    c_spec = lambda off: pl.BlockSpec((None, CONV_W, tf), lambda j, i: (l, 0, j + off))
    b_spec = lambda off: pl.BlockSpec((None, 1, tf), lambda j, i: (l, 0, j + off))
    s_spec = lambda off: pl.BlockSpec((None, ns, tf), lambda j, i: (l, 0, j + off))
    tail_spec = pl.BlockSpec((None, 8, tf), lambda j, i: (i, 0, j))
    u_spec = pl.BlockSpec((rs, tf), lambda j, i: (0, j))
    fixed = lambda shape: pl.BlockSpec(shape, lambda j, i: (0, 0))
    pos = jnp.asarray((np.arange(rs) % t_new).astype(np.int32).reshape(rs, 1))
    e1, e2 = _history_selectors(rs // t_new, t_new)
    return pl.pallas_call(
        functools.partial(_ffn_up_kernel, tiles_per_seq),
        grid=(nj, nt),
        in_specs=[
            fixed((rs, 1)), fixed((rs, ns)), fixed((rs, ns)),
            pl.BlockSpec((tm, D_MODEL), lambda j, i: (i, 0)),
            fixed((rs, D_MODEL)),
            w_spec(0), w_spec(nj), c_spec(0), c_spec(nj), b_spec(0), b_spec(nj), s_spec(0), s_spec(nj),
        ],
        out_specs=[
            pl.BlockSpec((None, tm, tf), lambda j, i: (j, i, 0)), tail_spec, tail_spec,
            pl.BlockSpec((None, rs, tf), lambda j, i: (j, 0, 0)), u_spec, u_spec,
        ],
        out_shape=[
            jax.ShapeDtypeStruct((nj, r, tf), BF16),
            jax.ShapeDtypeStruct((nt, 8, D_FF), F32),
            jax.ShapeDtypeStruct((nt, 8, D_FF), F32),
            jax.ShapeDtypeStruct((nj, rs, tf), BF16),
            jax.ShapeDtypeStruct((rs, D_FF), F32),
            jax.ShapeDtypeStruct((rs, D_FF), F32),
        ],
        scratch_shapes=[
            pltpu.VMEM((D_MODEL, tf), BF16),
            pltpu.VMEM((D_MODEL, tf), BF16),
            pltpu.VMEM((8, tf), F32),
            pltpu.VMEM((8, tf), F32),
        ],
        compiler_params=_cparams(("arbitrary", "arbitrary")),
        name="ffn_up",
    )(pos, e1, e2, h, hs, w_up, w_up, w_conv, w_conv, b_conv, b_conv, conv_state, conv_state)


def _down_accumulate(act_ref, w_ref, x_ref, x_out_ref):
    @pl.when(pl.program_id(1) == 0)
    def _():
        x_out_ref[...] = x_ref[...]

    act = act_ref[...]
    step = 512
    for n in range(D_MODEL // step):
        cols = slice(n * step, (n + 1) * step)
        x_out_ref[:, cols] += _dot(act, w_ref[:, cols])


def _down_last_kernel(act_ref, w_ref, x_ref, x_out_ref):
    _down_accumulate(act_ref, w_ref, x_ref, x_out_ref)


def _down_norm_kernel(act_ref, w_ref, x_ref, ln_ref, wgk_ref, x_out_ref, h_ref, gk_ref):
    _down_accumulate(act_ref, w_ref, x_ref, x_out_ref)

    @pl.when(pl.program_id(1) == pl.num_programs(1) - 1)
    def _():
        h = _rms(x_out_ref[...], ln_ref[...]).astype(BF16)
        h_ref[...] = h
        gk_ref[...] = _dot_nt(h, wgk_ref[...].astype(BF16))


def _down(act, w_down, x, ln1, w_in_t, l, tm, with_norm):
    nk, r, tk = act.shape
    row = lambda i, k: (i, 0)
    in_specs = [
        pl.BlockSpec((None, tm, tk), lambda i, k: (k, i, 0)),
        pl.BlockSpec((None, tk, D_MODEL), lambda i, k: (l, k, 0)),
        pl.BlockSpec((tm, D_MODEL), row),
    ]
    out_specs = [pl.BlockSpec((tm, D_MODEL), row)]
    out_shape = [jax.ShapeDtypeStruct((r, D_MODEL), F32)]
    args = [act, w_down, x]
    if with_norm:
        in_specs += [_ln_spec(l + 1, 2), _wgk_spec(l + 1, 2)]
        out_specs += [pl.BlockSpec((tm, D_MODEL), row), pl.BlockSpec((tm, GATE_RANK), row)]
        out_shape += [jax.ShapeDtypeStruct((r, D_MODEL), BF16), jax.ShapeDtypeStruct((r, GATE_RANK), F32)]
        args += [ln1, w_in_t]
    out = pl.pallas_call(
        _down_norm_kernel if with_norm else _down_last_kernel,
        grid=(r // tm, nk),
        in_specs=in_specs,
        out_specs=out_specs,
        out_shape=out_shape,
        compiler_params=_cparams(("parallel", "arbitrary")),
        name="down_norm" if with_norm else "down_last",
    )(*args)
    return out if with_norm else (out[0], None, None)


def kernel(x_prompt, x_sample, cache_win_k, cache_win_v, state_gla, state_conv, rel_bias,
           ln1, w_in, ln_q, ln_k, sinks, w_gk2, b_gk, ln_o, w_oa, w_ob, w_out,
           ln2, w_up, w_conv, b_conv, w_down):
    bp, seq, _ = x_prompt.shape
    bs, t_new, _ = x_sample.shape
    tm_p = TM_PROMPT
    tm_s = bs * t_new
    tiles_per_seq = seq // tm_p

    ln1_3 = ln1.reshape(DEPTH, 1, D_MODEL)
    ln2_3 = ln2.reshape(DEPTH, 1, D_MODEL)
    w_in_t = jnp.swapaxes(w_in, 1, 2)
    w_oa_b, w_ob_b, w_out_b, w_down_b = (w.astype(BF16) for w in (w_oa, w_ob, w_out, w_down))
    w2_b = w_gk2.astype(BF16)
    b_conv3 = b_conv.reshape(DEPTH, 1, 2 * D_FF)
    lnq2 = jnp.tile(ln_q, (1, 2)).reshape(DEPTH, 1, LANES)
    lnk2 = jnp.tile(ln_k, (1, 2)).reshape(DEPTH, 1, LANES)

    bias_p = jnp.stack([_bias_table(rel_bias, jnp.asarray(_dmap_prompt(first))) for first in (True, False)])
    bias_s = _bias_table(rel_bias, jnp.asarray(_dmap_sample(t_new, SAMPLE_GROUP_A)))

    xp = x_prompt.reshape(bp * seq, D_MODEL)
    xs = x_sample.reshape(bs * t_new, D_MODEL)
    win_k = jnp.transpose(cache_win_k, (0, 1, 3, 4, 2))
    win_v = jnp.transpose(cache_win_v, (0, 1, 3, 4, 2))
    conv_state = state_conv.reshape(DEPTH, bs * (CONV_W - 1), 2 * D_FF)

    hp, gkp = _norm(xp, ln1_3, w_in_t, 0, 512)
    hs, gks = _norm(xs, ln1_3, w_in_t, 0, tm_s)

    kp, vp, gp, cp = [], [], [], []
    ksm, vsm, gsm, csm = [], [], [], []
    for l in range(DEPTH):
        lq, lk, sk = lnq2[l], lnk2[l], sinks[l]
        w2_l, bgk_l, lno_l = w2_b[l], b_gk[l].reshape(1, WIDTH_BK), ln_o[l].reshape(1, DV_B)
        more = l + 1 < DEPTH

        proj = _proj_main(hp, w_in_t, l, tm_p)
        gates, gates_s, oa, kn = _gates_attn(hp, hs, w_in_t, proj, sk, lq, lk, bias_p, l, tm_p, seq)
        ob, s_new = _gla_prompt(proj, gkp, w2_l, bgk_l, lno_l, bp, seq)
        kp.append(kn.reshape(bp, seq, KV_WIDTH_A)[:, -WINDOW:].reshape(bp, WINDOW, N_KV_A, HEAD_DIM))
        vp.append(proj.reshape(bp, seq, MAIN_COLS)[:, -WINDOW:, OFF_VA:OFF_VA + KV_WIDTH_A]
                  .reshape(bp, WINDOW, N_KV_A, HEAD_DIM))
        gp.append(s_new)
        prompt = (oa, ob, gates, xp)

        proj = _proj_main(hs, w_in_t, l, tm_s)
        oa, kn = _attn_sample(proj, win_k, win_v, sk, lq, lk, bias_s, l, bs, t_new)
        ob, s_new = _gla_sample(proj, gks, w2_l, bgk_l, lno_l, state_gla, l, bs, t_new)

        xp, h2, xs, h2_s = _merge(prompt, (oa, ob, gates_s, xs), w_oa_b, w_ob_b, w_out_b, ln2_3, l, TM_MERGE)
        act, tail_v, tail_g, act_s, u_v, u_g = _ffn_up(h2, h2_s, w_up, w_conv, b_conv3, conv_state, l, TM_FFN_UP,
                                                       seq // TM_FFN_UP, t_new)
        xp, hp, gkp = _down(act, w_down_b, xp, ln1_3, w_in_t, l, tm_p, more)
        xs, hs, gks = _down(act_s, w_down_b, xs, ln1_3, w_in_t, l, tm_s, more)
        tails = jnp.concatenate([tail_v, tail_g], axis=-1).reshape(bp, seq // TM_FFN_UP, 8, 2 * D_FF)
        cp.append(tails[:, -1, 8 - (CONV_W - 1):])
        k_new = kn.reshape(bs, t_new, N_KV_A, HEAD_DIM)
        v_new = proj[:, OFF_VA:OFF_VA + KV_WIDTH_A].reshape(bs, t_new, N_KV_A, HEAD_DIM)
        ksm.append(jnp.concatenate([cache_win_k[l][:, t_new:], k_new], axis=1))
        vsm.append(jnp.concatenate([cache_win_v[l][:, t_new:], v_new], axis=1))
        gsm.append(s_new)
        u = jnp.concatenate([u_v, u_g], axis=-1).reshape(bs, t_new, 2 * D_FF)
        csm.append(u[:, t_new - (CONV_W - 1):])

    return (xp.reshape(bp, seq, D_MODEL), xs.reshape(bs, t_new, D_MODEL),
            jnp.stack(kp), jnp.stack(vp), jnp.stack(gp), jnp.stack(cp),
            jnp.stack(ksm), jnp.stack(vsm), jnp.stack(gsm), jnp.stack(csm))
```
